```python
import jax
import jax.numpy as jnp
from jax import lax
import numpy as np

D_MODEL = 1024
BATCH = 8
SEQ = 8192
DEPTH = 1
DEC_BATCH = 32
DEC_SEQ = 32
PAST_LEN = 4096

CHUNK = 64
N_META = 16
Q_BLOCK = 128
EPS = 1e-6
NEG_INF = -1e30
MLA_HEADS = 8
MLA_Q_LORA = 384
MLA_KV_LORA = 256
MLA_NOPE = 64
MLA_ROPE = 32
MLA_QK_DIM = MLA_NOPE + MLA_ROPE
MLA_V = 64
MLA_WIDTH = MLA_HEADS * MLA_V
MLA_SCALE = MLA_QK_DIM ** -0.5
ROPE_THETA = 10000.0
FOX_HEADS = 8
FOX_HEAD_DIM = 64
FOX_WIDTH = FOX_HEADS * FOX_HEAD_DIM
FOX_SCALE = FOX_HEAD_DIM ** -0.5
FORGET_BIAS_INIT = 3.0
D_MIX = MLA_WIDTH + FOX_WIDTH
IN_SPLIT_SIZES = (MLA_Q_LORA, MLA_KV_LORA, MLA_ROPE, FOX_WIDTH, FOX_WIDTH, FOX_WIDTH, FOX_HEADS)
D_IN = sum(IN_SPLIT_SIZES)
IN_SPLIT_POINTS = tuple(int(v) for v in np.cumsum(IN_SPLIT_SIZES)[:-1])
D_FF = ((8 * D_MODEL + 3 * 256 - 1) // (3 * 256)) * 256

kernel_name = 'hymba_mla_fox_streaming_step'


def rms_norm(x, g):
    xf = x.astype(jnp.float32)
    y = xf * lax.rsqrt(jnp.mean(xf * xf, axis=-1, keepdims=True) + EPS)
    return (y * g.astype(jnp.float32)).astype(x.dtype)


def rope(x, pos):
    half = MLA_ROPE // 2
    inv_freq = ROPE_THETA ** (-jnp.arange(half, dtype=jnp.float32) / half)
    ang = pos.astype(jnp.float32)[:, None] * inv_freq[None, :]
    cos = jnp.cos(ang)[None, :, None, :]
    sin = jnp.sin(ang)[None, :, None, :]
    xf = x.astype(jnp.float32)
    x1, x2 = xf[..., :half], xf[..., half:]
    return jnp.concatenate([x1 * cos - x2 * sin, x2 * cos + x1 * sin], axis=-1).astype(x.dtype)


def attend(q, k, v, scale, valid, bias=None):
    s = jnp.einsum('bqhd,bkhd->bhqk', q, k).astype(jnp.float32) * scale
    if bias is not None:
        s = s + bias
    if valid is not None:
        s = jnp.where(valid, s, NEG_INF)
    p = jax.nn.softmax(s, axis=-1).astype(v.dtype)
    return jnp.einsum('bhqk,bkhd->bqhd', p, v)


def fox_bias(cum_q, cum_k):
    return jnp.transpose(cum_q, (0, 2, 1))[..., :, None] - jnp.transpose(cum_k, (0, 2, 1))[..., None, :]


def project(h, pos, w_in, b_forget, q_norm, w_uq, kv_norm):
    B, L, _ = h.shape
    proj = h @ w_in
    c_q, c_kv, k_r, f_q, f_k, f_v, f_logit = jnp.split(proj, IN_SPLIT_POINTS, axis=-1)
    q = (rms_norm(c_q, q_norm) @ w_uq).reshape(B, L, MLA_HEADS, MLA_QK_DIM)
    q_mla = jnp.concatenate([q[..., :MLA_NOPE], rope(q[..., MLA_NOPE:], pos)], axis=-1)
    latent = rms_norm(c_kv, kv_norm)
    k_rope = rope(k_r[:, :, None, :], pos)[:, :, 0, :]
    heads = lambda a: a.reshape(B, L, FOX_HEADS, FOX_HEAD_DIM)
    log_f = jax.nn.log_sigmoid(f_logit.astype(jnp.float32) + b_forget.astype(jnp.float32))
    return q_mla, latent, k_rope, heads(f_q), heads(f_k), heads(f_v), log_f


def expand_mla_kv(latent, k_rope, w_ukv):
    B, K, _ = latent.shape
    kv = (latent @ w_ukv).reshape(B, K, MLA_HEADS, MLA_NOPE + MLA_V)
    k = jnp.concatenate([kv[..., :MLA_NOPE],
                         jnp.broadcast_to(k_rope[:, :, None, :], (B, K, MLA_HEADS, MLA_ROPE))], axis=-1)
    return k, kv[..., MLA_NOPE:]


def mix_prompt(h, w_in, b_forget, q_norm, w_uq, kv_norm, w_ukv):
    B, L, _ = h.shape
    idx = jnp.arange(L)
    cid = jnp.where(idx < N_META, -1, (idx - N_META) // CHUNK)
    q_mla, latent, k_rope, f_q, f_k, f_v, log_f = project(h, idx, w_in, b_forget, q_norm, w_uq, kv_norm)
    k_mla, v_mla = expand_mla_kv(latent, k_rope, w_ukv)
    cum = jnp.cumsum(log_f, axis=1)

    def block(start):
        take = lambda a: lax.dynamic_slice_in_dim(a, start, Q_BLOCK, axis=1)
        q_idx = start + jnp.arange(Q_BLOCK)
        q_cid = lax.dynamic_slice_in_dim(cid, start, Q_BLOCK, axis=0)
        o_a = attend(take(q_mla), k_mla, v_mla, MLA_SCALE, cid[None, :] <= q_cid[:, None])
        o_b = attend(take(f_q), f_k, f_v, FOX_SCALE, idx[None, :] <= q_idx[:, None],
                     fox_bias(take(cum), cum))
        return jnp.concatenate([o_a.reshape(B, Q_BLOCK, MLA_WIDTH),
                                o_b.reshape(B, Q_BLOCK, FOX_WIDTH)], axis=-1)

    out = lax.map(block, jnp.arange(0, L, Q_BLOCK))
    mixed = jnp.moveaxis(out, 0, 1).reshape(B, L, D_MIX)
    return mixed, (latent, k_rope, f_k, f_v, log_f)


def mix_sample(h, c_lat, c_kr, c_fk, c_fv, c_lf, w_in, b_forget, q_norm, w_uq, kv_norm, w_ukv):
    B, S, _ = h.shape
    past = c_lat.shape[1]
    pos = past + jnp.arange(S)
    q_mla, latent, k_rope, f_q, f_k, f_v, log_f = project(h, pos, w_in, b_forget, q_norm, w_uq, kv_norm)
    k_mla, v_mla = expand_mla_kv(jnp.concatenate([c_lat, latent], axis=1),
                                 jnp.concatenate([c_kr, k_rope], axis=1), w_ukv)
    o_a = attend(q_mla, k_mla, v_mla, MLA_SCALE, None)
    cum = jnp.cumsum(jnp.concatenate([c_lf.astype(jnp.float32), log_f], axis=1), axis=1)
    valid = jnp.arange(past + S)[None, :] <= pos[:, None]
    o_b = attend(f_q, jnp.concatenate([c_fk, f_k], axis=1), jnp.concatenate([c_fv, f_v], axis=1),
                 FOX_SCALE, valid, fox_bias(cum[:, past:], cum))
    mixed = jnp.concatenate([o_a.reshape(B, S, MLA_WIDTH), o_b.reshape(B, S, FOX_WIDTH)], axis=-1)
    return mixed, (latent, k_rope, f_k, f_v, log_f)


def finish_layer(x, mixed, w_out, norm_ffn, w_gate, w_up, w_down):
    x = x + mixed @ w_out
    h = rms_norm(x, norm_ffn)
    return x + (jax.nn.silu(h @ w_gate) * (h @ w_up)) @ w_down


def setup_inputs(seed: int = 0) -> dict:
    key = jax.random.key(seed)
    ks = jax.random.split(key, 22)
    nrm = lambda k, shape, scale=1.0: scale * jax.random.normal(k, shape, jnp.float32)
    gain = lambda k, shape: 1.0 + 0.02 * jax.random.normal(k, shape, jnp.float32)
    return {
        'x_prompt': nrm(ks[0], (BATCH, SEQ, D_MODEL)),
        'x_sample': nrm(ks[1], (DEC_BATCH, DEC_SEQ, D_MODEL)),
        'cache_mla_latent': nrm(ks[2], (DEPTH, DEC_BATCH, PAST_LEN, MLA_KV_LORA)),
        'cache_mla_krope': nrm(ks[3], (DEPTH, DEC_BATCH, PAST_LEN, MLA_ROPE)),
        'cache_fox_k': nrm(ks[4], (DEPTH, DEC_BATCH, PAST_LEN, FOX_HEADS, FOX_HEAD_DIM)),
        'cache_fox_v': nrm(ks[5], (DEPTH, DEC_BATCH, PAST_LEN, FOX_HEADS, FOX_HEAD_DIM)),
        'cache_fox_logf': jax.nn.log_sigmoid(FORGET_BIAS_INIT + nrm(ks[6], (DEPTH, DEC_BATCH, PAST_LEN, FOX_HEADS))),
        'meta_tokens': nrm(ks[7], (N_META, D_MODEL)),
        'norm_mix': gain(ks[8], (DEPTH, D_MODEL)),
        'w_in': nrm(ks[9], (DEPTH, D_MODEL, D_IN), D_MODEL ** -0.5),
        'b_forget': FORGET_BIAS_INIT + 0.5 * nrm(ks[10], (DEPTH, FOX_HEADS)),
        'mla_q_norm': gain(ks[11], (DEPTH, MLA_Q_LORA)),
        'w_mla_uq': nrm(ks[12], (DEPTH, MLA_Q_LORA, MLA_HEADS * MLA_QK_DIM), MLA_Q_LORA ** -0.5),
        'mla_kv_norm': gain(ks[13], (DEPTH, MLA_KV_LORA)),
        'w_mla_ukv': nrm(ks[14], (DEPTH, MLA_KV_LORA, MLA_HEADS * (MLA_NOPE + MLA_V)), MLA_KV_LORA ** -0.5),
        'w_out': nrm(ks[15], (DEPTH, D_MIX, D_MODEL), D_MIX ** -0.5),
        'norm_ffn': gain(ks[16], (DEPTH, D_MODEL)),
        'w_ffn_gate': nrm(ks[17], (DEPTH, D_MODEL, D_FF), D_MODEL ** -0.5),
        'w_ffn_up': nrm(ks[18], (DEPTH, D_MODEL, D_FF), D_MODEL ** -0.5),
        'w_ffn_down': nrm(ks[19], (DEPTH, D_FF, D_MODEL), D_FF ** -0.5),
        'norm_final': gain(ks[20], (D_MODEL,)),
    }


def reference(x_prompt, x_sample, cache_mla_latent, cache_mla_krope, cache_fox_k, cache_fox_v,
              cache_fox_logf, meta_tokens, norm_mix, w_in, b_forget, mla_q_norm, w_mla_uq,
              mla_kv_norm, w_mla_ukv, w_out, norm_ffn, w_ffn_gate, w_ffn_up, w_ffn_down, norm_final):
    B = x_prompt.shape[0]
    L = N_META + x_prompt.shape[1]
    L_pad = -(-L // Q_BLOCK) * Q_BLOCK
    xp = jnp.concatenate([jnp.broadcast_to(meta_tokens[None].astype(x_prompt.dtype), (B, N_META, D_MODEL)),
                          x_prompt], axis=1)
    xp = jnp.pad(xp, ((0, 0), (0, L_pad - L), (0, 0)))
    xs = x_sample
    prompt_rows = []
    sample_rows = []
    for l in range(DEPTH):
        mix_w = (w_in[l], b_forget[l], mla_q_norm[l], w_mla_uq[l], mla_kv_norm[l], w_mla_ukv[l])
        ffn_w = (w_out[l], norm_ffn[l], w_ffn_gate[l], w_ffn_up[l], w_ffn_down[l])
        mixed, rows = mix_prompt(rms_norm(xp, norm_mix[l]), *mix_w)
        xp = finish_layer(xp, mixed, *ffn_w)
        prompt_rows.append(rows)
        mixed, rows = mix_sample(rms_norm(xs, norm_mix[l]), cache_mla_latent[l], cache_mla_krope[l],
                                 cache_fox_k[l], cache_fox_v[l], cache_fox_logf[l], *mix_w)
        xs = finish_layer(xs, mixed, *ffn_w)
        sample_rows.append(rows)
    y_prompt = rms_norm(xp[:, N_META:L], norm_final)
    y_sample = rms_norm(xs, norm_final)
    lat_p, kr_p, fk_p, fv_p, lf_p = [jnp.stack([r[i][:, :L] for r in prompt_rows]) for i in range(5)]
    lat_s, kr_s, fk_s, fv_s, lf_s = [jnp.stack([r[i] for r in sample_rows]) for i in range(5)]
    return (y_prompt, y_sample, lat_p, kr_p, fk_p, fv_p, lf_p, lat_s, kr_s, fk_s, fv_s, lf_s)
```

```python
import functools

import numpy as np
import jax
import jax.numpy as jnp
from jax import lax
from jax.experimental import pallas as pl
from jax.experimental.pallas import tpu as pltpu

CHUNK = 64
EPS = 1e-6
NEG_INF = -1e30
MLA_HEADS = 8
MLA_NOPE = 64
MLA_ROPE = 32
MLA_V = 64
MLA_QK_DIM = MLA_NOPE + MLA_ROPE
MLA_SCALE = MLA_QK_DIM ** -0.5
ROPE_THETA = 10000.0
FOX_HEADS = 8
FOX_HEAD_DIM = 64
FOX_SCALE = FOX_HEAD_DIM ** -0.5
LOG2E = 1.4426950408889634

LANE = 128
HEAD_GROUP = LANE
HEADS = 8
V7X_VMEM_LIMIT = 56 * 1024 * 1024

BF16 = jnp.bfloat16
F32 = jnp.float32

_MLA_Q_LORA = 384
_MLA_KV_LORA = 256
_FOX_WIDTH = FOX_HEADS * FOX_HEAD_DIM
_WIDE = HEADS * HEAD_GROUP
_O_CQ = 0
_O_CKV = _O_CQ + _MLA_Q_LORA
_O_FQW = _O_CKV + _MLA_KV_LORA
_O_FKW = _O_FQW + _WIDE
_O_FK = _O_FKW + _WIDE
_O_FV = _O_FK + _FOX_WIDTH
_O_RA = _O_FV + _FOX_WIDTH
_O_RB = _O_RA + LANE
_O_FL = _O_RB + LANE
_NPROJ = _O_FL + LANE


def _nt_dot(a, b):
    return lax.dot_general(a, b, (((1,), (1,)), ((), ())), preferred_element_type=F32)


def _dot(a, b):
    return jnp.dot(a, b, preferred_element_type=F32)


def _rms(x, g):
    return x * lax.rsqrt(jnp.mean(x * x, axis=-1, keepdims=True) + EPS) * g


def _split3(x):
    hi = x.astype(BF16)
    r = x - hi.astype(F32)
    mid = r.astype(BF16)
    lo = (r - mid.astype(F32)).astype(BF16)
    return hi, mid, lo


def _proj_kernel(x_ref, carry_in_ref, tqc_ref, tqs_ref, tkc_ref, tks_ref,
                 g_ref, w_in_ref, bias_ref, qn_ref, w_uq_ref, kvn_ref, w_kexp_ref, w_uv_ref,
                 tri_ref, place_ref, ones_q_ref,
                 lat_ref, kr_ref, fk_ref, fv_ref, lf_ref,
                 qm_ref, km_ref, vm_ref, fq_ref, fka_ref, fvb_ref, carry_out_ref,
                 carry_scr, *, valid_rows):
    t = pl.program_id(1)

    @pl.when(t == 0)
    def _():
        carry_scr[...] = carry_in_ref[0]

    x = x_ref[0]
    h = _rms(x, g_ref[...]).astype(BF16)
    proj = _dot(h, w_in_ref[...])

    cqn = _rms(proj[:, _O_CQ:_O_CKV], qn_ref[...]).astype(BF16)
    q2 = _dot(cqn, w_uq_ref[...])
    tqc = jnp.tile(tqc_ref[...], (1, HEADS))
    tqs = jnp.tile(tqs_ref[...], (1, HEADS))
    qm_ref[0] = (q2[:, :_WIDE] * tqc + q2[:, _WIDE:] * tqs).astype(BF16)

    latent = _rms(proj[:, _O_CKV:_O_FQW], kvn_ref[...])
    lat_ref[0] = latent
    kr = proj[:, _O_RA:_O_RB] * tkc_ref[...] + proj[:, _O_RB:_O_FL] * tks_ref[...]
    kr_ref[0] = kr[:, :MLA_ROPE]
    lat_bf = latent.astype(BF16)
    kcat = jnp.concatenate([lat_bf, kr.astype(BF16)], axis=1)
    km_ref[0] = _dot(kcat, w_kexp_ref[...]).astype(BF16)
    vm_ref[0] = _dot(lat_bf, w_uv_ref[...]).astype(BF16)

    fk_ref[0] = proj[:, _O_FK:_O_FV]
    fv = proj[:, _O_FV:_O_RA]
    fv_ref[0] = fv
    fvb_ref[0] = fv.astype(BF16)

    z = proj[:, _O_FL:_NPROJ] + bias_ref[...]
    lane = lax.broadcasted_iota(jnp.int32, z.shape, 1)
    keep = lane < FOX_HEADS
    if valid_rows is not None:
        keep = keep & (lax.broadcasted_iota(jnp.int32, z.shape, 0) < valid_rows)
    lf = jnp.where(keep, jnp.minimum(z, 0.0) - jnp.log1p(jnp.exp(-jnp.abs(z))), 0.0)
    lf_ref[0] = lf[:, :FOX_HEADS]
    hi, mid, lo = _split3(lf)
    tri = tri_ref[...]
    cum = _dot(tri, hi) + _dot(tri, mid) + _dot(tri, lo) + carry_scr[...]
    n_rows = cum.shape[0]
    carry_scr[...] = cum[n_rows - 1:n_rows, :]
    carry_out_ref[0] = cum[n_rows - 1:n_rows, :]
    c_hi, c_mid, c_lo = _split3(cum * LOG2E)
    kb = _dot(jnp.concatenate([c_hi, c_mid, c_lo], axis=1), place_ref[...])
    fq_ref[0] = (proj[:, _O_FQW:_O_FKW] * (FOX_SCALE * LOG2E) + ones_q_ref[...]).astype(BF16)
    fka_ref[0] = (proj[:, _O_FKW:_O_FK] + kb).astype(BF16)


def _const_spec(shape):
    zeros = (0,) * len(shape)
    return pl.BlockSpec(shape, lambda *_: zeros, pipeline_mode=pl.Buffered(1))


def _proj_call(x, carry_in, tabs, w, tile, valid_rows=None):
    B, L, D = x.shape
    assert L % tile == 0
    nt = L // tile
    tri = jnp.asarray(np.tril(np.ones((tile, tile), np.float32)), BF16)

    row = lambda width: pl.BlockSpec((1, tile, width), lambda b, t: (b, t, 0))
    tab = pl.BlockSpec((tile, LANE), lambda b, t: (t, 0))
    in_specs = [row(D), pl.BlockSpec((1, 1, LANE), lambda b, t: (b, 0, 0)), tab, tab, tab, tab,
                _const_spec((1, D)), _const_spec((D, _NPROJ)), _const_spec((1, LANE)),
                _const_spec((1, _MLA_Q_LORA)), _const_spec((_MLA_Q_LORA, 2 * _WIDE)),
                _const_spec((1, _MLA_KV_LORA)), _const_spec((_MLA_KV_LORA + LANE, _WIDE)),
                _const_spec((_MLA_KV_LORA, MLA_HEADS * MLA_V)),
                _const_spec((tile, tile)), _const_spec((3 * LANE, _WIDE)), _const_spec((1, _WIDE))]
    out_shapes = [
        jax.ShapeDtypeStruct((B, L, _MLA_KV_LORA), F32),
        jax.ShapeDtypeStruct((B, L, MLA_ROPE), F32),
        jax.ShapeDtypeStruct((B, L, _FOX_WIDTH), F32),
        jax.ShapeDtypeStruct((B, L, _FOX_WIDTH), F32),
        jax.ShapeDtypeStruct((B, L, FOX_HEADS), F32),
        jax.ShapeDtypeStruct((B, L, _WIDE), BF16),
        jax.ShapeDtypeStruct((B, L, _WIDE), BF16),
        jax.ShapeDtypeStruct((B, L, MLA_HEADS * MLA_V), BF16),
        jax.ShapeDtypeStruct((B, L, _WIDE), BF16),
        jax.ShapeDtypeStruct((B, L, _WIDE), BF16),
        jax.ShapeDtypeStruct((B, L, _FOX_WIDTH), BF16),
        jax.ShapeDtypeStruct((B, 1, LANE), F32),
    ]
    out_specs = [row(s.shape[-1]) for s in out_shapes[:-1]]
    out_specs.append(pl.BlockSpec((1, 1, LANE), lambda b, t: (b, 0, 0)))
    return pl.pallas_call(
        functools.partial(_proj_kernel, valid_rows=valid_rows),
        out_shape=out_shapes,
        grid=(B, nt),
        in_specs=in_specs,
        out_specs=out_specs,
        scratch_shapes=[pltpu.VMEM((1, LANE), F32)],
        compiler_params=pltpu.CompilerParams(
            dimension_semantics=("parallel", "arbitrary"), vmem_limit_bytes=V7X_VMEM_LIMIT),
        name="proj",
    )(x, carry_in, *tabs, w["g_mix"], w["w_in"], w["bias_f"], w["q_norm"], w["w_uq2"],
      w["kv_norm"], w["w_kexp"], w["w_uv"], tri, w["place_k"], w["ones_q"])


def _softmax_update(h, s, v_pair, m_scr, l_scr, acc_scr):
    m_prev = m_scr[h]
    m_new = jnp.maximum(m_prev, jnp.max(s, axis=1, keepdims=True))
    alpha = jnp.exp2(m_prev - m_new)
    p = jnp.exp2(s - jnp.tile(m_new, (1, s.shape[1] // LANE)))
    l_scr[h] = alpha * l_scr[h] + jnp.sum(p, axis=1, keepdims=True)
    acc_scr[h] = alpha * acc_scr[h] + _dot(p.astype(BF16), v_pair)
    m_scr[h] = m_new


def _attn_kernel(qi_ref, kj_ref, q_ref, k_ref, v_ref, kmeta_ref, vmeta_ref, o_ref,
                 m_scr, l_scr, acc_scr, *, chunk_mask, n_meta):
    p_id = pl.program_id(1)
    i = qi_ref[p_id]
    j = kj_ref[p_id]
    tq = q_ref.shape[1]
    tk = k_ref.shape[1]

    def head_ops(h):
        q_h = q_ref[0, :, h * HEAD_GROUP:(h + 1) * HEAD_GROUP]
        pair = (h // 2) * LANE
        return q_h, pair

    @pl.when(j == 0)
    def _():
        m_scr[...] = jnp.full(m_scr.shape, NEG_INF, F32)
        l_scr[...] = jnp.zeros(l_scr.shape, F32)
        acc_scr[...] = jnp.zeros(acc_scr.shape, F32)
        col = lax.broadcasted_iota(jnp.int32, (tq, kmeta_ref.shape[0]), 1)
        for h in range(HEADS):
            q_h, pair = head_ops(h)
            s = _nt_dot(q_h, kmeta_ref[:, h * HEAD_GROUP:(h + 1) * HEAD_GROUP])
            s = jnp.where(col < n_meta, s, NEG_INF)
            _softmax_update(h, s, vmeta_ref[:, pair:pair + LANE], m_scr, l_scr, acc_scr)

    def block(masked):
        if masked:
            r = lax.broadcasted_iota(jnp.int32, (tq, tk), 0)
            c = lax.broadcasted_iota(jnp.int32, (tq, tk), 1)
            valid = (c // CHUNK <= r // CHUNK) if chunk_mask else (c <= r)
        for h in range(HEADS):
            q_h, pair = head_ops(h)
            s = _nt_dot(q_h, k_ref[0, :, h * HEAD_GROUP:(h + 1) * HEAD_GROUP])
            if masked:
                s = jnp.where(valid, s, NEG_INF)
            _softmax_update(h, s, v_ref[0, :, pair:pair + LANE], m_scr, l_scr, acc_scr)

    @pl.when(j < i)
    def _():
        block(False)

    @pl.when(j == i)
    def _():
        block(True)
        lane = lax.broadcasted_iota(jnp.int32, (tq, LANE), 1)
        for hp in range(HEADS // 2):
            even = acc_scr[2 * hp] / l_scr[2 * hp]
            odd = acc_scr[2 * hp + 1] / l_scr[2 * hp + 1]
            o_ref[0, :, hp * LANE:(hp + 1) * LANE] = jnp.where(lane < LANE // 2, even, odd).astype(o_ref.dtype)


def _attn_call(q, k, v, k_meta, v_meta, *, chunk_mask, n_meta, tile, name):
    B, F, _ = q.shape
    assert F % tile == 0 and tile % CHUNK == 0
    n = F // tile
    pairs = [(i, j) for i in range(n) for j in range(i + 1)]
    qi = jnp.asarray([p[0] for p in pairs], jnp.int32)
    kj = jnp.asarray([p[1] for p in pairs], jnp.int32)
    vw = v.shape[-1]
    grid_spec = pltpu.PrefetchScalarGridSpec(
        num_scalar_prefetch=2,
        grid=(B, len(pairs)),
        in_specs=[
            pl.BlockSpec((1, tile, _WIDE), lambda b, p, qi, kj: (b, qi[p], 0)),
            pl.BlockSpec((1, tile, _WIDE), lambda b, p, qi, kj: (b, kj[p], 0)),
            pl.BlockSpec((1, tile, vw), lambda b, p, qi, kj: (b, kj[p], 0)),
            pl.BlockSpec((LANE, _WIDE), lambda b, p, qi, kj: (0, 0)),
            pl.BlockSpec((LANE, vw), lambda b, p, qi, kj: (0, 0)),
        ],
        out_specs=pl.BlockSpec((1, tile, vw), lambda b, p, qi, kj: (b, qi[p], 0)),
        scratch_shapes=[pltpu.VMEM((HEADS, tile, LANE), F32),
                        pltpu.VMEM((HEADS, tile, LANE), F32),
                        pltpu.VMEM((HEADS, tile, LANE), F32)],
    )
    return pl.pallas_call(
        functools.partial(_attn_kernel, chunk_mask=chunk_mask, n_meta=n_meta),
        out_shape=jax.ShapeDtypeStruct((B, F, vw), BF16),
        grid_spec=grid_spec,
        compiler_params=pltpu.CompilerParams(
            dimension_semantics=("parallel", "arbitrary"), vmem_limit_bytes=V7X_VMEM_LIMIT),
        name=name,
    )(qi, kj, q, k, v, k_meta, v_meta)


def _ffn_kernel(x_ref, oa_ref, ob_ref, woa_ref, wob_ref, g_ref, wg_ref, wu_ref, wd_ref, gf_ref, y_ref):
    x1 = x_ref[...] + _dot(oa_ref[...], woa_ref[...]) + _dot(ob_ref[...], wob_ref[...])
    h = _rms(x1, g_ref[...]).astype(BF16)
    gate = _dot(h, wg_ref[...])
    up = _dot(h, wu_ref[...])
    act = (gate * jax.nn.sigmoid(gate) * up).astype(BF16)
    x2 = x1 + _dot(act, wd_ref[...])
    y_ref[...] = _rms(x2, gf_ref[...])


def _ffn_call(x, oa, ob, w, tile):
    R, D = x.shape
    tile = min(tile, R)
    assert R % tile == 0
    dm = oa.shape[-1]
    dff = w["w_gate"].shape[-1]
    row = lambda width: pl.BlockSpec((tile, width), lambda r: (r, 0))
    return pl.pallas_call(
        _ffn_kernel,
        out_shape=jax.ShapeDtypeStruct((R, D), F32),
        grid=(R // tile,),
        in_specs=[row(D), row(dm), row(dm),
                  _const_spec((dm, D)), _const_spec((dm, D)), _const_spec((1, D)),
                  _const_spec((D, dff)), _const_spec((D, dff)), _const_spec((dff, D)),
                  _const_spec((1, D))],
        out_specs=row(D),
        compiler_params=pltpu.CompilerParams(
            dimension_semantics=("parallel",), vmem_limit_bytes=V7X_VMEM_LIMIT),
        name="ffn",
    )(x, oa, ob, w["w_out_a"], w["w_out_b"], w["g_ffn"], w["w_gate"], w["w_up"], w["w_down"],
      w["g_final"])


_CUM_CHUNK = 256


def _cumsum_kernel(x_ref, u_ref, o_ref):
    rows, n = x_ref.shape
    u = u_ref[...]
    carry = jnp.zeros((rows, 1), F32)
    for c in range(n // _CUM_CHUNK):
        hi, mid, lo = _split3(x_ref[:, c * _CUM_CHUNK:(c + 1) * _CUM_CHUNK])
        y = _dot(hi, u) + _dot(mid, u) + _dot(lo, u) + carry
        o_ref[:, c * _CUM_CHUNK:(c + 1) * _CUM_CHUNK] = y
        carry = y[:, _CUM_CHUNK - 1:_CUM_CHUNK]


def _cumsum_call(x):
    rows, n = x.shape
    assert n % _CUM_CHUNK == 0
    u = jnp.asarray(np.triu(np.ones((_CUM_CHUNK, _CUM_CHUNK), np.float32)), BF16)
    return pl.pallas_call(
        _cumsum_kernel,
        out_shape=jax.ShapeDtypeStruct((rows, n), F32),
        compiler_params=pltpu.CompilerParams(vmem_limit_bytes=V7X_VMEM_LIMIT),
        name="cache_cumsum",
    )(x, u)


def _sample_fox_kernel(fq_ref, k_ref, v_ref, ck_ref, fkn_ref, fvn_ref, o_ref,
                       qm_scr, m_scr, l_scr, acc_scr):
    j = pl.program_id(1)
    nj = pl.num_programs(1)
    s_new = fq_ref.shape[1]
    lane = lax.broadcasted_iota(jnp.int32, (s_new, LANE), 1)

    @pl.when(j == 0)
    def _():
        m_scr[...] = jnp.full(m_scr.shape, NEG_INF, F32)
        l_scr[...] = jnp.zeros(l_scr.shape, F32)
        acc_scr[...] = jnp.zeros(acc_scr.shape, F32)
        for h in range(HEADS):
            q_h = jnp.where(lane < FOX_HEAD_DIM, fq_ref[0, :, h * HEAD_GROUP:(h + 1) * HEAD_GROUP], 0)
            if h % 2 == 1:
                q_h = pltpu.roll(q_h.astype(F32), FOX_HEAD_DIM, 1).astype(BF16)
            qm_scr[h] = q_h

    ck2 = ck_ref[0] * LOG2E
    for hp in range(HEADS // 2):
        k_pair = k_ref[0, :, hp * LANE:(hp + 1) * LANE].astype(BF16)
        v_pair = v_ref[0, :, hp * LANE:(hp + 1) * LANE].astype(BF16)
        for h in (2 * hp, 2 * hp + 1):
            s = _nt_dot(qm_scr[h], k_pair) - ck2[h:h + 1, :]
            _softmax_update(h, s, v_pair, m_scr, l_scr, acc_scr)

    @pl.when(j == nj - 1)
    def _():
        r = lax.broadcasted_iota(jnp.int32, (s_new, s_new), 0)
        c = lax.broadcasted_iota(jnp.int32, (s_new, s_new), 1)
        for h in range(HEADS):
            s = _nt_dot(fq_ref[0, :, h * HEAD_GROUP:(h + 1) * HEAD_GROUP],
                        fkn_ref[0, :, h * HEAD_GROUP:(h + 1) * HEAD_GROUP])
            s = jnp.where(c <= r, s, NEG_INF)
            m_prev = m_scr[h]
            m_new = jnp.maximum(m_prev, jnp.max(s, axis=1, keepdims=True))
            alpha = jnp.exp2(m_prev - m_new)
            p = jnp.exp2(s - m_new[:, :s_new])
            l_scr[h] = alpha * l_scr[h] + jnp.sum(p, axis=1, keepdims=True)
            pair = (h // 2) * LANE
            acc_scr[h] = alpha * acc_scr[h] + _dot(p.astype(BF16), fvn_ref[0, :, pair:pair + LANE])
        for hp in range(HEADS // 2):
            even = acc_scr[2 * hp] / l_scr[2 * hp]
            odd = acc_scr[2 * hp + 1] / l_scr[2 * hp + 1]
            o_ref[0, :, hp * LANE:(hp + 1) * LANE] = jnp.where(lane < LANE // 2, even, odd).astype(o_ref.dtype)


def _sample_fox_call(fq, cache_k, cache_v, cum_cache, fka_new, fv_new, tile):
    B, S, _ = fq.shape
    past = cache_k.shape[1]
    assert past % tile == 0
    w = cache_k.shape[-1]
    return pl.pallas_call(
        _sample_fox_kernel,
        out_shape=jax.ShapeDtypeStruct((B, S, w), BF16),
        grid=(B, past // tile),
        in_specs=[pl.BlockSpec((1, S, _WIDE), lambda b, j: (b, 0, 0)),
                  pl.BlockSpec((1, tile, w), lambda b, j: (b, j, 0)),
                  pl.BlockSpec((1, tile, w), lambda b, j: (b, j, 0)),
                  pl.BlockSpec((1, HEADS, tile), lambda b, j: (b, 0, j)),
                  pl.BlockSpec((1, S, _WIDE), lambda b, j: (b, 0, 0)),
                  pl.BlockSpec((1, S, w), lambda b, j: (b, 0, 0))],
        out_specs=pl.BlockSpec((1, S, w), lambda b, j: (b, 0, 0)),
        scratch_shapes=[pltpu.VMEM((HEADS, S, LANE), BF16),
                        pltpu.VMEM((HEADS, S, LANE), F32),
                        pltpu.VMEM((HEADS, S, LANE), F32),
                        pltpu.VMEM((HEADS, S, LANE), F32)],
        compiler_params=pltpu.CompilerParams(
            dimension_semantics=("parallel", "arbitrary"), vmem_limit_bytes=V7X_VMEM_LIMIT),
        name="sample_fox",
    )(fq, cache_k, cache_v, cum_cache, fka_new, fv_new)


def _sample_mla_kernel(q_ref, lat_ref, kr_ref, latn_ref, krn_ref, wabs_ref, prope_ref, wv_ref, o_ref,
                       ql_scr, qr_scr, m_scr, l_scr, acc_scr):
    j = pl.program_id(1)
    nj = pl.num_programs(1)
    s_new = q_ref.shape[1]

    @pl.when(j == 0)
    def _():
        m_scr[...] = jnp.full(m_scr.shape, NEG_INF, F32)
        l_scr[...] = jnp.zeros(l_scr.shape, F32)
        acc_scr[...] = jnp.zeros(acc_scr.shape, F32)
        for h in range(HEADS):
            q_h = q_ref[0, :, h * HEAD_GROUP:(h + 1) * HEAD_GROUP]
            ql_scr[h * s_new:(h + 1) * s_new, :] = _dot(q_h, wabs_ref[h]).astype(BF16)
            qr_scr[h * s_new:(h + 1) * s_new, :] = _dot(q_h, prope_ref[...]).astype(BF16)

    def update(lat, kr):
        s = _nt_dot(ql_scr[...], lat) + _nt_dot(qr_scr[...], kr)
        m_prev = m_scr[...]
        m_new = jnp.maximum(m_prev, jnp.max(s, axis=1, keepdims=True))
        alpha = jnp.exp2(m_prev - m_new)
        p = jnp.exp2(s - m_new[:, :1])
        l_scr[...] = alpha * l_scr[...] + jnp.sum(p, axis=1, keepdims=True)
        acc_scr[...] = jnp.tile(alpha, (1, acc_scr.shape[1] // LANE)) * acc_scr[...] + _dot(p.astype(BF16), lat)
        m_scr[...] = m_new

    def pad_rope(kr):
        return jnp.concatenate([kr, jnp.zeros((kr.shape[0], LANE - MLA_ROPE), kr.dtype)], axis=1)

    update(lat_ref[0].astype(BF16), pad_rope(kr_ref[0]).astype(BF16))

    @pl.when(j == nj - 1)
    def _():
        update(latn_ref[0].astype(BF16), pad_rope(krn_ref[0]).astype(BF16))
        o_lat = (acc_scr[...] / jnp.tile(l_scr[...], (1, acc_scr.shape[1] // LANE))).astype(BF16)
        out = _dot(o_lat[0:s_new], wv_ref[0])
        for h in range(1, HEADS):
            out = out + _dot(o_lat[h * s_new:(h + 1) * s_new], wv_ref[h])
        o_ref[0] = out.astype(o_ref.dtype)


def _sample_mla_call(q, cache_lat, cache_kr, lat_new, kr_new, w, tile):
    B, S, _ = q.shape
    past = cache_lat.shape[1]
    assert past % tile == 0
    c = cache_lat.shape[-1]
    ow = MLA_HEADS * MLA_V
    return pl.pallas_call(
        _sample_mla_kernel,
        out_shape=jax.ShapeDtypeStruct((B, S, ow), BF16),
        grid=(B, past // tile),
        in_specs=[pl.BlockSpec((1, S, _WIDE), lambda b, j: (b, 0, 0)),
                  pl.BlockSpec((1, tile, c), lambda b, j: (b, j, 0)),
                  pl.BlockSpec((1, tile, MLA_ROPE), lambda b, j: (b, j, 0)),
                  pl.BlockSpec((1, S, c), lambda b, j: (b, 0, 0)),
                  pl.BlockSpec((1, S, MLA_ROPE), lambda b, j: (b, 0, 0)),
                  _const_spec((HEADS, HEAD_GROUP, c)), _const_spec((HEAD_GROUP, LANE)),
                  _const_spec((HEADS, c, ow))],
        out_specs=pl.BlockSpec((1, S, ow), lambda b, j: (b, 0, 0)),
        scratch_shapes=[pltpu.VMEM((HEADS * S, c), BF16),
                        pltpu.VMEM((HEADS * S, LANE), BF16),
                        pltpu.VMEM((HEADS * S, LANE), F32),
                        pltpu.VMEM((HEADS * S, LANE), F32),
                        pltpu.VMEM((HEADS * S, c), F32)],
        compiler_params=pltpu.CompilerParams(
            dimension_semantics=("parallel", "arbitrary"), vmem_limit_bytes=V7X_VMEM_LIMIT),
        name="sample_mla",
    )(q, cache_lat, cache_kr, lat_new, kr_new, w["w_abs"], w["p_rope"], w["w_v_wide"])


def _gather_cols(w, src):
    src = np.asarray(src)
    picked = jnp.take(w, jnp.asarray(np.maximum(src, 0), jnp.int32), axis=1)
    return picked * jnp.asarray((src >= 0).astype(np.float32))[None, :]


def _prep_weights(norm_mix, w_in, b_forget, mla_q_norm, w_mla_uq, mla_kv_norm, w_mla_ukv, w_out,
                  norm_ffn, w_ffn_gate, w_ffn_up, w_ffn_down, norm_final):
    d_model = w_in.shape[0]
    o_kr = _MLA_Q_LORA + _MLA_KV_LORA
    o_fq = o_kr + MLA_ROPE
    o_fk = o_fq + _FOX_WIDTH
    o_fv = o_fk + _FOX_WIDTH
    o_fl = o_fv + _FOX_WIDTH
    half = MLA_ROPE // 2

    src = -np.ones((_NPROJ,), np.int64)
    src[_O_CQ:_O_CKV] = np.arange(0, _MLA_Q_LORA)
    src[_O_CKV:_O_FQW] = np.arange(_MLA_Q_LORA, o_kr)
    for h in range(FOX_HEADS):
        d = np.arange(FOX_HEAD_DIM)
        src[_O_FQW + h * HEAD_GROUP + d] = o_fq + h * FOX_HEAD_DIM + d
        src[_O_FKW + h * HEAD_GROUP + d] = o_fk + h * FOX_HEAD_DIM + d
    src[_O_FK:_O_FV] = np.arange(o_fk, o_fv)
    src[_O_FV:_O_RA] = np.arange(o_fv, o_fl)
    src[_O_RA:_O_RA + MLA_ROPE] = np.arange(o_kr, o_fq)
    src[_O_RB:_O_RB + half] = np.arange(o_kr + half, o_fq)
    src[_O_RB + half:_O_RB + MLA_ROPE] = np.arange(o_kr, o_kr + half)
    src[_O_FL:_O_FL + FOX_HEADS] = np.arange(o_fl, o_fl + FOX_HEADS)
    w_in_r = _gather_cols(w_in, src).astype(BF16)

    src_a = -np.ones((_WIDE,), np.int64)
    src_b = -np.ones((_WIDE,), np.int64)
    for h in range(MLA_HEADS):
        base = h * MLA_QK_DIM
        src_a[h * HEAD_GROUP + np.arange(MLA_QK_DIM)] = base + np.arange(MLA_QK_DIM)
        src_b[h * HEAD_GROUP + MLA_NOPE + np.arange(half)] = base + MLA_NOPE + half + np.arange(half)
        src_b[h * HEAD_GROUP + MLA_NOPE + half + np.arange(half)] = base + MLA_NOPE + np.arange(half)
    w_uq2 = _gather_cols(w_mla_uq, np.concatenate([src_a, src_b])).astype(BF16)

    src_k = -np.ones((_WIDE,), np.int64)
    src_v = np.zeros((MLA_HEADS * MLA_V,), np.int64)
    for h in range(MLA_HEADS):
        src_k[h * HEAD_GROUP + np.arange(MLA_NOPE)] = h * (MLA_NOPE + MLA_V) + np.arange(MLA_NOPE)
        src_v[h * MLA_V + np.arange(MLA_V)] = h * (MLA_NOPE + MLA_V) + MLA_NOPE + np.arange(MLA_V)
    place_r = np.zeros((LANE, _WIDE), np.float32)
    for h in range(MLA_HEADS):
        place_r[np.arange(MLA_ROPE), h * HEAD_GROUP + MLA_NOPE + np.arange(MLA_ROPE)] = 1.0
    w_uk_wide = _gather_cols(w_mla_ukv, src_k)
    w_kexp = jnp.concatenate([w_uk_wide, jnp.asarray(place_r)], axis=0).astype(BF16)
    w_uv = _gather_cols(w_mla_ukv, src_v).astype(BF16)

    w_uk = w_mla_ukv.reshape(_MLA_KV_LORA, MLA_HEADS, MLA_NOPE + MLA_V)[:, :, :MLA_NOPE]
    w_abs = jnp.zeros((MLA_HEADS, HEAD_GROUP, _MLA_KV_LORA), F32)
    w_abs = w_abs.at[:, :MLA_NOPE, :].set(jnp.transpose(w_uk, (1, 2, 0))).astype(BF16)
    p_rope = np.zeros((HEAD_GROUP, LANE), np.float32)
    p_rope[MLA_NOPE + np.arange(MLA_ROPE), np.arange(MLA_ROPE)] = 1.0
    w_uv_h = w_mla_ukv.reshape(_MLA_KV_LORA, MLA_HEADS, MLA_NOPE + MLA_V)[:, :, MLA_NOPE:]
    eye = jnp.asarray(np.eye(MLA_HEADS, dtype=np.float32))
    w_v_wide = (jnp.transpose(w_uv_h, (1, 0, 2))[:, :, None, :] * eye[:, None, :, None])
    w_v_wide = w_v_wide.reshape(MLA_HEADS, _MLA_KV_LORA, MLA_HEADS * MLA_V).astype(BF16)

    bias_f = jnp.zeros((1, LANE), F32).at[0, :FOX_HEADS].set(b_forget.astype(F32))
    place_k = np.zeros((3 * LANE, _WIDE), np.float32)
    ones_q = np.zeros((1, _WIDE), np.float32)
    for h in range(FOX_HEADS):
        for part in range(3):
            place_k[part * LANE + h, h * HEAD_GROUP + FOX_HEAD_DIM + part] = -1.0
            ones_q[0, h * HEAD_GROUP + FOX_HEAD_DIM + part] = 1.0

    d_mla = MLA_HEADS * MLA_V
    return {
        "g_mix": norm_mix.reshape(1, d_model).astype(F32),
        "w_in": w_in_r,
        "bias_f": bias_f,
        "q_norm": mla_q_norm.reshape(1, -1).astype(F32),
        "w_uq2": w_uq2,
        "kv_norm": mla_kv_norm.reshape(1, -1).astype(F32),
        "w_kexp": w_kexp,
        "w_uv": w_uv,
        "place_k": jnp.asarray(place_k, BF16),
        "ones_q": jnp.asarray(ones_q),
        "w_abs": w_abs,
        "p_rope": jnp.asarray(p_rope, BF16),
        "w_v_wide": w_v_wide,
        "w_out_a": w_out[:d_mla].astype(BF16),
        "w_out_b": w_out[d_mla:].astype(BF16),
        "g_ffn": norm_ffn.reshape(1, d_model).astype(F32),
        "w_gate": w_ffn_gate.astype(BF16),
        "w_up": w_ffn_up.astype(BF16),
        "w_down": w_ffn_down.astype(BF16),
        "g_final": norm_final.reshape(1, d_model).astype(F32),
    }


def _rope_tables(pos):
    half = MLA_ROPE // 2
    inv_freq = ROPE_THETA ** (-jnp.arange(half, dtype=F32) / half)
    ang = pos.astype(F32)[:, None] * inv_freq[None, :]
    cos, sin = jnp.cos(ang), jnp.sin(ang)
    n = pos.shape[0]
    sc = MLA_SCALE * LOG2E
    zeros = lambda w: jnp.zeros((n, w), F32)
    tqc = jnp.concatenate([jnp.full((n, MLA_NOPE), sc, F32), sc * cos, sc * cos,
                           zeros(LANE - MLA_QK_DIM)], axis=1)
    tqs = jnp.concatenate([zeros(MLA_NOPE), -sc * sin, sc * sin, zeros(LANE - MLA_QK_DIM)], axis=1)
    tkc = jnp.concatenate([cos, cos, zeros(LANE - MLA_ROPE)], axis=1)
    tks = jnp.concatenate([-sin, sin, zeros(LANE - MLA_ROPE)], axis=1)
    return tqc, tqs, tkc, tks


def _pad_rows(a, rows):
    return jnp.concatenate([a, jnp.zeros((rows - a.shape[0],) + a.shape[1:], a.dtype)], axis=0)


PROJ_TILE = 256
ATTN_TILE = 512
FFN_TILE = 256
SAMPLE_TILE = 1024


def kernel(x_prompt, x_sample, cache_mla_latent, cache_mla_krope, cache_fox_k, cache_fox_v, cache_fox_logf, meta_tokens, norm_mix, w_in, b_forget, mla_q_norm, w_mla_uq, mla_kv_norm, w_mla_ukv, w_out, norm_ffn, w_ffn_gate, w_ffn_up, w_ffn_down, norm_final):
    depth = w_in.shape[0]
    assert depth == 1, "single-layer trunk: the meta rows' mixing output never reaches an output"
    B, seq, d_model = x_prompt.shape
    SB, s_new, _ = x_sample.shape
    n_meta = meta_tokens.shape[0]
    past = cache_mla_latent.shape[2]
    assert n_meta <= LANE

    w = _prep_weights(norm_mix[0], w_in[0], b_forget[0], mla_q_norm[0], w_mla_uq[0], mla_kv_norm[0],
                      w_mla_ukv[0], w_out[0], norm_ffn[0], w_ffn_gate[0], w_ffn_up[0], w_ffn_down[0],
                      norm_final)

    zero_carry = jnp.zeros((1, 1, LANE), F32)
    meta = _proj_call(_pad_rows(meta_tokens.astype(F32), LANE)[None], zero_carry,
                      _rope_tables(jnp.arange(LANE)), w, LANE, valid_rows=n_meta)
    (m_lat, m_kr, m_fk, m_fv, m_lf, _, m_km, m_vm, _, m_fka, m_fvb, m_carry) = meta
    m_lat, m_kr, m_fk, m_fv, m_lf = (a[:, :n_meta] for a in (m_lat, m_kr, m_fk, m_fv, m_lf))
    frames = _proj_call(x_prompt, jnp.broadcast_to(m_carry, (B, 1, LANE)),
                        _rope_tables(n_meta + jnp.arange(seq)), w, PROJ_TILE)
    (f_lat, f_kr, f_fk, f_fv, f_lf, f_qm, f_km, f_vm, f_fq, f_fka, f_fvb, _) = frames

    o_mla = _attn_call(f_qm, f_km, f_vm, m_km[0], m_vm[0],
                       chunk_mask=True, n_meta=n_meta, tile=ATTN_TILE, name="attn_mla")
    o_fox = _attn_call(f_fq, f_fka, f_fvb, m_fka[0], m_fvb[0],
                       chunk_mask=False, n_meta=n_meta, tile=ATTN_TILE, name="attn_fox")
    y_prompt = _ffn_call(x_prompt.reshape(B * seq, d_model), o_mla.reshape(B * seq, -1),
                         o_fox.reshape(B * seq, -1), w, FFN_TILE).reshape(B, seq, d_model)

    with_meta = lambda m, f: jnp.concatenate([jnp.broadcast_to(m, (B,) + m.shape[1:]), f], axis=1)[None]
    lat_p = with_meta(m_lat, f_lat)
    kr_p = with_meta(m_kr, f_kr)
    fk_p = with_meta(m_fk, f_fk).reshape(1, B, n_meta + seq, FOX_HEADS, FOX_HEAD_DIM)
    fv_p = with_meta(m_fv, f_fv).reshape(1, B, n_meta + seq, FOX_HEADS, FOX_HEAD_DIM)
    lf_p = with_meta(m_lf, f_lf)

    logf_t = jnp.transpose(cache_fox_logf[0].astype(F32), (0, 2, 1))
    cum_cache = _cumsum_call(logf_t.reshape(SB * FOX_HEADS, past)).reshape(SB, FOX_HEADS, past)
    carry_s = jnp.zeros((SB, 1, LANE), F32).at[:, 0, :FOX_HEADS].set(cum_cache[:, :, past - 1])
    samp = _proj_call(x_sample, carry_s, _rope_tables(past + jnp.arange(s_new)), w, s_new)
    (s_lat, s_kr, s_fk, s_fv, s_lf, s_qm, _, _, s_fq, s_fka, s_fvb, _) = samp

    so_mla = _sample_mla_call(s_qm, cache_mla_latent[0], cache_mla_krope[0], s_lat, s_kr, w, SAMPLE_TILE)
    so_fox = _sample_fox_call(s_fq, cache_fox_k[0].reshape(SB, past, _FOX_WIDTH),
                              cache_fox_v[0].reshape(SB, past, _FOX_WIDTH), cum_cache, s_fka, s_fvb,
                              SAMPLE_TILE)
    y_sample = _ffn_call(x_sample.reshape(SB * s_new, d_model), so_mla.reshape(SB * s_new, -1),
                         so_fox.reshape(SB * s_new, -1), w, FFN_TILE).reshape(SB, s_new, d_model)

    heads = lambda a: a.reshape(1, SB, s_new, FOX_HEADS, FOX_HEAD_DIM)
    return (y_prompt, y_sample, lat_p, kr_p, fk_p, fv_p, lf_p,
            s_lat[None], s_kr[None], heads(s_fk), heads(s_fv), s_lf[None])
```

```python
import functools

import numpy as np
import jax
import jax.numpy as jnp
from jax import lax
from jax.experimental import pallas as pl
from jax.experimental.pallas import tpu as pltpu

CHUNK = 64
EPS = 1e-6
NEG_INF = -1e30
MLA_HEADS = 8
MLA_NOPE = 64
MLA_ROPE = 32
MLA_V = 64
MLA_QK_DIM = MLA_NOPE + MLA_ROPE
MLA_SCALE = MLA_QK_DIM ** -0.5
ROPE_THETA = 10000.0
FOX_HEADS = 8
FOX_HEAD_DIM = 64
FOX_SCALE = FOX_HEAD_DIM ** -0.5
LOG2E = 1.4426950408889634

LANE = 128
HEAD_GROUP = LANE
HEADS = 8
V7X_VMEM_LIMIT = 56 * 1024 * 1024

BF16 = jnp.bfloat16
F32 = jnp.float32

_MLA_Q_LORA = 384
_MLA_KV_LORA = 256
_FOX_WIDTH = FOX_HEADS * FOX_HEAD_DIM
_WIDE = HEADS * HEAD_GROUP
_O_CQ = 0
_O_CKV = _O_CQ + _MLA_Q_LORA
_O_FQW = _O_CKV + _MLA_KV_LORA
_O_FKW = _O_FQW + _WIDE
_O_FK = _O_FKW + _WIDE
_O_FV = _O_FK + _FOX_WIDTH
_O_RA = _O_FV + _FOX_WIDTH
_O_RB = _O_RA + LANE
_O_FL = _O_RB + LANE
_NPROJ = _O_FL + LANE


def _nt_dot(a, b):
    return lax.dot_general(a, b, (((1,), (1,)), ((), ())), preferred_element_type=F32)


def _dot(a, b):
    return jnp.dot(a, b, preferred_element_type=F32)


def _rms(x, g):
    return x * lax.rsqrt(jnp.mean(x * x, axis=-1, keepdims=True) + EPS) * g


def _split3(x):
    hi = x.astype(BF16)
    r = x - hi.astype(F32)
    mid = r.astype(BF16)
    lo = (r - mid.astype(F32)).astype(BF16)
    return hi, mid, lo


def _proj_kernel(x_ref, carry_in_ref, tqc_ref, tqs_ref, tkc_ref, tks_ref,
                 g_ref, w_in_ref, bias_ref, qn_ref, w_uq_ref, kvn_ref, w_kexp_ref, w_uv_ref,
                 tri_ref, place_ref, ones_q_ref,
                 lat_ref, kr_ref, fk_ref, fv_ref, lf_ref,
                 qm_ref, km_ref, vm_ref, fq_ref, fka_ref, fvb_ref, carry_out_ref,
                 carry_scr, *, valid_rows, transposed_v):
    t = pl.program_id(1)

    @pl.when(t == 0)
    def _():
        carry_scr[...] = carry_in_ref[0]

    x = x_ref[0]
    h = _rms(x, g_ref[...]).astype(BF16)
    proj = _dot(h, w_in_ref[...])

    cqn = _rms(proj[:, _O_CQ:_O_CKV], qn_ref[...]).astype(BF16)
    q2 = _dot(cqn, w_uq_ref[...])
    tqc = jnp.tile(tqc_ref[...], (1, HEADS))
    tqs = jnp.tile(tqs_ref[...], (1, HEADS))
    qm_ref[0] = (q2[:, :_WIDE] * tqc + q2[:, _WIDE:] * tqs).astype(BF16)

    latent = _rms(proj[:, _O_CKV:_O_FQW], kvn_ref[...])
    lat_ref[0] = latent
    kr = proj[:, _O_RA:_O_RB] * tkc_ref[...] + proj[:, _O_RB:_O_FL] * tks_ref[...]
    kr_ref[0] = kr[:, :MLA_ROPE]
    lat_bf = latent.astype(BF16)
    kcat = jnp.concatenate([lat_bf, kr.astype(BF16)], axis=1)
    km_ref[0] = _dot(kcat, w_kexp_ref[...]).astype(BF16)
    v_mla = _dot(lat_bf, w_uv_ref[...])
    vm_ref[0] = (v_mla.T if transposed_v else v_mla).astype(BF16)

    fk_ref[0] = proj[:, _O_FK:_O_FV]
    fv = proj[:, _O_FV:_O_RA]
    fv_ref[0] = fv
    fvb_ref[0] = (fv.T if transposed_v else fv).astype(BF16)

    z = proj[:, _O_FL:_NPROJ] + bias_ref[...]
    lane = lax.broadcasted_iota(jnp.int32, z.shape, 1)
    keep = lane < FOX_HEADS
    if valid_rows is not None:
        keep = keep & (lax.broadcasted_iota(jnp.int32, z.shape, 0) < valid_rows)
    lf = jnp.where(keep, jnp.minimum(z, 0.0) - jnp.log1p(jnp.exp(-jnp.abs(z))), 0.0)
    lf_ref[0] = lf[:, :FOX_HEADS]
    hi, mid, lo = _split3(lf)
    tri = tri_ref[...]
    cum = _dot(tri, hi) + _dot(tri, mid) + _dot(tri, lo) + carry_scr[...]
    n_rows = cum.shape[0]
    carry_scr[...] = cum[n_rows - 1:n_rows, :]
    carry_out_ref[0] = cum[n_rows - 1:n_rows, :]
    c_hi, c_mid, c_lo = _split3(cum * LOG2E)
    kb = _dot(jnp.concatenate([c_hi, c_mid, c_lo], axis=1), place_ref[...])
    fq_ref[0] = (proj[:, _O_FQW:_O_FKW] * (FOX_SCALE * LOG2E) + ones_q_ref[...]).astype(BF16)
    fka_ref[0] = (proj[:, _O_FKW:_O_FK] + kb).astype(BF16)


def _const_spec(shape):
    zeros = (0,) * len(shape)
    return pl.BlockSpec(shape, lambda *_: zeros, pipeline_mode=pl.Buffered(1))


def _proj_call(x, carry_in, tabs, w, tile, valid_rows=None, transposed_v=False):
    B, L, D = x.shape
    assert L % tile == 0
    nt = L // tile
    tri = jnp.asarray(np.tril(np.ones((tile, tile), np.float32)), BF16)

    row = lambda width: pl.BlockSpec((1, tile, width), lambda b, t: (b, t, 0))
    tab = pl.BlockSpec((tile, LANE), lambda b, t: (t, 0))
    in_specs = [row(D), pl.BlockSpec((1, 1, LANE), lambda b, t: (b, 0, 0)), tab, tab, tab, tab,
                _const_spec((1, D)), _const_spec((D, _NPROJ)), _const_spec((1, LANE)),
                _const_spec((1, _MLA_Q_LORA)), _const_spec((_MLA_Q_LORA, 2 * _WIDE)),
                _const_spec((1, _MLA_KV_LORA)), _const_spec((_MLA_KV_LORA + LANE, _WIDE)),
                _const_spec((_MLA_KV_LORA, MLA_HEADS * MLA_V)),
                _const_spec((tile, tile)), _const_spec((3 * LANE, _WIDE)), _const_spec((1, _WIDE))]
    out_shapes = [
        jax.ShapeDtypeStruct((B, L, _MLA_KV_LORA), F32),
        jax.ShapeDtypeStruct((B, L, MLA_ROPE), F32),
        jax.ShapeDtypeStruct((B, L, _FOX_WIDTH), F32),
        jax.ShapeDtypeStruct((B, L, _FOX_WIDTH), F32),
        jax.ShapeDtypeStruct((B, L, FOX_HEADS), F32),
        jax.ShapeDtypeStruct((B, L, _WIDE), BF16),
        jax.ShapeDtypeStruct((B, L, _WIDE), BF16),
        jax.ShapeDtypeStruct((B, L, MLA_HEADS * MLA_V), BF16),
        jax.ShapeDtypeStruct((B, L, _WIDE), BF16),
        jax.ShapeDtypeStruct((B, L, _WIDE), BF16),
        jax.ShapeDtypeStruct((B, L, _FOX_WIDTH), BF16),
        jax.ShapeDtypeStruct((B, 1, LANE), F32),
    ]
    out_specs = [row(s.shape[-1]) for s in out_shapes[:-1]]
    out_specs.append(pl.BlockSpec((1, 1, LANE), lambda b, t: (b, 0, 0)))
    if transposed_v:
        for idx in (7, 10):
            vw = out_shapes[idx].shape[-1]
            out_shapes[idx] = jax.ShapeDtypeStruct((B, vw, L), BF16)
            out_specs[idx] = pl.BlockSpec((1, vw, tile), lambda b, t: (b, 0, t))
    return pl.pallas_call(
        functools.partial(_proj_kernel, valid_rows=valid_rows, transposed_v=transposed_v),
        out_shape=out_shapes,
        grid=(B, nt),
        in_specs=in_specs,
        out_specs=out_specs,
        scratch_shapes=[pltpu.VMEM((1, LANE), F32)],
        compiler_params=pltpu.CompilerParams(
            dimension_semantics=("parallel", "arbitrary"), vmem_limit_bytes=V7X_VMEM_LIMIT),
        name="proj",
    )(x, carry_in, *tabs, w["g_mix"], w["w_in"], w["bias_f"], w["q_norm"], w["w_uq2"],
      w["kv_norm"], w["w_kexp"], w["w_uv"], tri, w["place_k"], w["ones_q"])


def _softmax_update(h, s, v_pair, m_scr, l_scr, acc_scr):
    m_prev = m_scr[h]
    m_new = jnp.maximum(m_prev, jnp.max(s, axis=1, keepdims=True))
    alpha = jnp.exp2(m_prev - m_new)
    p = jnp.exp2(s - jnp.tile(m_new, (1, s.shape[1] // LANE)))
    l_scr[h] = alpha * l_scr[h] + jnp.sum(p, axis=1, keepdims=True)
    acc_scr[h] = alpha * acc_scr[h] + _dot(p.astype(BF16), v_pair)
    m_scr[h] = m_new


QK_LOOKAHEAD = 8
UNIT_Q = 256
UNIT_K = 256
ONES_ROWS = 16


def _with_ones_rows(v_t):
    row = lax.broadcasted_iota(jnp.int32, (ONES_ROWS, v_t.shape[1]), 0)
    return jnp.concatenate([v_t, jnp.where(row == 0, 1.0, 0.0).astype(v_t.dtype)], axis=0)


def _softmax_update_t(h, qs, s_t, v_t, m_scr, acc_scr):
    m_prev = m_scr[h, :, qs]
    m_new = jnp.maximum(m_prev, jnp.max(s_t, axis=0, keepdims=True))
    alpha = jnp.exp2(m_prev - m_new)
    p_t = jnp.exp2(s_t - m_new)
    acc_scr[h, :, qs] = alpha * acc_scr[h, :, qs] + _dot(v_t, p_t.astype(BF16))
    m_scr[h, :, qs] = m_new


def _attn_kernel(qi_ref, kj_ref, q_ref, k_ref, vt_ref, kmeta_ref, vtmeta_ref, o_ref,
                 m_scr, acc_scr, *, chunk_mask, n_meta):
    p_id = pl.program_id(1)
    i = qi_ref[p_id]
    j = kj_ref[p_id]
    tq = q_ref.shape[1]
    tk = k_ref.shape[1]
    dv = acc_scr.shape[1] - ONES_ROWS

    def head(ref2d, h):
        return ref2d[:, h * HEAD_GROUP:(h + 1) * HEAD_GROUP]

    @pl.when(j == 0)
    def _():
        m_scr[...] = jnp.full(m_scr.shape, NEG_INF, F32)
        acc_scr[...] = jnp.zeros(acc_scr.shape, F32)
        row = lax.broadcasted_iota(jnp.int32, (kmeta_ref.shape[0], tq), 0)
        for h in range(HEADS):
            s_t = _nt_dot(head(kmeta_ref, h), head(q_ref.at[0], h))
            s_t = jnp.where(row < n_meta, s_t, NEG_INF)
            _softmax_update_t(h, slice(0, tq), s_t, _with_ones_rows(vtmeta_ref[h * dv:(h + 1) * dv, :]),
                              m_scr, acc_scr)

    uq = min(UNIT_Q, tq)
    uk = min(UNIT_K, tk)

    def block(masked):
        units = [(h, q0, k0) for h in range(HEADS) for k0 in range(0, tk, uk) for q0 in range(0, tq, uq)]
        if masked:
            units = [u for u in units if u[2] < u[1] + uq]
            kk = lax.broadcasted_iota(jnp.int32, (uk, uq), 0)
            qq = lax.broadcasted_iota(jnp.int32, (uk, uq), 1)

        def qk(u):
            h, q0, k0 = u
            return _nt_dot(k_ref[0, k0:k0 + uk, h * HEAD_GROUP:(h + 1) * HEAD_GROUP],
                           q_ref[0, q0:q0 + uq, h * HEAD_GROUP:(h + 1) * HEAD_GROUP])

        scores = {}
        for n in range(-QK_LOOKAHEAD, len(units)):
            if n + QK_LOOKAHEAD < len(units):
                scores[n + QK_LOOKAHEAD] = qk(units[n + QK_LOOKAHEAD])
            if n >= 0:
                h, q0, k0 = units[n]
                s_t = scores.pop(n)
                if masked and k0 + uk > q0 + 1:
                    if chunk_mask:
                        valid = (kk + k0) // CHUNK <= (qq + q0) // CHUNK
                    else:
                        valid = kk + k0 <= qq + q0
                    s_t = jnp.where(valid, s_t, NEG_INF)
                v_t = _with_ones_rows(vt_ref[0, h * dv:(h + 1) * dv, k0:k0 + uk])
                _softmax_update_t(h, slice(q0, q0 + uq), s_t, v_t, m_scr, acc_scr)

    @pl.when(j < i)
    def _():
        block(False)

    @pl.when(j == i)
    def _():
        block(True)
        def normalised(h):
            return acc_scr[h, :dv, :] / acc_scr[h, dv:dv + 1, :]

        for hp in range(HEADS // 2):
            o_pair_t = jnp.concatenate([normalised(2 * hp), normalised(2 * hp + 1)], axis=0)
            o_ref[0, :, hp * 2 * dv:(hp + 1) * 2 * dv] = o_pair_t.T.astype(o_ref.dtype)


def _attn_call(q, k, v_t, k_meta, v_t_meta, *, chunk_mask, n_meta, tile, name):
    B, F, _ = q.shape
    assert F % tile == 0 and tile % CHUNK == 0
    n = F // tile
    pairs = [(i, j) for i in range(n) for j in range(i + 1)]
    qi = jnp.asarray([p[0] for p in pairs], jnp.int32)
    kj = jnp.asarray([p[1] for p in pairs], jnp.int32)
    vw = v_t.shape[1]
    grid_spec = pltpu.PrefetchScalarGridSpec(
        num_scalar_prefetch=2,
        grid=(B, len(pairs)),
        in_specs=[
            pl.BlockSpec((1, tile, _WIDE), lambda b, p, qi, kj: (b, qi[p], 0)),
            pl.BlockSpec((1, tile, _WIDE), lambda b, p, qi, kj: (b, kj[p], 0)),
            pl.BlockSpec((1, vw, tile), lambda b, p, qi, kj: (b, 0, kj[p])),
            pl.BlockSpec((LANE, _WIDE), lambda b, p, qi, kj: (0, 0)),
            pl.BlockSpec((vw, LANE), lambda b, p, qi, kj: (0, 0)),
        ],
        out_specs=pl.BlockSpec((1, tile, vw), lambda b, p, qi, kj: (b, qi[p], 0)),
        scratch_shapes=[pltpu.VMEM((HEADS, 1, tile), F32),
                        pltpu.VMEM((HEADS, vw // HEADS + ONES_ROWS, tile), F32)],
    )
    return pl.pallas_call(
        functools.partial(_attn_kernel, chunk_mask=chunk_mask, n_meta=n_meta),
        out_shape=jax.ShapeDtypeStruct((B, F, vw), BF16),
        grid_spec=grid_spec,
        compiler_params=pltpu.CompilerParams(
            dimension_semantics=("parallel", "arbitrary"), vmem_limit_bytes=V7X_VMEM_LIMIT),
        name=name,
    )(qi, kj, q, k, v_t, k_meta, v_t_meta)


def _ffn_kernel(x_ref, oa_ref, ob_ref, woa_ref, wob_ref, g_ref, wg_ref, wu_ref, wd_ref, gf_ref, y_ref):
    x1 = x_ref[...] + _dot(oa_ref[...], woa_ref[...]) + _dot(ob_ref[...], wob_ref[...])
    h = _rms(x1, g_ref[...]).astype(BF16)
    gate = _dot(h, wg_ref[...])
    up = _dot(h, wu_ref[...])
    act = (gate * jax.nn.sigmoid(gate) * up).astype(BF16)
    x2 = x1 + _dot(act, wd_ref[...])
    y_ref[...] = _rms(x2, gf_ref[...])


def _ffn_call(x, oa, ob, w, tile):
    R, D = x.shape
    tile = min(tile, R)
    assert R % tile == 0
    dm = oa.shape[-1]
    dff = w["w_gate"].shape[-1]
    row = lambda width: pl.BlockSpec((tile, width), lambda r: (r, 0))
    return pl.pallas_call(
        _ffn_kernel,
        out_shape=jax.ShapeDtypeStruct((R, D), F32),
        grid=(R // tile,),
        in_specs=[row(D), row(dm), row(dm),
                  _const_spec((dm, D)), _const_spec((dm, D)), _const_spec((1, D)),
                  _const_spec((D, dff)), _const_spec((D, dff)), _const_spec((dff, D)),
                  _const_spec((1, D))],
        out_specs=row(D),
        compiler_params=pltpu.CompilerParams(
            dimension_semantics=("parallel",), vmem_limit_bytes=V7X_VMEM_LIMIT),
        name="ffn",
    )(x, oa, ob, w["w_out_a"], w["w_out_b"], w["g_ffn"], w["w_gate"], w["w_up"], w["w_down"],
      w["g_final"])


_CUM_CHUNK = 256


def _cumsum_kernel(x_ref, u_ref, o_ref):
    rows, n = x_ref.shape
    u = u_ref[...]
    carry = jnp.zeros((rows, 1), F32)
    for c in range(n // _CUM_CHUNK):
        hi, mid, lo = _split3(x_ref[:, c * _CUM_CHUNK:(c + 1) * _CUM_CHUNK])
        y = _dot(hi, u) + _dot(mid, u) + _dot(lo, u) + carry
        o_ref[:, c * _CUM_CHUNK:(c + 1) * _CUM_CHUNK] = y
        carry = y[:, _CUM_CHUNK - 1:_CUM_CHUNK]


def _cumsum_call(x):
    rows, n = x.shape
    assert n % _CUM_CHUNK == 0
    u = jnp.asarray(np.triu(np.ones((_CUM_CHUNK, _CUM_CHUNK), np.float32)), BF16)
    return pl.pallas_call(
        _cumsum_kernel,
        out_shape=jax.ShapeDtypeStruct((rows, n), F32),
        compiler_params=pltpu.CompilerParams(vmem_limit_bytes=V7X_VMEM_LIMIT),
        name="cache_cumsum",
    )(x, u)


def _sample_fox_kernel(fq_ref, k_ref, v_ref, ck_ref, fkn_ref, fvn_ref, o_ref,
                       qm_scr, m_scr, l_scr, acc_scr):
    j = pl.program_id(1)
    nj = pl.num_programs(1)
    s_new = fq_ref.shape[1]
    lane = lax.broadcasted_iota(jnp.int32, (s_new, LANE), 1)

    @pl.when(j == 0)
    def _():
        m_scr[...] = jnp.full(m_scr.shape, NEG_INF, F32)
        l_scr[...] = jnp.zeros(l_scr.shape, F32)
        acc_scr[...] = jnp.zeros(acc_scr.shape, F32)
        for h in range(HEADS):
            q_h = jnp.where(lane < FOX_HEAD_DIM, fq_ref[0, :, h * HEAD_GROUP:(h + 1) * HEAD_GROUP], 0)
            if h % 2 == 1:
                q_h = pltpu.roll(q_h.astype(F32), FOX_HEAD_DIM, 1).astype(BF16)
            qm_scr[h] = q_h

    ck2 = ck_ref[0] * LOG2E
    for hp in range(HEADS // 2):
        k_pair = k_ref[0, :, hp * LANE:(hp + 1) * LANE].astype(BF16)
        v_pair = v_ref[0, :, hp * LANE:(hp + 1) * LANE].astype(BF16)
        for h in (2 * hp, 2 * hp + 1):
            s = _nt_dot(qm_scr[h], k_pair) - ck2[h:h + 1, :]
            _softmax_update(h, s, v_pair, m_scr, l_scr, acc_scr)

    @pl.when(j == nj - 1)
    def _():
        r = lax.broadcasted_iota(jnp.int32, (s_new, s_new), 0)
        c = lax.broadcasted_iota(jnp.int32, (s_new, s_new), 1)
        for h in range(HEADS):
            s = _nt_dot(fq_ref[0, :, h * HEAD_GROUP:(h + 1) * HEAD_GROUP],
                        fkn_ref[0, :, h * HEAD_GROUP:(h + 1) * HEAD_GROUP])
            s = jnp.where(c <= r, s, NEG_INF)
            m_prev = m_scr[h]
            m_new = jnp.maximum(m_prev, jnp.max(s, axis=1, keepdims=True))
            alpha = jnp.exp2(m_prev - m_new)
            p = jnp.exp2(s - m_new[:, :s_new])
            l_scr[h] = alpha * l_scr[h] + jnp.sum(p, axis=1, keepdims=True)
            pair = (h // 2) * LANE
            acc_scr[h] = alpha * acc_scr[h] + _dot(p.astype(BF16), fvn_ref[0, :, pair:pair + LANE])
        for hp in range(HEADS // 2):
            even = acc_scr[2 * hp] / l_scr[2 * hp]
            odd = acc_scr[2 * hp + 1] / l_scr[2 * hp + 1]
            o_ref[0, :, hp * LANE:(hp + 1) * LANE] = jnp.where(lane < LANE // 2, even, odd).astype(o_ref.dtype)


def _sample_fox_call(fq, cache_k, cache_v, cum_cache, fka_new, fv_new, tile):
    B, S, _ = fq.shape
    past = cache_k.shape[1]
    assert past % tile == 0
    w = cache_k.shape[-1]
    return pl.pallas_call(
        _sample_fox_kernel,
        out_shape=jax.ShapeDtypeStruct((B, S, w), BF16),
        grid=(B, past // tile),
        in_specs=[pl.BlockSpec((1, S, _WIDE), lambda b, j: (b, 0, 0)),
                  pl.BlockSpec((1, tile, w), lambda b, j: (b, j, 0)),
                  pl.BlockSpec((1, tile, w), lambda b, j: (b, j, 0)),
                  pl.BlockSpec((1, HEADS, tile), lambda b, j: (b, 0, j)),
                  pl.BlockSpec((1, S, _WIDE), lambda b, j: (b, 0, 0)),
                  pl.BlockSpec((1, S, w), lambda b, j: (b, 0, 0))],
        out_specs=pl.BlockSpec((1, S, w), lambda b, j: (b, 0, 0)),
        scratch_shapes=[pltpu.VMEM((HEADS, S, LANE), BF16),
                        pltpu.VMEM((HEADS, S, LANE), F32),
                        pltpu.VMEM((HEADS, S, LANE), F32),
                        pltpu.VMEM((HEADS, S, LANE), F32)],
        compiler_params=pltpu.CompilerParams(
            dimension_semantics=("parallel", "arbitrary"), vmem_limit_bytes=V7X_VMEM_LIMIT),
        name="sample_fox",
    )(fq, cache_k, cache_v, cum_cache, fka_new, fv_new)


def _sample_mla_kernel(q_ref, lat_ref, kr_ref, latn_ref, krn_ref, wabs_ref, prope_ref, wv_ref, o_ref,
                       ql_scr, qr_scr, m_scr, l_scr, acc_scr):
    j = pl.program_id(1)
    nj = pl.num_programs(1)
    s_new = q_ref.shape[1]

    @pl.when(j == 0)
    def _():
        m_scr[...] = jnp.full(m_scr.shape, NEG_INF, F32)
        l_scr[...] = jnp.zeros(l_scr.shape, F32)
        acc_scr[...] = jnp.zeros(acc_scr.shape, F32)
        for h in range(HEADS):
            q_h = q_ref[0, :, h * HEAD_GROUP:(h + 1) * HEAD_GROUP]
            ql_scr[h * s_new:(h + 1) * s_new, :] = _dot(q_h, wabs_ref[h]).astype(BF16)
            qr_scr[h * s_new:(h + 1) * s_new, :] = _dot(q_h, prope_ref[...]).astype(BF16)

    def update(lat, kr):
        s = _nt_dot(ql_scr[...], lat) + _nt_dot(qr_scr[...], kr)
        m_prev = m_scr[...]
        m_new = jnp.maximum(m_prev, jnp.max(s, axis=1, keepdims=True))
        alpha = jnp.exp2(m_prev - m_new)
        p = jnp.exp2(s - m_new[:, :1])
        l_scr[...] = alpha * l_scr[...] + jnp.sum(p, axis=1, keepdims=True)
        acc_scr[...] = jnp.tile(alpha, (1, acc_scr.shape[1] // LANE)) * acc_scr[...] + _dot(p.astype(BF16), lat)
        m_scr[...] = m_new

    def pad_rope(kr):
        return jnp.concatenate([kr, jnp.zeros((kr.shape[0], LANE - MLA_ROPE), kr.dtype)], axis=1)

    update(lat_ref[0].astype(BF16), pad_rope(kr_ref[0]).astype(BF16))

    @pl.when(j == nj - 1)
    def _():
        update(latn_ref[0].astype(BF16), pad_rope(krn_ref[0]).astype(BF16))
        o_lat = (acc_scr[...] / jnp.tile(l_scr[...], (1, acc_scr.shape[1] // LANE))).astype(BF16)
        out = _dot(o_lat[0:s_new], wv_ref[0])
        for h in range(1, HEADS):
            out = out + _dot(o_lat[h * s_new:(h + 1) * s_new], wv_ref[h])
        o_ref[0] = out.astype(o_ref.dtype)


def _sample_mla_call(q, cache_lat, cache_kr, lat_new, kr_new, w, tile):
    B, S, _ = q.shape
    past = cache_lat.shape[1]
    assert past % tile == 0
    c = cache_lat.shape[-1]
    ow = MLA_HEADS * MLA_V
    return pl.pallas_call(
        _sample_mla_kernel,
        out_shape=jax.ShapeDtypeStruct((B, S, ow), BF16),
        grid=(B, past // tile),
        in_specs=[pl.BlockSpec((1, S, _WIDE), lambda b, j: (b, 0, 0)),
                  pl.BlockSpec((1, tile, c), lambda b, j: (b, j, 0)),
                  pl.BlockSpec((1, tile, MLA_ROPE), lambda b, j: (b, j, 0)),
                  pl.BlockSpec((1, S, c), lambda b, j: (b, 0, 0)),
                  pl.BlockSpec((1, S, MLA_ROPE), lambda b, j: (b, 0, 0)),
                  _const_spec((HEADS, HEAD_GROUP, c)), _const_spec((HEAD_GROUP, LANE)),
                  _const_spec((HEADS, c, ow))],
        out_specs=pl.BlockSpec((1, S, ow), lambda b, j: (b, 0, 0)),
        scratch_shapes=[pltpu.VMEM((HEADS * S, c), BF16),
                        pltpu.VMEM((HEADS * S, LANE), BF16),
                        pltpu.VMEM((HEADS * S, LANE), F32),
                        pltpu.VMEM((HEADS * S, LANE), F32),
                        pltpu.VMEM((HEADS * S, c), F32)],
        compiler_params=pltpu.CompilerParams(
            dimension_semantics=("parallel", "arbitrary"), vmem_limit_bytes=V7X_VMEM_LIMIT),
        name="sample_mla",
    )(q, cache_lat, cache_kr, lat_new, kr_new, w["w_abs"], w["p_rope"], w["w_v_wide"])


def _gather_cols(w, src):
    src = np.asarray(src)
    picked = jnp.take(w, jnp.asarray(np.maximum(src, 0), jnp.int32), axis=1)
    return picked * jnp.asarray((src >= 0).astype(np.float32))[None, :]


def _prep_weights(norm_mix, w_in, b_forget, mla_q_norm, w_mla_uq, mla_kv_norm, w_mla_ukv, w_out,
                  norm_ffn, w_ffn_gate, w_ffn_up, w_ffn_down, norm_final):
    d_model = w_in.shape[0]
    o_kr = _MLA_Q_LORA + _MLA_KV_LORA
    o_fq = o_kr + MLA_ROPE
    o_fk = o_fq + _FOX_WIDTH
    o_fv = o_fk + _FOX_WIDTH
    o_fl = o_fv + _FOX_WIDTH
    half = MLA_ROPE // 2

    src = -np.ones((_NPROJ,), np.int64)
    src[_O_CQ:_O_CKV] = np.arange(0, _MLA_Q_LORA)
    src[_O_CKV:_O_FQW] = np.arange(_MLA_Q_LORA, o_kr)
    for h in range(FOX_HEADS):
        d = np.arange(FOX_HEAD_DIM)
        src[_O_FQW + h * HEAD_GROUP + d] = o_fq + h * FOX_HEAD_DIM + d
        src[_O_FKW + h * HEAD_GROUP + d] = o_fk + h * FOX_HEAD_DIM + d
    src[_O_FK:_O_FV] = np.arange(o_fk, o_fv)
    src[_O_FV:_O_RA] = np.arange(o_fv, o_fl)
    src[_O_RA:_O_RA + MLA_ROPE] = np.arange(o_kr, o_fq)
    src[_O_RB:_O_RB + half] = np.arange(o_kr + half, o_fq)
    src[_O_RB + half:_O_RB + MLA_ROPE] = np.arange(o_kr, o_kr + half)
    src[_O_FL:_O_FL + FOX_HEADS] = np.arange(o_fl, o_fl + FOX_HEADS)
    w_in_r = _gather_cols(w_in, src).astype(BF16)

    src_a = -np.ones((_WIDE,), np.int64)
    src_b = -np.ones((_WIDE,), np.int64)
    for h in range(MLA_HEADS):
        base = h * MLA_QK_DIM
        src_a[h * HEAD_GROUP + np.arange(MLA_QK_DIM)] = base + np.arange(MLA_QK_DIM)
        src_b[h * HEAD_GROUP + MLA_NOPE + np.arange(half)] = base + MLA_NOPE + half + np.arange(half)
        src_b[h * HEAD_GROUP + MLA_NOPE + half + np.arange(half)] = base + MLA_NOPE + np.arange(half)
    w_uq2 = _gather_cols(w_mla_uq, np.concatenate([src_a, src_b])).astype(BF16)

    src_k = -np.ones((_WIDE,), np.int64)
    src_v = np.zeros((MLA_HEADS * MLA_V,), np.int64)
    for h in range(MLA_HEADS):
        src_k[h * HEAD_GROUP + np.arange(MLA_NOPE)] = h * (MLA_NOPE + MLA_V) + np.arange(MLA_NOPE)
        src_v[h * MLA_V + np.arange(MLA_V)] = h * (MLA_NOPE + MLA_V) + MLA_NOPE + np.arange(MLA_V)
    place_r = np.zeros((LANE, _WIDE), np.float32)
    for h in range(MLA_HEADS):
        place_r[np.arange(MLA_ROPE), h * HEAD_GROUP + MLA_NOPE + np.arange(MLA_ROPE)] = 1.0
    w_uk_wide = _gather_cols(w_mla_ukv, src_k)
    w_kexp = jnp.concatenate([w_uk_wide, jnp.asarray(place_r)], axis=0).astype(BF16)
    w_uv = _gather_cols(w_mla_ukv, src_v).astype(BF16)

    w_uk = w_mla_ukv.reshape(_MLA_KV_LORA, MLA_HEADS, MLA_NOPE + MLA_V)[:, :, :MLA_NOPE]
    w_abs = jnp.zeros((MLA_HEADS, HEAD_GROUP, _MLA_KV_LORA), F32)
    w_abs = w_abs.at[:, :MLA_NOPE, :].set(jnp.transpose(w_uk, (1, 2, 0))).astype(BF16)
    p_rope = np.zeros((HEAD_GROUP, LANE), np.float32)
    p_rope[MLA_NOPE + np.arange(MLA_ROPE), np.arange(MLA_ROPE)] = 1.0
    w_uv_h = w_mla_ukv.reshape(_MLA_KV_LORA, MLA_HEADS, MLA_NOPE + MLA_V)[:, :, MLA_NOPE:]
    eye = jnp.asarray(np.eye(MLA_HEADS, dtype=np.float32))
    w_v_wide = (jnp.transpose(w_uv_h, (1, 0, 2))[:, :, None, :] * eye[:, None, :, None])
    w_v_wide = w_v_wide.reshape(MLA_HEADS, _MLA_KV_LORA, MLA_HEADS * MLA_V).astype(BF16)

    bias_f = jnp.zeros((1, LANE), F32).at[0, :FOX_HEADS].set(b_forget.astype(F32))
    place_k = np.zeros((3 * LANE, _WIDE), np.float32)
    ones_q = np.zeros((1, _WIDE), np.float32)
    for h in range(FOX_HEADS):
        for part in range(3):
            place_k[part * LANE + h, h * HEAD_GROUP + FOX_HEAD_DIM + part] = -1.0
            ones_q[0, h * HEAD_GROUP + FOX_HEAD_DIM + part] = 1.0

    d_mla = MLA_HEADS * MLA_V
    return {
        "g_mix": norm_mix.reshape(1, d_model).astype(F32),
        "w_in": w_in_r,
        "bias_f": bias_f,
        "q_norm": mla_q_norm.reshape(1, -1).astype(F32),
        "w_uq2": w_uq2,
        "kv_norm": mla_kv_norm.reshape(1, -1).astype(F32),
        "w_kexp": w_kexp,
        "w_uv": w_uv,
        "place_k": jnp.asarray(place_k, BF16),
        "ones_q": jnp.asarray(ones_q),
        "w_abs": w_abs,
        "p_rope": jnp.asarray(p_rope, BF16),
        "w_v_wide": w_v_wide,
        "w_out_a": w_out[:d_mla].astype(BF16),
        "w_out_b": w_out[d_mla:].astype(BF16),
        "g_ffn": norm_ffn.reshape(1, d_model).astype(F32),
        "w_gate": w_ffn_gate.astype(BF16),
        "w_up": w_ffn_up.astype(BF16),
        "w_down": w_ffn_down.astype(BF16),
        "g_final": norm_final.reshape(1, d_model).astype(F32),
    }


def _rope_tables(pos):
    half = MLA_ROPE // 2
    inv_freq = ROPE_THETA ** (-jnp.arange(half, dtype=F32) / half)
    ang = pos.astype(F32)[:, None] * inv_freq[None, :]
    cos, sin = jnp.cos(ang), jnp.sin(ang)
    n = pos.shape[0]
    sc = MLA_SCALE * LOG2E
    zeros = lambda w: jnp.zeros((n, w), F32)
    tqc = jnp.concatenate([jnp.full((n, MLA_NOPE), sc, F32), sc * cos, sc * cos,
                           zeros(LANE - MLA_QK_DIM)], axis=1)
    tqs = jnp.concatenate([zeros(MLA_NOPE), -sc * sin, sc * sin, zeros(LANE - MLA_QK_DIM)], axis=1)
    tkc = jnp.concatenate([cos, cos, zeros(LANE - MLA_ROPE)], axis=1)
    tks = jnp.concatenate([-sin, sin, zeros(LANE - MLA_ROPE)], axis=1)
    return tqc, tqs, tkc, tks


def _pad_rows(a, rows):
    return jnp.concatenate([a, jnp.zeros((rows - a.shape[0],) + a.shape[1:], a.dtype)], axis=0)


PROJ_TILE = 256
ATTN_TILE = 512
FFN_TILE = 256
SAMPLE_TILE = 1024


def kernel(x_prompt, x_sample, cache_mla_latent, cache_mla_krope, cache_fox_k, cache_fox_v, cache_fox_logf, meta_tokens, norm_mix, w_in, b_forget, mla_q_norm, w_mla_uq, mla_kv_norm, w_mla_ukv, w_out, norm_ffn, w_ffn_gate, w_ffn_up, w_ffn_down, norm_final):
    depth = w_in.shape[0]
    assert depth == 1, "single-layer trunk: the meta rows' mixing output never reaches an output"
    B, seq, d_model = x_prompt.shape
    SB, s_new, _ = x_sample.shape
    n_meta = meta_tokens.shape[0]
    past = cache_mla_latent.shape[2]
    assert n_meta <= LANE

    w = _prep_weights(norm_mix[0], w_in[0], b_forget[0], mla_q_norm[0], w_mla_uq[0], mla_kv_norm[0],
                      w_mla_ukv[0], w_out[0], norm_ffn[0], w_ffn_gate[0], w_ffn_up[0], w_ffn_down[0],
                      norm_final)

    zero_carry = jnp.zeros((1, 1, LANE), F32)
    meta = _proj_call(_pad_rows(meta_tokens.astype(F32), LANE)[None], zero_carry,
                      _rope_tables(jnp.arange(LANE)), w, LANE, valid_rows=n_meta, transposed_v=True)
    (m_lat, m_kr, m_fk, m_fv, m_lf, _, m_km, m_vm, _, m_fka, m_fvb, m_carry) = meta
    m_lat, m_kr, m_fk, m_fv, m_lf = (a[:, :n_meta] for a in (m_lat, m_kr, m_fk, m_fv, m_lf))
    frames = _proj_call(x_prompt, jnp.broadcast_to(m_carry, (B, 1, LANE)),
                        _rope_tables(n_meta + jnp.arange(seq)), w, PROJ_TILE, transposed_v=True)
    (f_lat, f_kr, f_fk, f_fv, f_lf, f_qm, f_km, f_vm, f_fq, f_fka, f_fvb, _) = frames

    o_mla = _attn_call(f_qm, f_km, f_vm, m_km[0], m_vm[0],
                       chunk_mask=True, n_meta=n_meta, tile=ATTN_TILE, name="attn_mla")
    o_fox = _attn_call(f_fq, f_fka, f_fvb, m_fka[0], m_fvb[0],
                       chunk_mask=False, n_meta=n_meta, tile=ATTN_TILE, name="attn_fox")
    y_prompt = _ffn_call(x_prompt.reshape(B * seq, d_model), o_mla.reshape(B * seq, -1),
                         o_fox.reshape(B * seq, -1), w, FFN_TILE).reshape(B, seq, d_model)

    with_meta = lambda m, f: jnp.concatenate([jnp.broadcast_to(m, (B,) + m.shape[1:]), f], axis=1)[None]
    lat_p = with_meta(m_lat, f_lat)
    kr_p = with_meta(m_kr, f_kr)
    fk_p = with_meta(m_fk, f_fk).reshape(1, B, n_meta + seq, FOX_HEADS, FOX_HEAD_DIM)
    fv_p = with_meta(m_fv, f_fv).reshape(1, B, n_meta + seq, FOX_HEADS, FOX_HEAD_DIM)
    lf_p = with_meta(m_lf, f_lf)

    logf_t = jnp.transpose(cache_fox_logf[0].astype(F32), (0, 2, 1))
    cum_cache = _cumsum_call(logf_t.reshape(SB * FOX_HEADS, past)).reshape(SB, FOX_HEADS, past)
    carry_s = jnp.zeros((SB, 1, LANE), F32).at[:, 0, :FOX_HEADS].set(cum_cache[:, :, past - 1])
    samp = _proj_call(x_sample, carry_s, _rope_tables(past + jnp.arange(s_new)), w, s_new)
    (s_lat, s_kr, s_fk, s_fv, s_lf, s_qm, _, _, s_fq, s_fka, s_fvb, _) = samp

    so_mla = _sample_mla_call(s_qm, cache_mla_latent[0], cache_mla_krope[0], s_lat, s_kr, w, SAMPLE_TILE)
    so_fox = _sample_fox_call(s_fq, cache_fox_k[0].reshape(SB, past, _FOX_WIDTH),
                              cache_fox_v[0].reshape(SB, past, _FOX_WIDTH), cum_cache, s_fka, s_fvb,
                              SAMPLE_TILE)
    y_sample = _ffn_call(x_sample.reshape(SB * s_new, d_model), so_mla.reshape(SB * s_new, -1),
                         so_fox.reshape(SB * s_new, -1), w, FFN_TILE).reshape(SB, s_new, d_model)

    heads = lambda a: a.reshape(1, SB, s_new, FOX_HEADS, FOX_HEAD_DIM)
    return (y_prompt, y_sample, lat_p, kr_p, fk_p, fv_p, lf_p,
            s_lat[None], s_kr[None], heads(s_fk), heads(s_fv), s_lf[None])
```

```python
import functools

import numpy as np
import jax
import jax.numpy as jnp
from jax import lax
from jax.experimental import pallas as pl
from jax.experimental.pallas import tpu as pltpu

CHUNK = 64
EPS = 1e-6
NEG_INF = -1e30
MLA_HEADS = 8
MLA_NOPE = 64
MLA_ROPE = 32
MLA_V = 64
MLA_QK_DIM = MLA_NOPE + MLA_ROPE
MLA_SCALE = MLA_QK_DIM ** -0.5
ROPE_THETA = 10000.0
FOX_HEADS = 8
FOX_HEAD_DIM = 64
FOX_SCALE = FOX_HEAD_DIM ** -0.5
LOG2E = 1.4426950408889634

LANE = 128
SUBLANE = 8
HEAD_GROUP = LANE
HEADS = 8
V7X_VMEM_LIMIT = 56 * 1024 * 1024

BF16 = jnp.bfloat16
F32 = jnp.float32

_MLA_Q_LORA = 384
_MLA_KV_LORA = 256
_FOX_WIDTH = FOX_HEADS * FOX_HEAD_DIM
_WIDE = HEADS * HEAD_GROUP
_O_CQ = 0
_O_CKV = _O_CQ + _MLA_Q_LORA
_O_FQW = _O_CKV + _MLA_KV_LORA
_O_FKW = _O_FQW + _WIDE
_O_FK = _O_FKW + _WIDE
_O_FV = _O_FK + _FOX_WIDTH
_O_RA = _O_FV + _FOX_WIDTH
_O_RB = _O_RA + LANE
_O_FL = _O_RB + LANE
_NPROJ = _O_FL + LANE


def _nt_dot(a, b):
    return lax.dot_general(a, b, (((1,), (1,)), ((), ())), preferred_element_type=F32)


def _dot(a, b):
    return jnp.dot(a, b, preferred_element_type=F32)


def _rms(x, g):
    return x * lax.rsqrt(jnp.mean(x * x, axis=-1, keepdims=True) + EPS) * g


def _split3(x):
    hi = x.astype(BF16)
    r = x - hi.astype(F32)
    mid = r.astype(BF16)
    lo = (r - mid.astype(F32)).astype(BF16)
    return hi, mid, lo


def _proj_kernel(x_ref, carry_in_ref, tqc_ref, tqs_ref, tkc_ref, tks_ref,
                 g_ref, w_in_ref, bias_ref, qn_ref, w_uq_ref, kvn_ref, w_kexp_ref, w_uv_ref,
                 tri_ref, place_ref, ones_q_ref,
                 lat_ref, kr_ref, fk_ref, fv_ref, lf_ref,
                 qm_ref, km_ref, vm_ref, fq_ref, fka_ref, fvb_ref, carry_out_ref,
                 carry_scr, *, valid_rows, transposed_v):
    t = pl.program_id(1)

    @pl.when(t == 0)
    def _():
        carry_scr[...] = carry_in_ref[0]

    x = x_ref[0]
    h = _rms(x, g_ref[...]).astype(BF16)
    proj = _dot(h, w_in_ref[...])

    cqn = _rms(proj[:, _O_CQ:_O_CKV], qn_ref[...]).astype(BF16)
    q2 = _dot(cqn, w_uq_ref[...])
    tqc = jnp.tile(tqc_ref[...], (1, HEADS))
    tqs = jnp.tile(tqs_ref[...], (1, HEADS))
    qm_ref[0] = (q2[:, :_WIDE] * tqc + q2[:, _WIDE:] * tqs).astype(BF16)

    latent = _rms(proj[:, _O_CKV:_O_FQW], kvn_ref[...])
    def put(ref, val):
        ref[...] = val.reshape(ref.shape)

    put(lat_ref, latent)
    kr = proj[:, _O_RA:_O_RB] * tkc_ref[...] + proj[:, _O_RB:_O_FL] * tks_ref[...]
    put(kr_ref, kr[:, :MLA_ROPE])
    lat_bf = latent.astype(BF16)
    kcat = jnp.concatenate([lat_bf, kr.astype(BF16)], axis=1)
    km_ref[0] = _dot(kcat, w_kexp_ref[...]).astype(BF16)
    v_mla = _dot(lat_bf, w_uv_ref[...])
    vm_ref[0] = (v_mla.T if transposed_v else v_mla).astype(BF16)

    put(fk_ref, proj[:, _O_FK:_O_FV])
    fv = proj[:, _O_FV:_O_RA]
    put(fv_ref, fv)
    fvb_ref[0] = (fv.T if transposed_v else fv).astype(BF16)

    z = proj[:, _O_FL:_NPROJ] + bias_ref[...]
    lane = lax.broadcasted_iota(jnp.int32, z.shape, 1)
    keep = lane < FOX_HEADS
    if valid_rows is not None:
        keep = keep & (lax.broadcasted_iota(jnp.int32, z.shape, 0) < valid_rows)
    lf = jnp.where(keep, jnp.minimum(z, 0.0) - jnp.log1p(jnp.exp(-jnp.abs(z))), 0.0)
    put(lf_ref, lf[:, :FOX_HEADS])
    hi, mid, lo = _split3(lf)
    tri = tri_ref[...]
    cum = _dot(tri, hi) + _dot(tri, mid) + _dot(tri, lo) + carry_scr[...]
    n_rows = cum.shape[0]
    carry_scr[...] = cum[n_rows - 1:n_rows, :]
    carry_out_ref[0] = cum[n_rows - 1:n_rows, :]
    c_hi, c_mid, c_lo = _split3(cum * LOG2E)
    kb = _dot(jnp.concatenate([c_hi, c_mid, c_lo], axis=1), place_ref[...])
    fq_ref[0] = (proj[:, _O_FQW:_O_FKW] * (FOX_SCALE * LOG2E) + ones_q_ref[...]).astype(BF16)
    fka_ref[0] = (proj[:, _O_FKW:_O_FK] + kb).astype(BF16)


def _const_spec(shape):
    zeros = (0,) * len(shape)
    return pl.BlockSpec(shape, lambda *_: zeros, pipeline_mode=pl.Buffered(1))


def _proj_call(x, carry_in, tabs, w, tile, valid_rows=None, transposed_v=False, row_offset=0):
    B, L, D = x.shape
    assert L % tile == 0
    nt = L // tile
    tri = jnp.asarray(np.tril(np.ones((tile, tile), np.float32)), BF16)

    row = lambda width: pl.BlockSpec((1, tile, width), lambda b, t: (b, t, 0))
    tab = pl.BlockSpec((tile, LANE), lambda b, t: (t, 0))
    in_specs = [row(D), pl.BlockSpec((1, 1, LANE), lambda b, t: (b, 0, 0)), tab, tab, tab, tab,
                _const_spec((1, D)), _const_spec((D, _NPROJ)), _const_spec((1, LANE)),
                _const_spec((1, _MLA_Q_LORA)), _const_spec((_MLA_Q_LORA, 2 * _WIDE)),
                _const_spec((1, _MLA_KV_LORA)), _const_spec((_MLA_KV_LORA + LANE, _WIDE)),
                _const_spec((_MLA_KV_LORA, MLA_HEADS * MLA_V)),
                _const_spec((tile, tile)), _const_spec((3 * LANE, _WIDE)), _const_spec((1, _WIDE))]
    out_shapes = [
        jax.ShapeDtypeStruct((B, L, _MLA_KV_LORA), F32),
        jax.ShapeDtypeStruct((B, L, MLA_ROPE), F32),
        jax.ShapeDtypeStruct((B, L, _FOX_WIDTH), F32),
        jax.ShapeDtypeStruct((B, L, _FOX_WIDTH), F32),
        jax.ShapeDtypeStruct((B, L, FOX_HEADS), F32),
        jax.ShapeDtypeStruct((B, L, _WIDE), BF16),
        jax.ShapeDtypeStruct((B, L, _WIDE), BF16),
        jax.ShapeDtypeStruct((B, L, MLA_HEADS * MLA_V), BF16),
        jax.ShapeDtypeStruct((B, L, _WIDE), BF16),
        jax.ShapeDtypeStruct((B, L, _WIDE), BF16),
        jax.ShapeDtypeStruct((B, L, _FOX_WIDTH), BF16),
        jax.ShapeDtypeStruct((B, 1, LANE), F32),
    ]
    out_specs = [row(s.shape[-1]) for s in out_shapes[:-1]]
    out_specs.append(pl.BlockSpec((1, 1, LANE), lambda b, t: (b, 0, 0)))
    if transposed_v:
        for idx in (7, 10):
            vw = out_shapes[idx].shape[-1]
            out_shapes[idx] = jax.ShapeDtypeStruct((B, vw, L), BF16)
            out_specs[idx] = pl.BlockSpec((1, vw, tile), lambda b, t: (b, 0, t))
    if row_offset:
        assert row_offset % SUBLANE == 0 and L % SUBLANE == 0
        for idx in range(5):
            width = out_shapes[idx].shape[-1]
            out_shapes[idx] = jax.ShapeDtypeStruct((B * (row_offset + L), width), F32)
            out_specs[idx] = pl.BlockSpec(
                (pl.Element(tile), pl.Element(width)),
                lambda b, t: (pl.multiple_of(b * (row_offset + L) + row_offset + t * tile, SUBLANE), 0))
    return pl.pallas_call(
        functools.partial(_proj_kernel, valid_rows=valid_rows, transposed_v=transposed_v),
        out_shape=out_shapes,
        grid=(B, nt),
        in_specs=in_specs,
        out_specs=out_specs,
        scratch_shapes=[pltpu.VMEM((1, LANE), F32)],
        compiler_params=pltpu.CompilerParams(
            dimension_semantics=("parallel", "arbitrary"), vmem_limit_bytes=V7X_VMEM_LIMIT),
        name="proj",
    )(x, carry_in, *tabs, w["g_mix"], w["w_in"], w["bias_f"], w["q_norm"], w["w_uq2"],
      w["kv_norm"], w["w_kexp"], w["w_uv"], tri, w["place_k"], w["ones_q"])


def _place_rows_kernel(*refs):
    n = len(refs) // 3
    for src, out in zip(refs[:n], refs[2 * n:]):
        out[0] = src[...]


def _place_rows_call(rows, dests):
    n = len(rows)
    B = dests[0].shape[0]
    in_specs = [pl.BlockSpec(r.shape, lambda b: (0, 0)) for r in rows]
    in_specs += [pl.BlockSpec(memory_space=pl.ANY) for _ in dests]
    out_specs = [pl.BlockSpec((1,) + r.shape, lambda b: (b, 0, 0)) for r in rows]
    return pl.pallas_call(
        _place_rows_kernel,
        out_shape=[jax.ShapeDtypeStruct(d.shape, d.dtype) for d in dests],
        grid=(B,),
        in_specs=in_specs,
        out_specs=out_specs,
        input_output_aliases={n + i: i for i in range(n)},
        compiler_params=pltpu.CompilerParams(dimension_semantics=("arbitrary",)),
        name="place_meta_rows",
    )(*rows, *dests)


def _softmax_update(h, s, v_pair, m_scr, l_scr, acc_scr):
    m_prev = m_scr[h]
    m_new = jnp.maximum(m_prev, jnp.max(s, axis=1, keepdims=True))
    alpha = jnp.exp2(m_prev - m_new)
    p = jnp.exp2(s - jnp.tile(m_new, (1, s.shape[1] // LANE)))
    l_scr[h] = alpha * l_scr[h] + jnp.sum(p, axis=1, keepdims=True)
    acc_scr[h] = alpha * acc_scr[h] + _dot(p.astype(BF16), v_pair)
    m_scr[h] = m_new


QK_LOOKAHEAD = 8
UNIT_Q = 256
UNIT_K = 256
ONES_ROWS = 16


def _with_ones_rows(v_t):
    row = lax.broadcasted_iota(jnp.int32, (ONES_ROWS, v_t.shape[1]), 0)
    return jnp.concatenate([v_t, jnp.where(row == 0, 1.0, 0.0).astype(v_t.dtype)], axis=0)


def _softmax_update_t(h, qs, s_t, v_t, m_scr, acc_scr):
    uk, uq = s_t.shape
    s3 = s_t.reshape(uk // SUBLANE, SUBLANE, uq)
    m_cur = jnp.max(s3, axis=0)
    for shift in (4, 2, 1):
        m_cur = jnp.maximum(m_cur, pltpu.roll(m_cur, shift, 0))
    m_prev = m_scr[h, :, qs]
    m_new = jnp.maximum(m_prev, m_cur)
    alpha = jnp.exp2(m_prev - m_new)
    p_t = jnp.exp2(s3 - m_new[None]).reshape(uk, uq)
    pv = _dot(v_t, p_t.astype(BF16))
    rows = pv.shape[0]
    acc = acc_scr[h, :, qs].reshape(rows // SUBLANE, SUBLANE, uq)
    acc_scr[h, :, qs] = (alpha[None] * acc).reshape(rows, uq) + pv
    m_scr[h, :, qs] = m_new


def _attn_kernel(qi_ref, kj_ref, q_ref, k_ref, vt_ref, kmeta_ref, vtmeta_ref, o_ref,
                 m_scr, acc_scr, *, chunk_mask, n_meta):
    p_id = pl.program_id(1)
    i = qi_ref[p_id]
    j = kj_ref[p_id]
    tq = q_ref.shape[1]
    tk = k_ref.shape[1]
    dv = acc_scr.shape[1] - ONES_ROWS

    def head(ref2d, h):
        return ref2d[:, h * HEAD_GROUP:(h + 1) * HEAD_GROUP]

    @pl.when(j == 0)
    def _():
        m_scr[...] = jnp.full(m_scr.shape, NEG_INF, F32)
        acc_scr[...] = jnp.zeros(acc_scr.shape, F32)
        row = lax.broadcasted_iota(jnp.int32, (kmeta_ref.shape[0], tq), 0)
        for h in range(HEADS):
            s_t = _nt_dot(head(kmeta_ref, h), head(q_ref.at[0], h))
            s_t = jnp.where(row < n_meta, s_t, NEG_INF)
            _softmax_update_t(h, slice(0, tq), s_t, _with_ones_rows(vtmeta_ref[h * dv:(h + 1) * dv, :]),
                              m_scr, acc_scr)

    uq = min(UNIT_Q, tq)
    uk = min(UNIT_K, tk)

    def block(masked):
        units = [(h, q0, k0) for h in range(HEADS) for k0 in range(0, tk, uk) for q0 in range(0, tq, uq)]
        if masked:
            units = [u for u in units if u[2] < u[1] + uq]
            kk = lax.broadcasted_iota(jnp.int32, (uk, uq), 0)
            qq = lax.broadcasted_iota(jnp.int32, (uk, uq), 1)

        def qk(u):
            h, q0, k0 = u
            return _nt_dot(k_ref[0, k0:k0 + uk, h * HEAD_GROUP:(h + 1) * HEAD_GROUP],
                           q_ref[0, q0:q0 + uq, h * HEAD_GROUP:(h + 1) * HEAD_GROUP])

        scores = {}
        for n in range(-QK_LOOKAHEAD, len(units)):
            if n + QK_LOOKAHEAD < len(units):
                scores[n + QK_LOOKAHEAD] = qk(units[n + QK_LOOKAHEAD])
            if n >= 0:
                h, q0, k0 = units[n]
                s_t = scores.pop(n)
                if masked and k0 + uk > q0 + 1:
                    if chunk_mask:
                        valid = (kk + k0) // CHUNK <= (qq + q0) // CHUNK
                    else:
                        valid = kk + k0 <= qq + q0
                    s_t = jnp.where(valid, s_t, NEG_INF)
                v_t = _with_ones_rows(vt_ref[0, h * dv:(h + 1) * dv, k0:k0 + uk])
                _softmax_update_t(h, slice(q0, q0 + uq), s_t, v_t, m_scr, acc_scr)

    @pl.when(j < i)
    def _():
        block(False)

    @pl.when(j == i)
    def _():
        block(True)
        def normalised(h):
            return acc_scr[h, :dv, :] / acc_scr[h, dv:dv + 1, :]

        for hp in range(HEADS // 2):
            o_pair_t = jnp.concatenate([normalised(2 * hp), normalised(2 * hp + 1)], axis=0)
            o_ref[0, :, hp * 2 * dv:(hp + 1) * 2 * dv] = o_pair_t.T.astype(o_ref.dtype)


def _attn_call(q, k, v_t, k_meta, v_t_meta, *, chunk_mask, n_meta, tile, name):
    B, F, _ = q.shape
    assert F % tile == 0 and tile % CHUNK == 0
    n = F // tile
    pairs = [(i, j) for i in range(n) for j in range(i + 1)]
    qi = jnp.asarray([p[0] for p in pairs], jnp.int32)
    kj = jnp.asarray([p[1] for p in pairs], jnp.int32)
    vw = v_t.shape[1]
    grid_spec = pltpu.PrefetchScalarGridSpec(
        num_scalar_prefetch=2,
        grid=(B, len(pairs)),
        in_specs=[
            pl.BlockSpec((1, tile, _WIDE), lambda b, p, qi, kj: (b, qi[p], 0)),
            pl.BlockSpec((1, tile, _WIDE), lambda b, p, qi, kj: (b, kj[p], 0)),
            pl.BlockSpec((1, vw, tile), lambda b, p, qi, kj: (b, 0, kj[p])),
            pl.BlockSpec((LANE, _WIDE), lambda b, p, qi, kj: (0, 0)),
            pl.BlockSpec((vw, LANE), lambda b, p, qi, kj: (0, 0)),
        ],
        out_specs=pl.BlockSpec((1, tile, vw), lambda b, p, qi, kj: (b, qi[p], 0)),
        scratch_shapes=[pltpu.VMEM((HEADS, SUBLANE, tile), F32),
                        pltpu.VMEM((HEADS, vw // HEADS + ONES_ROWS, tile), F32)],
    )
    return pl.pallas_call(
        functools.partial(_attn_kernel, chunk_mask=chunk_mask, n_meta=n_meta),
        out_shape=jax.ShapeDtypeStruct((B, F, vw), BF16),
        grid_spec=grid_spec,
        compiler_params=pltpu.CompilerParams(
            dimension_semantics=("parallel", "arbitrary"), vmem_limit_bytes=V7X_VMEM_LIMIT),
        name=name,
    )(qi, kj, q, k, v_t, k_meta, v_t_meta)


def _ffn_kernel(x_ref, oa_ref, ob_ref, woa_ref, wob_ref, g_ref, wg_ref, wu_ref, wd_ref, gf_ref, y_ref):
    x1 = x_ref[...] + _dot(oa_ref[...], woa_ref[...]) + _dot(ob_ref[...], wob_ref[...])
    h = _rms(x1, g_ref[...]).astype(BF16)
    gate = _dot(h, wg_ref[...])
    up = _dot(h, wu_ref[...])
    act = (gate * jax.nn.sigmoid(gate) * up).astype(BF16)
    x2 = x1 + _dot(act, wd_ref[...])
    y_ref[...] = _rms(x2, gf_ref[...])


def _ffn_call(x, oa, ob, w, tile):
    R, D = x.shape
    tile = min(tile, R)
    assert R % tile == 0
    dm = oa.shape[-1]
    dff = w["w_gate"].shape[-1]
    row = lambda width: pl.BlockSpec((tile, width), lambda r: (r, 0))
    return pl.pallas_call(
        _ffn_kernel,
        out_shape=jax.ShapeDtypeStruct((R, D), F32),
        grid=(R // tile,),
        in_specs=[row(D), row(dm), row(dm),
                  _const_spec((dm, D)), _const_spec((dm, D)), _const_spec((1, D)),
                  _const_spec((D, dff)), _const_spec((D, dff)), _const_spec((dff, D)),
                  _const_spec((1, D))],
        out_specs=row(D),
        compiler_params=pltpu.CompilerParams(
            dimension_semantics=("parallel",), vmem_limit_bytes=V7X_VMEM_LIMIT),
        name="ffn",
    )(x, oa, ob, w["w_out_a"], w["w_out_b"], w["g_ffn"], w["w_gate"], w["w_up"], w["w_down"],
      w["g_final"])


_CUM_CHUNK = 256


def _cumsum_kernel(x_ref, u_ref, o_ref):
    rows, n = x_ref.shape
    u = u_ref[...]
    carry = jnp.zeros((rows, 1), F32)
    for c in range(n // _CUM_CHUNK):
        hi, mid, lo = _split3(x_ref[:, c * _CUM_CHUNK:(c + 1) * _CUM_CHUNK])
        y = _dot(hi, u) + _dot(mid, u) + _dot(lo, u) + carry
        o_ref[:, c * _CUM_CHUNK:(c + 1) * _CUM_CHUNK] = y
        carry = y[:, _CUM_CHUNK - 1:_CUM_CHUNK]


def _cumsum_call(x):
    rows, n = x.shape
    assert n % _CUM_CHUNK == 0
    u = jnp.asarray(np.triu(np.ones((_CUM_CHUNK, _CUM_CHUNK), np.float32)), BF16)
    return pl.pallas_call(
        _cumsum_kernel,
        out_shape=jax.ShapeDtypeStruct((rows, n), F32),
        compiler_params=pltpu.CompilerParams(vmem_limit_bytes=V7X_VMEM_LIMIT),
        name="cache_cumsum",
    )(x, u)


def _sample_fox_kernel(fq_ref, k_ref, v_ref, ck_ref, fkn_ref, fvn_ref, o_ref,
                       qm_scr, m_scr, l_scr, acc_scr):
    j = pl.program_id(1)
    nj = pl.num_programs(1)
    s_new = fq_ref.shape[1]
    lane = lax.broadcasted_iota(jnp.int32, (s_new, LANE), 1)

    @pl.when(j == 0)
    def _():
        m_scr[...] = jnp.full(m_scr.shape, NEG_INF, F32)
        l_scr[...] = jnp.zeros(l_scr.shape, F32)
        acc_scr[...] = jnp.zeros(acc_scr.shape, F32)
        for h in range(HEADS):
            q_h = jnp.where(lane < FOX_HEAD_DIM, fq_ref[0, :, h * HEAD_GROUP:(h + 1) * HEAD_GROUP], 0)
            if h % 2 == 1:
                q_h = pltpu.roll(q_h.astype(F32), FOX_HEAD_DIM, 1).astype(BF16)
            qm_scr[h] = q_h

    ck2 = ck_ref[0] * LOG2E
    for hp in range(HEADS // 2):
        k_pair = k_ref[0, :, hp * LANE:(hp + 1) * LANE].astype(BF16)
        v_pair = v_ref[0, :, hp * LANE:(hp + 1) * LANE].astype(BF16)
        for h in (2 * hp, 2 * hp + 1):
            s = _nt_dot(qm_scr[h], k_pair) - ck2[h:h + 1, :]
            _softmax_update(h, s, v_pair, m_scr, l_scr, acc_scr)

    @pl.when(j == nj - 1)
    def _():
        r = lax.broadcasted_iota(jnp.int32, (s_new, s_new), 0)
        c = lax.broadcasted_iota(jnp.int32, (s_new, s_new), 1)
        for h in range(HEADS):
            s = _nt_dot(fq_ref[0, :, h * HEAD_GROUP:(h + 1) * HEAD_GROUP],
                        fkn_ref[0, :, h * HEAD_GROUP:(h + 1) * HEAD_GROUP])
            s = jnp.where(c <= r, s, NEG_INF)
            m_prev = m_scr[h]
            m_new = jnp.maximum(m_prev, jnp.max(s, axis=1, keepdims=True))
            alpha = jnp.exp2(m_prev - m_new)
            p = jnp.exp2(s - m_new[:, :s_new])
            l_scr[h] = alpha * l_scr[h] + jnp.sum(p, axis=1, keepdims=True)
            pair = (h // 2) * LANE
            acc_scr[h] = alpha * acc_scr[h] + _dot(p.astype(BF16), fvn_ref[0, :, pair:pair + LANE])
        for hp in range(HEADS // 2):
            even = acc_scr[2 * hp] / l_scr[2 * hp]
            odd = acc_scr[2 * hp + 1] / l_scr[2 * hp + 1]
            o_ref[0, :, hp * LANE:(hp + 1) * LANE] = jnp.where(lane < LANE // 2, even, odd).astype(o_ref.dtype)


def _sample_fox_call(fq, cache_k, cache_v, cum_cache, fka_new, fv_new, tile):
    B, S, _ = fq.shape
    past = cache_k.shape[1]
    assert past % tile == 0
    w = cache_k.shape[-1]
    return pl.pallas_call(
        _sample_fox_kernel,
        out_shape=jax.ShapeDtypeStruct((B, S, w), BF16),
        grid=(B, past // tile),
        in_specs=[pl.BlockSpec((1, S, _WIDE), lambda b, j: (b, 0, 0)),
                  pl.BlockSpec((1, tile, w), lambda b, j: (b, j, 0)),
                  pl.BlockSpec((1, tile, w), lambda b, j: (b, j, 0)),
                  pl.BlockSpec((1, HEADS, tile), lambda b, j: (b, 0, j)),
                  pl.BlockSpec((1, S, _WIDE), lambda b, j: (b, 0, 0)),
                  pl.BlockSpec((1, S, w), lambda b, j: (b, 0, 0))],
        out_specs=pl.BlockSpec((1, S, w), lambda b, j: (b, 0, 0)),
        scratch_shapes=[pltpu.VMEM((HEADS, S, LANE), BF16),
                        pltpu.VMEM((HEADS, S, LANE), F32),
                        pltpu.VMEM((HEADS, S, LANE), F32),
                        pltpu.VMEM((HEADS, S, LANE), F32)],
        compiler_params=pltpu.CompilerParams(
            dimension_semantics=("parallel", "arbitrary"), vmem_limit_bytes=V7X_VMEM_LIMIT),
        name="sample_fox",
    )(fq, cache_k, cache_v, cum_cache, fka_new, fv_new)


def _sample_mla_kernel(q_ref, lat_ref, kr_ref, latn_ref, krn_ref, wabs_ref, prope_ref, wv_ref, o_ref,
                       ql_scr, qr_scr, m_scr, l_scr, acc_scr):
    j = pl.program_id(1)
    nj = pl.num_programs(1)
    s_new = q_ref.shape[1]

    @pl.when(j == 0)
    def _():
        m_scr[...] = jnp.full(m_scr.shape, NEG_INF, F32)
        l_scr[...] = jnp.zeros(l_scr.shape, F32)
        acc_scr[...] = jnp.zeros(acc_scr.shape, F32)
        for h in range(HEADS):
            q_h = q_ref[0, :, h * HEAD_GROUP:(h + 1) * HEAD_GROUP]
            ql_scr[h * s_new:(h + 1) * s_new, :] = _dot(q_h, wabs_ref[h]).astype(BF16)
            qr_scr[h * s_new:(h + 1) * s_new, :] = _dot(q_h, prope_ref[...]).astype(BF16)

    def update(lat, kr):
        s = _nt_dot(ql_scr[...], lat) + _nt_dot(qr_scr[...], kr)
        m_prev = m_scr[...]
        m_new = jnp.maximum(m_prev, jnp.max(s, axis=1, keepdims=True))
        alpha = jnp.exp2(m_prev - m_new)
        p = jnp.exp2(s - m_new[:, :1])
        l_scr[...] = alpha * l_scr[...] + jnp.sum(p, axis=1, keepdims=True)
        acc_scr[...] = jnp.tile(alpha, (1, acc_scr.shape[1] // LANE)) * acc_scr[...] + _dot(p.astype(BF16), lat)
        m_scr[...] = m_new

    def pad_rope(kr):
        return jnp.concatenate([kr, jnp.zeros((kr.shape[0], LANE - MLA_ROPE), kr.dtype)], axis=1)

    update(lat_ref[0].astype(BF16), pad_rope(kr_ref[0]).astype(BF16))

    @pl.when(j == nj - 1)
    def _():
        update(latn_ref[0].astype(BF16), pad_rope(krn_ref[0]).astype(BF16))
        o_lat = (acc_scr[...] / jnp.tile(l_scr[...], (1, acc_scr.shape[1] // LANE))).astype(BF16)
        out = _dot(o_lat[0:s_new], wv_ref[0])
        for h in range(1, HEADS):
            out = out + _dot(o_lat[h * s_new:(h + 1) * s_new], wv_ref[h])
        o_ref[0] = out.astype(o_ref.dtype)


def _sample_mla_call(q, cache_lat, cache_kr, lat_new, kr_new, w, tile):
    B, S, _ = q.shape
    past = cache_lat.shape[1]
    assert past % tile == 0
    c = cache_lat.shape[-1]
    ow = MLA_HEADS * MLA_V
    return pl.pallas_call(
        _sample_mla_kernel,
        out_shape=jax.ShapeDtypeStruct((B, S, ow), BF16),
        grid=(B, past // tile),
        in_specs=[pl.BlockSpec((1, S, _WIDE), lambda b, j: (b, 0, 0)),
                  pl.BlockSpec((1, tile, c), lambda b, j: (b, j, 0)),
                  pl.BlockSpec((1, tile, MLA_ROPE), lambda b, j: (b, j, 0)),
                  pl.BlockSpec((1, S, c), lambda b, j: (b, 0, 0)),
                  pl.BlockSpec((1, S, MLA_ROPE), lambda b, j: (b, 0, 0)),
                  _const_spec((HEADS, HEAD_GROUP, c)), _const_spec((HEAD_GROUP, LANE)),
                  _const_spec((HEADS, c, ow))],
        out_specs=pl.BlockSpec((1, S, ow), lambda b, j: (b, 0, 0)),
        scratch_shapes=[pltpu.VMEM((HEADS * S, c), BF16),
                        pltpu.VMEM((HEADS * S, LANE), BF16),
                        pltpu.VMEM((HEADS * S, LANE), F32),
                        pltpu.VMEM((HEADS * S, LANE), F32),
                        pltpu.VMEM((HEADS * S, c), F32)],
        compiler_params=pltpu.CompilerParams(
            dimension_semantics=("parallel", "arbitrary"), vmem_limit_bytes=V7X_VMEM_LIMIT),
        name="sample_mla",
    )(q, cache_lat, cache_kr, lat_new, kr_new, w["w_abs"], w["p_rope"], w["w_v_wide"])


def _gather_cols(w, src):
    src = np.asarray(src)
    pieces, start = [], 0
    for i in range(1, len(src) + 1):
        run_ends = i == len(src) or (src[i] != src[i - 1] + 1 if src[i - 1] >= 0 else src[i] >= 0)
        if run_ends:
            if src[start] >= 0:
                pieces.append(w[:, int(src[start]):int(src[start]) + i - start])
            else:
                pieces.append(jnp.zeros((w.shape[0], i - start), w.dtype))
            start = i
    return jnp.concatenate(pieces, axis=1)


def _prep_weights(norm_mix, w_in, b_forget, mla_q_norm, w_mla_uq, mla_kv_norm, w_mla_ukv, w_out,
                  norm_ffn, w_ffn_gate, w_ffn_up, w_ffn_down, norm_final):
    d_model = w_in.shape[0]
    o_kr = _MLA_Q_LORA + _MLA_KV_LORA
    o_fq = o_kr + MLA_ROPE
    o_fk = o_fq + _FOX_WIDTH
    o_fv = o_fk + _FOX_WIDTH
    o_fl = o_fv + _FOX_WIDTH
    half = MLA_ROPE // 2

    src = -np.ones((_NPROJ,), np.int64)
    src[_O_CQ:_O_CKV] = np.arange(0, _MLA_Q_LORA)
    src[_O_CKV:_O_FQW] = np.arange(_MLA_Q_LORA, o_kr)
    for h in range(FOX_HEADS):
        d = np.arange(FOX_HEAD_DIM)
        src[_O_FQW + h * HEAD_GROUP + d] = o_fq + h * FOX_HEAD_DIM + d
        src[_O_FKW + h * HEAD_GROUP + d] = o_fk + h * FOX_HEAD_DIM + d
    src[_O_FK:_O_FV] = np.arange(o_fk, o_fv)
    src[_O_FV:_O_RA] = np.arange(o_fv, o_fl)
    src[_O_RA:_O_RA + MLA_ROPE] = np.arange(o_kr, o_fq)
    src[_O_RB:_O_RB + half] = np.arange(o_kr + half, o_fq)
    src[_O_RB + half:_O_RB + MLA_ROPE] = np.arange(o_kr, o_kr + half)
    src[_O_FL:_O_FL + FOX_HEADS] = np.arange(o_fl, o_fl + FOX_HEADS)
    w_in_r = _gather_cols(w_in, src).astype(BF16)

    src_a = -np.ones((_WIDE,), np.int64)
    src_b = -np.ones((_WIDE,), np.int64)
    for h in range(MLA_HEADS):
        base = h * MLA_QK_DIM
        src_a[h * HEAD_GROUP + np.arange(MLA_QK_DIM)] = base + np.arange(MLA_QK_DIM)
        src_b[h * HEAD_GROUP + MLA_NOPE + np.arange(half)] = base + MLA_NOPE + half + np.arange(half)
        src_b[h * HEAD_GROUP + MLA_NOPE + half + np.arange(half)] = base + MLA_NOPE + np.arange(half)
    w_uq2 = _gather_cols(w_mla_uq, np.concatenate([src_a, src_b])).astype(BF16)

    src_k = -np.ones((_WIDE,), np.int64)
    src_v = np.zeros((MLA_HEADS * MLA_V,), np.int64)
    for h in range(MLA_HEADS):
        src_k[h * HEAD_GROUP + np.arange(MLA_NOPE)] = h * (MLA_NOPE + MLA_V) + np.arange(MLA_NOPE)
        src_v[h * MLA_V + np.arange(MLA_V)] = h * (MLA_NOPE + MLA_V) + MLA_NOPE + np.arange(MLA_V)
    place_r = np.zeros((LANE, _WIDE), np.float32)
    for h in range(MLA_HEADS):
        place_r[np.arange(MLA_ROPE), h * HEAD_GROUP + MLA_NOPE + np.arange(MLA_ROPE)] = 1.0
    w_uk_wide = _gather_cols(w_mla_ukv, src_k)
    w_kexp = jnp.concatenate([w_uk_wide, jnp.asarray(place_r)], axis=0).astype(BF16)
    w_uv = _gather_cols(w_mla_ukv, src_v).astype(BF16)

    w_uk = w_mla_ukv.reshape(_MLA_KV_LORA, MLA_HEADS, MLA_NOPE + MLA_V)[:, :, :MLA_NOPE]
    w_abs = jnp.zeros((MLA_HEADS, HEAD_GROUP, _MLA_KV_LORA), F32)
    w_abs = w_abs.at[:, :MLA_NOPE, :].set(jnp.transpose(w_uk, (1, 2, 0))).astype(BF16)
    p_rope = np.zeros((HEAD_GROUP, LANE), np.float32)
    p_rope[MLA_NOPE + np.arange(MLA_ROPE), np.arange(MLA_ROPE)] = 1.0
    w_uv_h = w_mla_ukv.reshape(_MLA_KV_LORA, MLA_HEADS, MLA_NOPE + MLA_V)[:, :, MLA_NOPE:]
    eye = jnp.asarray(np.eye(MLA_HEADS, dtype=np.float32))
    w_v_wide = (jnp.transpose(w_uv_h, (1, 0, 2))[:, :, None, :] * eye[:, None, :, None])
    w_v_wide = w_v_wide.reshape(MLA_HEADS, _MLA_KV_LORA, MLA_HEADS * MLA_V).astype(BF16)

    bias_f = jnp.zeros((1, LANE), F32).at[0, :FOX_HEADS].set(b_forget.astype(F32))
    place_k = np.zeros((3 * LANE, _WIDE), np.float32)
    ones_q = np.zeros((1, _WIDE), np.float32)
    for h in range(FOX_HEADS):
        for part in range(3):
            place_k[part * LANE + h, h * HEAD_GROUP + FOX_HEAD_DIM + part] = -1.0
            ones_q[0, h * HEAD_GROUP + FOX_HEAD_DIM + part] = 1.0

    d_mla = MLA_HEADS * MLA_V
    return {
        "g_mix": norm_mix.reshape(1, d_model).astype(F32),
        "w_in": w_in_r,
        "bias_f": bias_f,
        "q_norm": mla_q_norm.reshape(1, -1).astype(F32),
        "w_uq2": w_uq2,
        "kv_norm": mla_kv_norm.reshape(1, -1).astype(F32),
        "w_kexp": w_kexp,
        "w_uv": w_uv,
        "place_k": jnp.asarray(place_k, BF16),
        "ones_q": jnp.asarray(ones_q),
        "w_abs": w_abs,
        "p_rope": jnp.asarray(p_rope, BF16),
        "w_v_wide": w_v_wide,
        "w_out_a": w_out[:d_mla].astype(BF16),
        "w_out_b": w_out[d_mla:].astype(BF16),
        "g_ffn": norm_ffn.reshape(1, d_model).astype(F32),
        "w_gate": w_ffn_gate.astype(BF16),
        "w_up": w_ffn_up.astype(BF16),
        "w_down": w_ffn_down.astype(BF16),
        "g_final": norm_final.reshape(1, d_model).astype(F32),
    }


def _rope_tables(pos):
    half = MLA_ROPE // 2
    inv_freq = ROPE_THETA ** (-jnp.arange(half, dtype=F32) / half)
    ang = pos.astype(F32)[:, None] * inv_freq[None, :]
    cos, sin = jnp.cos(ang), jnp.sin(ang)
    n = pos.shape[0]
    sc = MLA_SCALE * LOG2E
    zeros = lambda w: jnp.zeros((n, w), F32)
    tqc = jnp.concatenate([jnp.full((n, MLA_NOPE), sc, F32), sc * cos, sc * cos,
                           zeros(LANE - MLA_QK_DIM)], axis=1)
    tqs = jnp.concatenate([zeros(MLA_NOPE), -sc * sin, sc * sin, zeros(LANE - MLA_QK_DIM)], axis=1)
    tkc = jnp.concatenate([cos, cos, zeros(LANE - MLA_ROPE)], axis=1)
    tks = jnp.concatenate([-sin, sin, zeros(LANE - MLA_ROPE)], axis=1)
    return tqc, tqs, tkc, tks


def _pad_rows(a, rows):
    return jnp.concatenate([a, jnp.zeros((rows - a.shape[0],) + a.shape[1:], a.dtype)], axis=0)


PROJ_TILE = 256
ATTN_TILE = 512
FFN_TILE = 256
SAMPLE_TILE = 1024


def kernel(x_prompt, x_sample, cache_mla_latent, cache_mla_krope, cache_fox_k, cache_fox_v, cache_fox_logf, meta_tokens, norm_mix, w_in, b_forget, mla_q_norm, w_mla_uq, mla_kv_norm, w_mla_ukv, w_out, norm_ffn, w_ffn_gate, w_ffn_up, w_ffn_down, norm_final):
    depth = w_in.shape[0]
    assert depth == 1, "single-layer trunk: the meta rows' mixing output never reaches an output"
    B, seq, d_model = x_prompt.shape
    SB, s_new, _ = x_sample.shape
    n_meta = meta_tokens.shape[0]
    past = cache_mla_latent.shape[2]
    assert n_meta <= LANE

    w = _prep_weights(norm_mix[0], w_in[0], b_forget[0], mla_q_norm[0], w_mla_uq[0], mla_kv_norm[0],
                      w_mla_ukv[0], w_out[0], norm_ffn[0], w_ffn_gate[0], w_ffn_up[0], w_ffn_down[0],
                      norm_final)

    zero_carry = jnp.zeros((1, 1, LANE), F32)
    meta = _proj_call(_pad_rows(meta_tokens.astype(F32), LANE)[None], zero_carry,
                      _rope_tables(jnp.arange(LANE)), w, LANE, valid_rows=n_meta, transposed_v=True)
    (m_lat, m_kr, m_fk, m_fv, m_lf, _, m_km, m_vm, _, m_fka, m_fvb, m_carry) = meta
    frames = _proj_call(x_prompt, jnp.broadcast_to(m_carry, (B, 1, LANE)),
                        _rope_tables(n_meta + jnp.arange(seq)), w, PROJ_TILE, transposed_v=True,
                        row_offset=n_meta)
    f_qm, f_km, f_vm, f_fq, f_fka, f_fvb = frames[5:11]
    lat_p, kr_p, fk_p, fv_p, lf_p = _place_rows_call(
        [a[0, :n_meta] for a in (m_lat, m_kr, m_fk, m_fv, m_lf)],
        [a.reshape(B, n_meta + seq, a.shape[-1]) for a in frames[:5]])

    o_mla = _attn_call(f_qm, f_km, f_vm, m_km[0], m_vm[0],
                       chunk_mask=True, n_meta=n_meta, tile=ATTN_TILE, name="attn_mla")
    o_fox = _attn_call(f_fq, f_fka, f_fvb, m_fka[0], m_fvb[0],
                       chunk_mask=False, n_meta=n_meta, tile=ATTN_TILE, name="attn_fox")
    y_prompt = _ffn_call(x_prompt.reshape(B * seq, d_model), o_mla.reshape(B * seq, -1),
                         o_fox.reshape(B * seq, -1), w, FFN_TILE).reshape(B, seq, d_model)

    lat_p, kr_p, lf_p = lat_p[None], kr_p[None], lf_p[None]
    fk_p = fk_p.reshape(1, B, n_meta + seq, FOX_HEADS, FOX_HEAD_DIM)
    fv_p = fv_p.reshape(1, B, n_meta + seq, FOX_HEADS, FOX_HEAD_DIM)

    logf_t = jnp.transpose(cache_fox_logf[0].astype(F32), (0, 2, 1))
    cum_cache = _cumsum_call(logf_t.reshape(SB * FOX_HEADS, past)).reshape(SB, FOX_HEADS, past)
    carry_s = jnp.zeros((SB, 1, LANE), F32).at[:, 0, :FOX_HEADS].set(cum_cache[:, :, past - 1])
    samp = _proj_call(x_sample, carry_s, _rope_tables(past + jnp.arange(s_new)), w, s_new)
    (s_lat, s_kr, s_fk, s_fv, s_lf, s_qm, _, _, s_fq, s_fka, s_fvb, _) = samp

    so_mla = _sample_mla_call(s_qm, cache_mla_latent[0], cache_mla_krope[0], s_lat, s_kr, w, SAMPLE_TILE)
    so_fox = _sample_fox_call(s_fq, cache_fox_k[0].reshape(SB, past, _FOX_WIDTH),
                              cache_fox_v[0].reshape(SB, past, _FOX_WIDTH), cum_cache, s_fka, s_fvb,
                              SAMPLE_TILE)
    y_sample = _ffn_call(x_sample.reshape(SB * s_new, d_model), so_mla.reshape(SB * s_new, -1),
                         so_fox.reshape(SB * s_new, -1), w, FFN_TILE).reshape(SB, s_new, d_model)

    heads = lambda a: a.reshape(1, SB, s_new, FOX_HEADS, FOX_HEAD_DIM)
    return (y_prompt, y_sample, lat_p, kr_p, fk_p, fv_p, lf_p,
            s_lat[None], s_kr[None], heads(s_fk), heads(s_fv), s_lf[None])
```

```python
import functools

import numpy as np
import jax
import jax.numpy as jnp
from jax import lax
from jax.experimental import pallas as pl
from jax.experimental.pallas import tpu as pltpu

CHUNK = 64
EPS = 1e-6
NEG_INF = -1e30
MLA_HEADS = 8
MLA_NOPE = 64
MLA_ROPE = 32
MLA_V = 64
MLA_QK_DIM = MLA_NOPE + MLA_ROPE
MLA_SCALE = MLA_QK_DIM ** -0.5
ROPE_THETA = 10000.0
FOX_HEADS = 8
FOX_HEAD_DIM = 64
FOX_SCALE = FOX_HEAD_DIM ** -0.5
LOG2E = 1.4426950408889634

LANE = 128
SUBLANE = 8
BF16_SUBLANES = 16
HEAD_GROUP = LANE
HEADS = 8
V7X_VMEM_LIMIT = 56 * 1024 * 1024

BF16 = jnp.bfloat16
F32 = jnp.float32

_MLA_Q_LORA = 384
_MLA_KV_LORA = 256
_FOX_WIDTH = FOX_HEADS * FOX_HEAD_DIM
_WIDE = HEADS * HEAD_GROUP
_O_CQ = 0
_O_CKV = _O_CQ + _MLA_Q_LORA
_O_FQW = _O_CKV + _MLA_KV_LORA
_O_FKW = _O_FQW + _WIDE
_O_FK = _O_FKW + _WIDE
_O_FV = _O_FK + _FOX_WIDTH
_O_RA = _O_FV + _FOX_WIDTH
_O_RB = _O_RA + LANE
_O_FL = _O_RB + LANE
_NPROJ = _O_FL + LANE


def _nt_dot(a, b):
    return lax.dot_general(a, b, (((1,), (1,)), ((), ())), preferred_element_type=F32)


def _dot(a, b):
    return jnp.dot(a, b, preferred_element_type=F32)


def _rms(x, g):
    return x * lax.rsqrt(jnp.mean(x * x, axis=-1, keepdims=True) + EPS) * g


def _split3(x):
    hi = x.astype(BF16)
    r = x - hi.astype(F32)
    mid = r.astype(BF16)
    lo = (r - mid.astype(F32)).astype(BF16)
    return hi, mid, lo


def _proj_kernel(x_ref, carry_in_ref, tqc_ref, tqs_ref, tkc_ref, tks_ref,
                 g_ref, w_in_ref, bias_ref, qn_ref, w_uq_ref, kvn_ref, w_kexp_ref, w_uv_ref,
                 tri_ref, place_ref, ones_q_ref,
                 lat_ref, kr_ref, fk_ref, fv_ref, lf_ref,
                 qm_ref, km_ref, vm_ref, fq_ref, fka_ref, fvb_ref, carry_out_ref,
                 carry_scr, *, valid_rows, transposed_v):
    t = pl.program_id(1)

    @pl.when(t == 0)
    def _():
        carry_scr[...] = carry_in_ref[0]

    x = x_ref[0]
    h = _rms(x, g_ref[...]).astype(BF16)
    proj = _dot(h, w_in_ref[...])

    cqn = _rms(proj[:, _O_CQ:_O_CKV], qn_ref[...]).astype(BF16)
    q2 = _dot(cqn, w_uq_ref[...])
    tqc = jnp.tile(tqc_ref[...], (1, HEADS))
    tqs = jnp.tile(tqs_ref[...], (1, HEADS))
    qm_ref[0] = (q2[:, :_WIDE] * tqc + q2[:, _WIDE:] * tqs).astype(BF16)

    latent = _rms(proj[:, _O_CKV:_O_FQW], kvn_ref[...])
    def put(ref, val):
        ref[...] = val.reshape(ref.shape)

    put(lat_ref, latent)
    kr = proj[:, _O_RA:_O_RB] * tkc_ref[...] + proj[:, _O_RB:_O_FL] * tks_ref[...]
    put(kr_ref, kr[:, :MLA_ROPE])
    lat_bf = latent.astype(BF16)
    kcat = jnp.concatenate([lat_bf, kr.astype(BF16)], axis=1)
    km_ref[0] = _dot(kcat, w_kexp_ref[...]).astype(BF16)
    v_mla = _dot(lat_bf, w_uv_ref[...])
    vm_ref[0] = (v_mla.T if transposed_v else v_mla).astype(BF16)

    put(fk_ref, proj[:, _O_FK:_O_FV])
    fv = proj[:, _O_FV:_O_RA]
    put(fv_ref, fv)
    fvb_ref[0] = (fv.T if transposed_v else fv).astype(BF16)

    z = proj[:, _O_FL:_NPROJ] + bias_ref[...]
    lane = lax.broadcasted_iota(jnp.int32, z.shape, 1)
    keep = lane < FOX_HEADS
    if valid_rows is not None:
        keep = keep & (lax.broadcasted_iota(jnp.int32, z.shape, 0) < valid_rows)
    lf = jnp.where(keep, jnp.minimum(z, 0.0) - jnp.log1p(jnp.exp(-jnp.abs(z))), 0.0)
    put(lf_ref, lf[:, :FOX_HEADS])
    hi, mid, lo = _split3(lf)
    tri = tri_ref[...]
    cum = _dot(tri, hi) + _dot(tri, mid) + _dot(tri, lo) + carry_scr[...]
    n_rows = cum.shape[0]
    carry_scr[...] = cum[n_rows - 1:n_rows, :]
    carry_out_ref[0] = cum[n_rows - 1:n_rows, :]
    c_hi, c_mid, c_lo = _split3(cum * LOG2E)
    kb = _dot(jnp.concatenate([c_hi, c_mid, c_lo], axis=1), place_ref[...])
    fq_ref[0] = (proj[:, _O_FQW:_O_FKW] * (FOX_SCALE * LOG2E) + ones_q_ref[...]).astype(BF16)
    fka_ref[0] = (proj[:, _O_FKW:_O_FK] + kb).astype(BF16)


def _const_spec(shape):
    zeros = (0,) * len(shape)
    return pl.BlockSpec(shape, lambda *_: zeros, pipeline_mode=pl.Buffered(1))


def _proj_call(x, carry_in, tabs, w, tile, valid_rows=None, transposed_v=False, row_offset=0):
    B, L, D = x.shape
    assert L % tile == 0
    nt = L // tile
    tri = jnp.asarray(np.tril(np.ones((tile, tile), np.float32)), BF16)

    row = lambda width: pl.BlockSpec((1, tile, width), lambda b, t: (b, t, 0))
    tab = pl.BlockSpec((tile, LANE), lambda b, t: (t, 0))
    in_specs = [row(D), pl.BlockSpec((1, 1, LANE), lambda b, t: (b, 0, 0)), tab, tab, tab, tab,
                _const_spec((1, D)), _const_spec((D, _NPROJ)), _const_spec((1, LANE)),
                _const_spec((1, _MLA_Q_LORA)), _const_spec((_MLA_Q_LORA, 2 * _WIDE)),
                _const_spec((1, _MLA_KV_LORA)), _const_spec((_MLA_KV_LORA + LANE, _WIDE)),
                _const_spec((_MLA_KV_LORA, MLA_HEADS * MLA_V)),
                _const_spec((tile, tile)), _const_spec((3 * LANE, _WIDE)), _const_spec((1, _WIDE))]
    out_shapes = [
        jax.ShapeDtypeStruct((B, L, _MLA_KV_LORA), F32),
        jax.ShapeDtypeStruct((B, L, MLA_ROPE), F32),
        jax.ShapeDtypeStruct((B, L, _FOX_WIDTH), F32),
        jax.ShapeDtypeStruct((B, L, _FOX_WIDTH), F32),
        jax.ShapeDtypeStruct((B, L, FOX_HEADS), F32),
        jax.ShapeDtypeStruct((B, L, _WIDE), BF16),
        jax.ShapeDtypeStruct((B, L, _WIDE), BF16),
        jax.ShapeDtypeStruct((B, L, MLA_HEADS * MLA_V), BF16),
        jax.ShapeDtypeStruct((B, L, _WIDE), BF16),
        jax.ShapeDtypeStruct((B, L, _WIDE), BF16),
        jax.ShapeDtypeStruct((B, L, _FOX_WIDTH), BF16),
        jax.ShapeDtypeStruct((B, 1, LANE), F32),
    ]
    out_specs = [row(s.shape[-1]) for s in out_shapes[:-1]]
    out_specs.append(pl.BlockSpec((1, 1, LANE), lambda b, t: (b, 0, 0)))
    if transposed_v:
        for idx in (7, 10):
            vw = out_shapes[idx].shape[-1]
            out_shapes[idx] = jax.ShapeDtypeStruct((B, vw, L), BF16)
            out_specs[idx] = pl.BlockSpec((1, vw, tile), lambda b, t: (b, 0, t))
    if row_offset:
        assert row_offset % SUBLANE == 0 and L % SUBLANE == 0
        for idx in range(5):
            width = out_shapes[idx].shape[-1]
            out_shapes[idx] = jax.ShapeDtypeStruct((B * (row_offset + L), width), F32)
            out_specs[idx] = pl.BlockSpec(
                (pl.Element(tile), pl.Element(width)),
                lambda b, t: (pl.multiple_of(b * (row_offset + L) + row_offset + t * tile, SUBLANE), 0))
    return pl.pallas_call(
        functools.partial(_proj_kernel, valid_rows=valid_rows, transposed_v=transposed_v),
        out_shape=out_shapes,
        grid=(B, nt),
        in_specs=in_specs,
        out_specs=out_specs,
        scratch_shapes=[pltpu.VMEM((1, LANE), F32)],
        compiler_params=pltpu.CompilerParams(
            dimension_semantics=("parallel", "arbitrary"), vmem_limit_bytes=V7X_VMEM_LIMIT),
        name="proj",
    )(x, carry_in, *tabs, w["g_mix"], w["w_in"], w["bias_f"], w["q_norm"], w["w_uq2"],
      w["kv_norm"], w["w_kexp"], w["w_uv"], tri, w["place_k"], w["ones_q"])


def _place_rows_kernel(*refs):
    n = len(refs) // 3
    for src, out in zip(refs[:n], refs[2 * n:]):
        out[0] = src[...]


def _place_rows_call(rows, dests):
    n = len(rows)
    B = dests[0].shape[0]
    in_specs = [pl.BlockSpec(r.shape, lambda b, nd=r.ndim: (0,) * nd) for r in rows]
    in_specs += [pl.BlockSpec(memory_space=pl.ANY) for _ in dests]
    out_specs = [pl.BlockSpec((1,) + r.shape, lambda b, nd=r.ndim: (b,) + (0,) * nd) for r in rows]
    return pl.pallas_call(
        _place_rows_kernel,
        out_shape=[jax.ShapeDtypeStruct(d.shape, d.dtype) for d in dests],
        grid=(B,),
        in_specs=in_specs,
        out_specs=out_specs,
        input_output_aliases={n + i: i for i in range(n)},
        compiler_params=pltpu.CompilerParams(dimension_semantics=("arbitrary",)),
        name="place_meta_rows",
    )(*rows, *dests)


def _softmax_update(h, s, v_pair, m_scr, l_scr, acc_scr):
    m_prev = m_scr[h]
    m_new = jnp.maximum(m_prev, jnp.max(s, axis=1, keepdims=True))
    alpha = jnp.exp2(m_prev - m_new)
    p = jnp.exp2(s - jnp.tile(m_new, (1, s.shape[1] // LANE)))
    l_scr[h] = alpha * l_scr[h] + jnp.sum(p, axis=1, keepdims=True)
    acc_scr[h] = alpha * acc_scr[h] + _dot(p.astype(BF16), v_pair)
    m_scr[h] = m_new


QK_LOOKAHEAD = 8
UNIT_Q = 256
UNIT_K = 256
ONES_ROWS = 16


def _with_ones_rows(v_t):
    row = lax.broadcasted_iota(jnp.int32, (ONES_ROWS, v_t.shape[1]), 0)
    return jnp.concatenate([v_t, jnp.where(row == 0, 1.0, 0.0).astype(v_t.dtype)], axis=0)


def _softmax_update_t(h, qs, s_t, v_t, m_scr, acc_scr):
    uk, uq = s_t.shape
    s3 = s_t.reshape(uk // SUBLANE, SUBLANE, uq)
    m_cur = jnp.max(s3, axis=0)
    for shift in (4, 2, 1):
        m_cur = jnp.maximum(m_cur, pltpu.roll(m_cur, shift, 0))
    m_prev = m_scr[h, :, qs]
    m_new = jnp.maximum(m_prev, m_cur)
    alpha = jnp.exp2(m_prev - m_new)
    p_t = jnp.exp2(s3 - m_new[None]).reshape(uk, uq)
    pv = _dot(v_t, p_t.astype(BF16))
    rows = pv.shape[0]
    acc = acc_scr[h, :, qs].reshape(rows // SUBLANE, SUBLANE, uq)
    acc_scr[h, :, qs] = (alpha[None] * acc).reshape(rows, uq) + pv
    m_scr[h, :, qs] = m_new


def _attn_kernel(qi_ref, kj_ref, q_ref, k_ref, vt_ref, kmeta_ref, vtmeta_ref, o_ref,
                 m_scr, acc_scr, *, chunk_mask, n_meta):
    p_id = pl.program_id(1)
    i = qi_ref[p_id]
    j = kj_ref[p_id]
    tq = q_ref.shape[1]
    tk = k_ref.shape[1]
    dv = acc_scr.shape[1] - ONES_ROWS

    def head(ref2d, h):
        return ref2d[:, h * HEAD_GROUP:(h + 1) * HEAD_GROUP]

    @pl.when(j == 0)
    def _():
        n_pad = kmeta_ref.shape[0]
        n_rows = -(-n_meta // BF16_SUBLANES) * BF16_SUBLANES
        row = lax.broadcasted_iota(jnp.int32, (n_rows, tq), 0)
        scores = [_nt_dot(kmeta_ref[:n_rows, h * HEAD_GROUP:(h + 1) * HEAD_GROUP], head(q_ref.at[0], h))
                  for h in range(HEADS)]
        for h in range(HEADS):
            s3 = jnp.where(row < n_meta, scores[h], NEG_INF).reshape(n_rows // SUBLANE, SUBLANE, tq)
            m_new = jnp.max(s3, axis=0)
            for shift in (4, 2, 1):
                m_new = jnp.maximum(m_new, pltpu.roll(m_new, shift, 0))
            p_t = jnp.exp2(s3 - m_new[None]).reshape(n_rows, tq).astype(BF16)
            p_t = jnp.concatenate([p_t, jnp.zeros((n_pad - n_rows, tq), BF16)], axis=0)
            acc_scr[h] = _dot(_with_ones_rows(vtmeta_ref[h * dv:(h + 1) * dv, :]), p_t)
            m_scr[h] = m_new

    uq = min(UNIT_Q, tq)
    uk = min(UNIT_K, tk)

    def block(key_shift):
        masked = key_shift is not None
        units = [(h, q0, k0) for h in range(HEADS) for k0 in range(0, tk, uk) for q0 in range(0, tq, uq)]
        if masked:
            units = [u for u in units if key_shift + u[2] < u[1] + uq]
            kk = lax.broadcasted_iota(jnp.int32, (uk, uq), 0) + key_shift
            qq = lax.broadcasted_iota(jnp.int32, (uk, uq), 1)

        def qk(u):
            h, q0, k0 = u
            return _nt_dot(k_ref[0, k0:k0 + uk, h * HEAD_GROUP:(h + 1) * HEAD_GROUP],
                           q_ref[0, q0:q0 + uq, h * HEAD_GROUP:(h + 1) * HEAD_GROUP])

        scores = {}
        for n in range(-QK_LOOKAHEAD, len(units)):
            if n + QK_LOOKAHEAD < len(units):
                scores[n + QK_LOOKAHEAD] = qk(units[n + QK_LOOKAHEAD])
            if n >= 0:
                h, q0, k0 = units[n]
                s_t = scores.pop(n)
                if masked and key_shift + k0 + uk > q0 + 1:
                    if chunk_mask:
                        valid = (kk + k0) // CHUNK <= (qq + q0) // CHUNK
                    else:
                        valid = kk + k0 <= qq + q0
                    s_t = jnp.where(valid, s_t, NEG_INF)
                v_t = _with_ones_rows(vt_ref[0, h * dv:(h + 1) * dv, k0:k0 + uk])
                _softmax_update_t(h, slice(q0, q0 + uq), s_t, v_t, m_scr, acc_scr)

    ratio = tq // tk
    d = j - ratio * i

    @pl.when(d < 0)
    def _():
        block(None)

    for dd in range(ratio):
        @pl.when(d == dd)
        def _(dd=dd):
            block(dd * tk)

    @pl.when(d == ratio - 1)
    def _():
        def normalised(h):
            return acc_scr[h, :dv, :] / acc_scr[h, dv:dv + 1, :]

        for hp in range(HEADS // 2):
            o_pair_t = jnp.concatenate([normalised(2 * hp), normalised(2 * hp + 1)], axis=0)
            o_ref[0, :, hp * 2 * dv:(hp + 1) * 2 * dv] = o_pair_t.T.astype(o_ref.dtype)


def _attn_call(q, k, v_t, k_meta, v_t_meta, *, chunk_mask, n_meta, tq, tk, name):
    B, F, _ = q.shape
    tq, tk = min(tq, F), min(tk, F)
    assert F % tq == 0 and tq % tk == 0 and tk % CHUNK == 0
    ratio = tq // tk
    pairs = [(i, j) for i in range(F // tq) for j in range(ratio * (i + 1))]
    qi = jnp.asarray([p[0] for p in pairs], jnp.int32)
    kj = jnp.asarray([p[1] for p in pairs], jnp.int32)
    vw = v_t.shape[1]
    grid_spec = pltpu.PrefetchScalarGridSpec(
        num_scalar_prefetch=2,
        grid=(B, len(pairs)),
        in_specs=[
            pl.BlockSpec((1, tq, _WIDE), lambda b, p, qi, kj: (b, qi[p], 0)),
            pl.BlockSpec((1, tk, _WIDE), lambda b, p, qi, kj: (b, kj[p], 0)),
            pl.BlockSpec((1, vw, tk), lambda b, p, qi, kj: (b, 0, kj[p])),
            pl.BlockSpec((LANE, _WIDE), lambda b, p, qi, kj: (0, 0)),
            pl.BlockSpec((vw, LANE), lambda b, p, qi, kj: (0, 0)),
        ],
        out_specs=pl.BlockSpec((1, tq, vw), lambda b, p, qi, kj: (b, qi[p], 0)),
        scratch_shapes=[pltpu.VMEM((HEADS, SUBLANE, tq), F32),
                        pltpu.VMEM((HEADS, vw // HEADS + ONES_ROWS, tq), F32)],
    )
    return pl.pallas_call(
        functools.partial(_attn_kernel, chunk_mask=chunk_mask, n_meta=n_meta),
        out_shape=jax.ShapeDtypeStruct((B, F, vw), BF16),
        grid_spec=grid_spec,
        compiler_params=pltpu.CompilerParams(
            dimension_semantics=("parallel", "arbitrary"), vmem_limit_bytes=V7X_VMEM_LIMIT),
        name=name,
    )(qi, kj, q, k, v_t, k_meta, v_t_meta)


def _ffn_kernel(x_ref, oa_ref, ob_ref, woa_ref, wob_ref, g_ref, wg_ref, wu_ref, wd_ref, gf_ref, y_ref):
    x1 = x_ref[...] + _dot(oa_ref[...], woa_ref[...]) + _dot(ob_ref[...], wob_ref[...])
    h = _rms(x1, g_ref[...]).astype(BF16)
    gate = _dot(h, wg_ref[...])
    up = _dot(h, wu_ref[...])
    act = (gate * jax.nn.sigmoid(gate) * up).astype(BF16)
    x2 = x1 + _dot(act, wd_ref[...])
    y_ref[...] = _rms(x2, gf_ref[...])


def _ffn_call(x, oa, ob, w, tile):
    R, D = x.shape
    tile = min(tile, R)
    assert R % tile == 0
    dm = oa.shape[-1]
    dff = w["w_gate"].shape[-1]
    row = lambda width: pl.BlockSpec((tile, width), lambda r: (r, 0))
    return pl.pallas_call(
        _ffn_kernel,
        out_shape=jax.ShapeDtypeStruct((R, D), F32),
        grid=(R // tile,),
        in_specs=[row(D), row(dm), row(dm),
                  _const_spec((dm, D)), _const_spec((dm, D)), _const_spec((1, D)),
                  _const_spec((D, dff)), _const_spec((D, dff)), _const_spec((dff, D)),
                  _const_spec((1, D))],
        out_specs=row(D),
        compiler_params=pltpu.CompilerParams(
            dimension_semantics=("parallel",), vmem_limit_bytes=V7X_VMEM_LIMIT),
        name="ffn",
    )(x, oa, ob, w["w_out_a"], w["w_out_b"], w["g_ffn"], w["w_gate"], w["w_up"], w["w_down"],
      w["g_final"])


_CUM_CHUNK = 256


def _cumsum_kernel(x_ref, u_ref, o_ref):
    rows, n = x_ref.shape
    u = u_ref[...]
    carry = jnp.zeros((rows, 1), F32)
    for c in range(n // _CUM_CHUNK):
        hi, mid, lo = _split3(x_ref[:, c * _CUM_CHUNK:(c + 1) * _CUM_CHUNK])
        y = _dot(hi, u) + _dot(mid, u) + _dot(lo, u) + carry
        o_ref[:, c * _CUM_CHUNK:(c + 1) * _CUM_CHUNK] = y
        carry = y[:, _CUM_CHUNK - 1:_CUM_CHUNK]


def _cumsum_call(x):
    rows, n = x.shape
    assert n % _CUM_CHUNK == 0
    u = jnp.asarray(np.triu(np.ones((_CUM_CHUNK, _CUM_CHUNK), np.float32)), BF16)
    return pl.pallas_call(
        _cumsum_kernel,
        out_shape=jax.ShapeDtypeStruct((rows, n), F32),
        compiler_params=pltpu.CompilerParams(vmem_limit_bytes=V7X_VMEM_LIMIT),
        name="cache_cumsum",
    )(x, u)


def _sample_fox_kernel(fq_ref, k_ref, v_ref, ck_ref, fkn_ref, fvn_ref, o_ref,
                       qm_scr, m_scr, l_scr, acc_scr):
    j = pl.program_id(1)
    nj = pl.num_programs(1)
    s_new = fq_ref.shape[1]
    lane = lax.broadcasted_iota(jnp.int32, (s_new, LANE), 1)

    @pl.when(j == 0)
    def _():
        m_scr[...] = jnp.full(m_scr.shape, NEG_INF, F32)
        l_scr[...] = jnp.zeros(l_scr.shape, F32)
        acc_scr[...] = jnp.zeros(acc_scr.shape, F32)
        for h in range(HEADS):
            q_h = jnp.where(lane < FOX_HEAD_DIM, fq_ref[0, :, h * HEAD_GROUP:(h + 1) * HEAD_GROUP], 0)
            if h % 2 == 1:
                q_h = pltpu.roll(q_h.astype(F32), FOX_HEAD_DIM, 1).astype(BF16)
            qm_scr[h] = q_h

    ck2 = ck_ref[0] * LOG2E
    for hp in range(HEADS // 2):
        k_pair = k_ref[0, :, hp * LANE:(hp + 1) * LANE].astype(BF16)
        v_pair = v_ref[0, :, hp * LANE:(hp + 1) * LANE].astype(BF16)
        for h in (2 * hp, 2 * hp + 1):
            s = _nt_dot(qm_scr[h], k_pair) - ck2[h:h + 1, :]
            _softmax_update(h, s, v_pair, m_scr, l_scr, acc_scr)

    @pl.when(j == nj - 1)
    def _():
        r = lax.broadcasted_iota(jnp.int32, (s_new, s_new), 0)
        c = lax.broadcasted_iota(jnp.int32, (s_new, s_new), 1)
        for h in range(HEADS):
            s = _nt_dot(fq_ref[0, :, h * HEAD_GROUP:(h + 1) * HEAD_GROUP],
                        fkn_ref[0, :, h * HEAD_GROUP:(h + 1) * HEAD_GROUP])
            s = jnp.where(c <= r, s, NEG_INF)
            m_prev = m_scr[h]
            m_new = jnp.maximum(m_prev, jnp.max(s, axis=1, keepdims=True))
            alpha = jnp.exp2(m_prev - m_new)
            p = jnp.exp2(s - m_new[:, :s_new])
            l_scr[h] = alpha * l_scr[h] + jnp.sum(p, axis=1, keepdims=True)
            pair = (h // 2) * LANE
            acc_scr[h] = alpha * acc_scr[h] + _dot(p.astype(BF16), fvn_ref[0, :, pair:pair + LANE])
        for hp in range(HEADS // 2):
            even = acc_scr[2 * hp] / l_scr[2 * hp]
            odd = acc_scr[2 * hp + 1] / l_scr[2 * hp + 1]
            o_ref[0, :, hp * LANE:(hp + 1) * LANE] = jnp.where(lane < LANE // 2, even, odd).astype(o_ref.dtype)


def _sample_fox_call(fq, cache_k, cache_v, cum_cache, fka_new, fv_new, tile):
    B, S, _ = fq.shape
    past = cache_k.shape[1]
    assert past % tile == 0
    w = cache_k.shape[-1]
    return pl.pallas_call(
        _sample_fox_kernel,
        out_shape=jax.ShapeDtypeStruct((B, S, w), BF16),
        grid=(B, past // tile),
        in_specs=[pl.BlockSpec((1, S, _WIDE), lambda b, j: (b, 0, 0)),
                  pl.BlockSpec((1, tile, w), lambda b, j: (b, j, 0)),
                  pl.BlockSpec((1, tile, w), lambda b, j: (b, j, 0)),
                  pl.BlockSpec((1, HEADS, tile), lambda b, j: (b, 0, j)),
                  pl.BlockSpec((1, S, _WIDE), lambda b, j: (b, 0, 0)),
                  pl.BlockSpec((1, S, w), lambda b, j: (b, 0, 0))],
        out_specs=pl.BlockSpec((1, S, w), lambda b, j: (b, 0, 0)),
        scratch_shapes=[pltpu.VMEM((HEADS, S, LANE), BF16),
                        pltpu.VMEM((HEADS, S, LANE), F32),
                        pltpu.VMEM((HEADS, S, LANE), F32),
                        pltpu.VMEM((HEADS, S, LANE), F32)],
        compiler_params=pltpu.CompilerParams(
            dimension_semantics=("parallel", "arbitrary"), vmem_limit_bytes=V7X_VMEM_LIMIT),
        name="sample_fox",
    )(fq, cache_k, cache_v, cum_cache, fka_new, fv_new)


def _sample_mla_kernel(q_ref, lat_ref, kr_ref, latn_ref, krn_ref, wabs_ref, prope_ref, wv_ref, o_ref,
                       ql_scr, qr_scr, m_scr, l_scr, acc_scr):
    j = pl.program_id(1)
    nj = pl.num_programs(1)
    s_new = q_ref.shape[1]

    @pl.when(j == 0)
    def _():
        m_scr[...] = jnp.full(m_scr.shape, NEG_INF, F32)
        l_scr[...] = jnp.zeros(l_scr.shape, F32)
        acc_scr[...] = jnp.zeros(acc_scr.shape, F32)
        for h in range(HEADS):
            q_h = q_ref[0, :, h * HEAD_GROUP:(h + 1) * HEAD_GROUP]
            ql_scr[h * s_new:(h + 1) * s_new, :] = _dot(q_h, wabs_ref[h]).astype(BF16)
            qr_scr[h * s_new:(h + 1) * s_new, :] = _dot(q_h, prope_ref[...]).astype(BF16)

    def update(lat, kr):
        s = _nt_dot(ql_scr[...], lat) + _nt_dot(qr_scr[...], kr)
        m_prev = m_scr[...]
        m_new = jnp.maximum(m_prev, jnp.max(s, axis=1, keepdims=True))
        alpha = jnp.exp2(m_prev - m_new)
        p = jnp.exp2(s - m_new[:, :1])
        l_scr[...] = alpha * l_scr[...] + jnp.sum(p, axis=1, keepdims=True)
        acc_scr[...] = jnp.tile(alpha, (1, acc_scr.shape[1] // LANE)) * acc_scr[...] + _dot(p.astype(BF16), lat)
        m_scr[...] = m_new

    def pad_rope(kr):
        return jnp.concatenate([kr, jnp.zeros((kr.shape[0], LANE - MLA_ROPE), kr.dtype)], axis=1)

    update(lat_ref[0].astype(BF16), pad_rope(kr_ref[0]).astype(BF16))

    @pl.when(j == nj - 1)
    def _():
        update(latn_ref[0].astype(BF16), pad_rope(krn_ref[0]).astype(BF16))
        o_lat = (acc_scr[...] / jnp.tile(l_scr[...], (1, acc_scr.shape[1] // LANE))).astype(BF16)
        out = _dot(o_lat[0:s_new], wv_ref[0])
        for h in range(1, HEADS):
            out = out + _dot(o_lat[h * s_new:(h + 1) * s_new], wv_ref[h])
        o_ref[0] = out.astype(o_ref.dtype)


def _sample_mla_call(q, cache_lat, cache_kr, lat_new, kr_new, w, tile):
    B, S, _ = q.shape
    past = cache_lat.shape[1]
    assert past % tile == 0
    c = cache_lat.shape[-1]
    ow = MLA_HEADS * MLA_V
    return pl.pallas_call(
        _sample_mla_kernel,
        out_shape=jax.ShapeDtypeStruct((B, S, ow), BF16),
        grid=(B, past // tile),
        in_specs=[pl.BlockSpec((1, S, _WIDE), lambda b, j: (b, 0, 0)),
                  pl.BlockSpec((1, tile, c), lambda b, j: (b, j, 0)),
                  pl.BlockSpec((1, tile, MLA_ROPE), lambda b, j: (b, j, 0)),
                  pl.BlockSpec((1, S, c), lambda b, j: (b, 0, 0)),
                  pl.BlockSpec((1, S, MLA_ROPE), lambda b, j: (b, 0, 0)),
                  _const_spec((HEADS, HEAD_GROUP, c)), _const_spec((HEAD_GROUP, LANE)),
                  _const_spec((HEADS, c, ow))],
        out_specs=pl.BlockSpec((1, S, ow), lambda b, j: (b, 0, 0)),
        scratch_shapes=[pltpu.VMEM((HEADS * S, c), BF16),
                        pltpu.VMEM((HEADS * S, LANE), BF16),
                        pltpu.VMEM((HEADS * S, LANE), F32),
                        pltpu.VMEM((HEADS * S, LANE), F32),
                        pltpu.VMEM((HEADS * S, c), F32)],
        compiler_params=pltpu.CompilerParams(
            dimension_semantics=("parallel", "arbitrary"), vmem_limit_bytes=V7X_VMEM_LIMIT),
        name="sample_mla",
    )(q, cache_lat, cache_kr, lat_new, kr_new, w["w_abs"], w["p_rope"], w["w_v_wide"])


def _gather_cols(w, src):
    src = np.asarray(src)
    pieces, start = [], 0
    for i in range(1, len(src) + 1):
        run_ends = i == len(src) or (src[i] != src[i - 1] + 1 if src[i - 1] >= 0 else src[i] >= 0)
        if run_ends:
            if src[start] >= 0:
                pieces.append(w[:, int(src[start]):int(src[start]) + i - start])
            else:
                pieces.append(jnp.zeros((w.shape[0], i - start), w.dtype))
            start = i
    return jnp.concatenate(pieces, axis=1)


def _prep_weights(norm_mix, w_in, b_forget, mla_q_norm, w_mla_uq, mla_kv_norm, w_mla_ukv, w_out,
                  norm_ffn, w_ffn_gate, w_ffn_up, w_ffn_down, norm_final):
    d_model = w_in.shape[0]
    o_kr = _MLA_Q_LORA + _MLA_KV_LORA
    o_fq = o_kr + MLA_ROPE
    o_fk = o_fq + _FOX_WIDTH
    o_fv = o_fk + _FOX_WIDTH
    o_fl = o_fv + _FOX_WIDTH
    half = MLA_ROPE // 2

    src = -np.ones((_NPROJ,), np.int64)
    src[_O_CQ:_O_CKV] = np.arange(0, _MLA_Q_LORA)
    src[_O_CKV:_O_FQW] = np.arange(_MLA_Q_LORA, o_kr)
    for h in range(FOX_HEADS):
        d = np.arange(FOX_HEAD_DIM)
        src[_O_FQW + h * HEAD_GROUP + d] = o_fq + h * FOX_HEAD_DIM + d
        src[_O_FKW + h * HEAD_GROUP + d] = o_fk + h * FOX_HEAD_DIM + d
    src[_O_FK:_O_FV] = np.arange(o_fk, o_fv)
    src[_O_FV:_O_RA] = np.arange(o_fv, o_fl)
    src[_O_RA:_O_RA + MLA_ROPE] = np.arange(o_kr, o_fq)
    src[_O_RB:_O_RB + half] = np.arange(o_kr + half, o_fq)
    src[_O_RB + half:_O_RB + MLA_ROPE] = np.arange(o_kr, o_kr + half)
    src[_O_FL:_O_FL + FOX_HEADS] = np.arange(o_fl, o_fl + FOX_HEADS)
    w_in_r = _gather_cols(w_in, src).astype(BF16)

    src_a = -np.ones((_WIDE,), np.int64)
    src_b = -np.ones((_WIDE,), np.int64)
    for h in range(MLA_HEADS):
        base = h * MLA_QK_DIM
        src_a[h * HEAD_GROUP + np.arange(MLA_QK_DIM)] = base + np.arange(MLA_QK_DIM)
        src_b[h * HEAD_GROUP + MLA_NOPE + np.arange(half)] = base + MLA_NOPE + half + np.arange(half)
        src_b[h * HEAD_GROUP + MLA_NOPE + half + np.arange(half)] = base + MLA_NOPE + np.arange(half)
    w_uq2 = _gather_cols(w_mla_uq, np.concatenate([src_a, src_b])).astype(BF16)

    src_k = -np.ones((_WIDE,), np.int64)
    src_v = np.zeros((MLA_HEADS * MLA_V,), np.int64)
    for h in range(MLA_HEADS):
        src_k[h * HEAD_GROUP + np.arange(MLA_NOPE)] = h * (MLA_NOPE + MLA_V) + np.arange(MLA_NOPE)
        src_v[h * MLA_V + np.arange(MLA_V)] = h * (MLA_NOPE + MLA_V) + MLA_NOPE + np.arange(MLA_V)
    place_r = np.zeros((LANE, _WIDE), np.float32)
    for h in range(MLA_HEADS):
        place_r[np.arange(MLA_ROPE), h * HEAD_GROUP + MLA_NOPE + np.arange(MLA_ROPE)] = 1.0
    w_uk_wide = _gather_cols(w_mla_ukv, src_k)
    w_kexp = jnp.concatenate([w_uk_wide, jnp.asarray(place_r)], axis=0).astype(BF16)
    w_uv = _gather_cols(w_mla_ukv, src_v).astype(BF16)

    w_uk = w_mla_ukv.reshape(_MLA_KV_LORA, MLA_HEADS, MLA_NOPE + MLA_V)[:, :, :MLA_NOPE]
    w_abs = jnp.zeros((MLA_HEADS, HEAD_GROUP, _MLA_KV_LORA), F32)
    w_abs = w_abs.at[:, :MLA_NOPE, :].set(jnp.transpose(w_uk, (1, 2, 0))).astype(BF16)
    p_rope = np.zeros((HEAD_GROUP, LANE), np.float32)
    p_rope[MLA_NOPE + np.arange(MLA_ROPE), np.arange(MLA_ROPE)] = 1.0
    w_uv_h = w_mla_ukv.reshape(_MLA_KV_LORA, MLA_HEADS, MLA_NOPE + MLA_V)[:, :, MLA_NOPE:]
    eye = jnp.asarray(np.eye(MLA_HEADS, dtype=np.float32))
    w_v_wide = (jnp.transpose(w_uv_h, (1, 0, 2))[:, :, None, :] * eye[:, None, :, None])
    w_v_wide = w_v_wide.reshape(MLA_HEADS, _MLA_KV_LORA, MLA_HEADS * MLA_V).astype(BF16)

    bias_f = jnp.zeros((1, LANE), F32).at[0, :FOX_HEADS].set(b_forget.astype(F32))
    place_k = np.zeros((3 * LANE, _WIDE), np.float32)
    ones_q = np.zeros((1, _WIDE), np.float32)
    for h in range(FOX_HEADS):
        for part in range(3):
            place_k[part * LANE + h, h * HEAD_GROUP + FOX_HEAD_DIM + part] = -1.0
            ones_q[0, h * HEAD_GROUP + FOX_HEAD_DIM + part] = 1.0

    d_mla = MLA_HEADS * MLA_V
    return {
        "g_mix": norm_mix.reshape(1, d_model).astype(F32),
        "w_in": w_in_r,
        "bias_f": bias_f,
        "q_norm": mla_q_norm.reshape(1, -1).astype(F32),
        "w_uq2": w_uq2,
        "kv_norm": mla_kv_norm.reshape(1, -1).astype(F32),
        "w_kexp": w_kexp,
        "w_uv": w_uv,
        "place_k": jnp.asarray(place_k, BF16),
        "ones_q": jnp.asarray(ones_q),
        "w_abs": w_abs,
        "p_rope": jnp.asarray(p_rope, BF16),
        "w_v_wide": w_v_wide,
        "w_out_a": w_out[:d_mla].astype(BF16),
        "w_out_b": w_out[d_mla:].astype(BF16),
        "g_ffn": norm_ffn.reshape(1, d_model).astype(F32),
        "w_gate": w_ffn_gate.astype(BF16),
        "w_up": w_ffn_up.astype(BF16),
        "w_down": w_ffn_down.astype(BF16),
        "g_final": norm_final.reshape(1, d_model).astype(F32),
    }


def _rope_tables(pos):
    half = MLA_ROPE // 2
    inv_freq = ROPE_THETA ** (-jnp.arange(half, dtype=F32) / half)
    ang = pos.astype(F32)[:, None] * inv_freq[None, :]
    cos, sin = jnp.cos(ang), jnp.sin(ang)
    n = pos.shape[0]
    sc = MLA_SCALE * LOG2E
    zeros = lambda w: jnp.zeros((n, w), F32)
    tqc = jnp.concatenate([jnp.full((n, MLA_NOPE), sc, F32), sc * cos, sc * cos,
                           zeros(LANE - MLA_QK_DIM)], axis=1)
    tqs = jnp.concatenate([zeros(MLA_NOPE), -sc * sin, sc * sin, zeros(LANE - MLA_QK_DIM)], axis=1)
    tkc = jnp.concatenate([cos, cos, zeros(LANE - MLA_ROPE)], axis=1)
    tks = jnp.concatenate([-sin, sin, zeros(LANE - MLA_ROPE)], axis=1)
    return tqc, tqs, tkc, tks


def _pad_rows(a, rows):
    return jnp.concatenate([a, jnp.zeros((rows - a.shape[0],) + a.shape[1:], a.dtype)], axis=0)


PROJ_TILE = 256
ATTN_TQ = 1024
ATTN_TK = 512
FFN_TILE = 256
SAMPLE_TILE = 1024


def kernel(x_prompt, x_sample, cache_mla_latent, cache_mla_krope, cache_fox_k, cache_fox_v, cache_fox_logf, meta_tokens, norm_mix, w_in, b_forget, mla_q_norm, w_mla_uq, mla_kv_norm, w_mla_ukv, w_out, norm_ffn, w_ffn_gate, w_ffn_up, w_ffn_down, norm_final):
    depth = w_in.shape[0]
    assert depth == 1, "single-layer trunk: the meta rows' mixing output never reaches an output"
    B, seq, d_model = x_prompt.shape
    SB, s_new, _ = x_sample.shape
    n_meta = meta_tokens.shape[0]
    past = cache_mla_latent.shape[2]
    assert n_meta <= LANE

    w = _prep_weights(norm_mix[0], w_in[0], b_forget[0], mla_q_norm[0], w_mla_uq[0], mla_kv_norm[0],
                      w_mla_ukv[0], w_out[0], norm_ffn[0], w_ffn_gate[0], w_ffn_up[0], w_ffn_down[0],
                      norm_final)

    zero_carry = jnp.zeros((1, 1, LANE), F32)
    meta = _proj_call(_pad_rows(meta_tokens.astype(F32), LANE)[None], zero_carry,
                      _rope_tables(jnp.arange(LANE)), w, LANE, valid_rows=n_meta, transposed_v=True)
    (m_lat, m_kr, m_fk, m_fv, m_lf, _, m_km, m_vm, _, m_fka, m_fvb, m_carry) = meta
    frames = _proj_call(x_prompt, jnp.broadcast_to(m_carry, (B, 1, LANE)),
                        _rope_tables(n_meta + jnp.arange(seq)), w, PROJ_TILE, transposed_v=True,
                        row_offset=n_meta)
    f_qm, f_km, f_vm, f_fq, f_fka, f_fvb = frames[5:11]
    lat_p, kr_p, fk_p, fv_p, lf_p = _place_rows_call(
        [a[0, :n_meta] for a in (m_lat, m_kr, m_fk, m_fv, m_lf)],
        [a.reshape((B, n_meta + seq) + a.shape[1:]) for a in frames[:5]])

    o_mla = _attn_call(f_qm, f_km, f_vm, m_km[0], m_vm[0],
                       chunk_mask=True, n_meta=n_meta, tq=ATTN_TQ, tk=ATTN_TK, name="attn_mla")
    o_fox = _attn_call(f_fq, f_fka, f_fvb, m_fka[0], m_fvb[0],
                       chunk_mask=False, n_meta=n_meta, tq=ATTN_TQ, tk=ATTN_TK, name="attn_fox")
    y_prompt = _ffn_call(x_prompt.reshape(B * seq, d_model), o_mla.reshape(B * seq, -1),
                         o_fox.reshape(B * seq, -1), w, FFN_TILE).reshape(B, seq, d_model)

    lat_p, kr_p, lf_p = lat_p[None], kr_p[None], lf_p[None]
    fk_p = fk_p.reshape(1, B, n_meta + seq, FOX_HEADS, FOX_HEAD_DIM)
    fv_p = fv_p.reshape(1, B, n_meta + seq, FOX_HEADS, FOX_HEAD_DIM)

    logf_t = jnp.transpose(cache_fox_logf[0].astype(F32), (0, 2, 1))
    cum_cache = _cumsum_call(logf_t.reshape(SB * FOX_HEADS, past)).reshape(SB, FOX_HEADS, past)
    carry_s = jnp.zeros((SB, 1, LANE), F32).at[:, 0, :FOX_HEADS].set(cum_cache[:, :, past - 1])
    samp = _proj_call(x_sample, carry_s, _rope_tables(past + jnp.arange(s_new)), w, s_new)
    (s_lat, s_kr, s_fk, s_fv, s_lf, s_qm, _, _, s_fq, s_fka, s_fvb, _) = samp

    so_mla = _sample_mla_call(s_qm, cache_mla_latent[0], cache_mla_krope[0], s_lat, s_kr, w, SAMPLE_TILE)
    so_fox = _sample_fox_call(s_fq, cache_fox_k[0].reshape(SB, past, _FOX_WIDTH),
                              cache_fox_v[0].reshape(SB, past, _FOX_WIDTH), cum_cache, s_fka, s_fvb,
                              SAMPLE_TILE)
    y_sample = _ffn_call(x_sample.reshape(SB * s_new, d_model), so_mla.reshape(SB * s_new, -1),
                         so_fox.reshape(SB * s_new, -1), w, FFN_TILE).reshape(SB, s_new, d_model)

    heads = lambda a: a.reshape(1, SB, s_new, FOX_HEADS, FOX_HEAD_DIM)
    return (y_prompt, y_sample, lat_p, kr_p, fk_p, fv_p, lf_p,
            s_lat[None], s_kr[None], heads(s_fk), heads(s_fv), s_lf[None])
```

```python
import functools

import numpy as np
import jax
import jax.numpy as jnp
from jax import lax
from jax.experimental import pallas as pl
from jax.experimental.pallas import tpu as pltpu

CHUNK = 64
EPS = 1e-6
NEG_INF = -1e30
MLA_HEADS = 8
MLA_NOPE = 64
MLA_ROPE = 32
MLA_V = 64
MLA_QK_DIM = MLA_NOPE + MLA_ROPE
MLA_SCALE = MLA_QK_DIM ** -0.5
ROPE_THETA = 10000.0
FOX_HEADS = 8
FOX_HEAD_DIM = 64
FOX_SCALE = FOX_HEAD_DIM ** -0.5
LOG2E = 1.4426950408889634

LANE = 128
SUBLANE = 8
BF16_SUBLANES = 16
HEAD_GROUP = LANE
HEADS = 8
V7X_VMEM_LIMIT = 56 * 1024 * 1024

BF16 = jnp.bfloat16
F32 = jnp.float32

_MLA_Q_LORA = 384
_MLA_KV_LORA = 256
_FOX_WIDTH = FOX_HEADS * FOX_HEAD_DIM
_WIDE = HEADS * HEAD_GROUP
_O_CQ = 0
_O_CKV = _O_CQ + _MLA_Q_LORA
_O_FQ = _O_CKV + _MLA_KV_LORA
_O_FK = _O_FQ + _FOX_WIDTH
_O_FV = _O_FK + _FOX_WIDTH
_O_MISC = _O_FV + _FOX_WIDTH
_NPROJ = _O_MISC + LANE
_LF_LANE = MLA_ROPE


def _nt_dot(a, b):
    return lax.dot_general(a, b, (((1,), (1,)), ((), ())), preferred_element_type=F32)


def _dot(a, b):
    return jnp.dot(a, b, preferred_element_type=F32)


def _rms(x, g):
    return x * lax.rsqrt(jnp.mean(x * x, axis=-1, keepdims=True) + EPS) * g


def _split3(x):
    hi = x.astype(BF16)
    r = x - hi.astype(F32)
    mid = r.astype(BF16)
    lo = (r - mid.astype(F32)).astype(BF16)
    return hi, mid, lo


def _proj_kernel(*refs, n_base, **static):
    _proj_body(*refs[n_base:], **static)


def _proj_body(x_ref, carry_in_ref, tqc_ref, tqs_ref, tkc_ref, tks_ref,
               g_ref, w_in_ref, bias_ref, qn_ref, w_uq_ref, kvn_ref, w_kexp_ref, w_uv_ref,
               tri_ref, place_ref, ones_q_ref,
               lat_ref, kr_ref, fk_ref, fv_ref, lf_ref,
               qm_ref, km_ref, vm_ref, fq_ref, fka_ref, fvb_ref, carry_out_ref,
               carry_scr, *, valid_rows, transposed_v):
    t = pl.program_id(1)

    @pl.when(t == 0)
    def _():
        carry_scr[...] = carry_in_ref[0]

    x = x_ref[0]
    h = _rms(x, g_ref[...]).astype(BF16)
    proj = _dot(h, w_in_ref[...])

    lane = lax.broadcasted_iota(jnp.int32, (x.shape[0], LANE), 1)
    half = MLA_ROPE // 2

    def rotary(a, first, cos_tab, sin_tab):
        partner = jnp.where(lane < first + half, pltpu.roll(a, LANE - half, 1), pltpu.roll(a, half, 1))
        return a * cos_tab + partner * sin_tab

    def put(ref, val):
        ref[...] = val.reshape(ref.shape)

    cqn = _rms(proj[:, _O_CQ:_O_CKV], qn_ref[...]).astype(BF16)
    q2 = _dot(cqn, w_uq_ref[...])
    tqc, tqs = tqc_ref[...], tqs_ref[...]
    qm_ref[0] = jnp.concatenate(
        [rotary(q2[:, h * HEAD_GROUP:(h + 1) * HEAD_GROUP], MLA_NOPE, tqc, tqs) for h in range(HEADS)],
        axis=1).astype(BF16)

    latent = _rms(proj[:, _O_CKV:_O_FQ], kvn_ref[...])
    put(lat_ref, latent)
    misc = proj[:, _O_MISC:_NPROJ]
    kr = rotary(misc, 0, tkc_ref[...], tks_ref[...])
    put(kr_ref, kr[:, :MLA_ROPE])
    lat_bf = latent.astype(BF16)
    kcat = jnp.concatenate([lat_bf, kr.astype(BF16)], axis=1)
    km_ref[0] = _dot(kcat, w_kexp_ref[...]).astype(BF16)
    v_mla = _dot(lat_bf, w_uv_ref[...])
    vm_ref[0] = (v_mla.T if transposed_v else v_mla).astype(BF16)

    fq = proj[:, _O_FQ:_O_FK]
    fk = proj[:, _O_FK:_O_FV]
    fv = proj[:, _O_FV:_O_MISC]
    put(fk_ref, fk)
    put(fv_ref, fv)
    fvb_ref[0] = (fv.T if transposed_v else fv).astype(BF16)

    def widen(a):
        low = lane < FOX_HEAD_DIM
        groups = []
        for p in range(FOX_HEADS // 2):
            pair = a[:, p * LANE:(p + 1) * LANE]
            groups.append(jnp.where(low, pair, 0.0))
            groups.append(jnp.where(low, pltpu.roll(pair, LANE - FOX_HEAD_DIM, 1), 0.0))
        return jnp.concatenate(groups, axis=1)

    z = misc + bias_ref[...]
    keep = (lane >= _LF_LANE) & (lane < _LF_LANE + FOX_HEADS)
    if valid_rows is not None:
        keep = keep & (lax.broadcasted_iota(jnp.int32, z.shape, 0) < valid_rows)
    lf = jnp.where(keep, jnp.minimum(z, 0.0) - jnp.log1p(jnp.exp(-jnp.abs(z))), 0.0)
    put(lf_ref, lf[:, _LF_LANE:_LF_LANE + FOX_HEADS])
    hi, mid, lo = _split3(lf)
    tri = tri_ref[...]
    cum = _dot(tri, hi) + _dot(tri, mid) + _dot(tri, lo) + carry_scr[...]
    n_rows = cum.shape[0]
    carry_scr[...] = cum[n_rows - 1:n_rows, :]
    carry_out_ref[0] = cum[n_rows - 1:n_rows, :]
    c_hi, c_mid, c_lo = _split3(cum * LOG2E)
    kb = _dot(jnp.concatenate([c_hi, c_mid, c_lo], axis=1), place_ref[...])
    fq_ref[0] = (widen(fq) * (FOX_SCALE * LOG2E) + ones_q_ref[...]).astype(BF16)
    fka_ref[0] = (widen(fk) + kb).astype(BF16)


def _const_spec(shape):
    zeros = (0,) * len(shape)
    return pl.BlockSpec(shape, lambda *_: zeros, pipeline_mode=pl.Buffered(1))


def _proj_call(x, carry_in, tabs, w, tile, valid_rows=None, transposed_v=False, row_offset=0):
    B, L, D = x.shape
    assert L % tile == 0
    nt = L // tile
    tri = jnp.asarray(np.tril(np.ones((tile, tile), np.float32)), BF16)

    row = lambda width: pl.BlockSpec((1, tile, width), lambda b, t: (b, t, 0))
    tab = pl.BlockSpec((tile, LANE), lambda b, t: (t, 0))
    in_specs = [row(D), pl.BlockSpec((1, 1, LANE), lambda b, t: (b, 0, 0)), tab, tab, tab, tab,
                _const_spec((1, D)), _const_spec((D, _NPROJ)), _const_spec((1, LANE)),
                _const_spec((1, _MLA_Q_LORA)), _const_spec((_MLA_Q_LORA, _WIDE)),
                _const_spec((1, _MLA_KV_LORA)), _const_spec((_MLA_KV_LORA + LANE, _WIDE)),
                _const_spec((_MLA_KV_LORA, MLA_HEADS * MLA_V)),
                _const_spec((tile, tile)), _const_spec((3 * LANE, _WIDE)), _const_spec((1, _WIDE))]
    out_shapes = [
        jax.ShapeDtypeStruct((B, L, _MLA_KV_LORA), F32),
        jax.ShapeDtypeStruct((B, L, MLA_ROPE), F32),
        jax.ShapeDtypeStruct((B, L, _FOX_WIDTH), F32),
        jax.ShapeDtypeStruct((B, L, _FOX_WIDTH), F32),
        jax.ShapeDtypeStruct((B, L, FOX_HEADS), F32),
        jax.ShapeDtypeStruct((B, L, _WIDE), BF16),
        jax.ShapeDtypeStruct((B, L, _WIDE), BF16),
        jax.ShapeDtypeStruct((B, L, MLA_HEADS * MLA_V), BF16),
        jax.ShapeDtypeStruct((B, L, _WIDE), BF16),
        jax.ShapeDtypeStruct((B, L, _WIDE), BF16),
        jax.ShapeDtypeStruct((B, L, _FOX_WIDTH), BF16),
        jax.ShapeDtypeStruct((B, 1, LANE), F32),
    ]
    out_specs = [row(s.shape[-1]) for s in out_shapes[:-1]]
    out_specs.append(pl.BlockSpec((1, 1, LANE), lambda b, t: (b, 0, 0)))
    if transposed_v:
        for idx in (7, 10):
            vw = out_shapes[idx].shape[-1]
            out_shapes[idx] = jax.ShapeDtypeStruct((B, vw, L), BF16)
            out_specs[idx] = pl.BlockSpec((1, vw, tile), lambda b, t: (b, 0, t))
    if row_offset:
        assert row_offset % SUBLANE == 0 and L % SUBLANE == 0
        for idx in range(5):
            width = out_shapes[idx].shape[-1]
            out_shapes[idx] = jax.ShapeDtypeStruct((B * (row_offset + L), width), F32)
            out_specs[idx] = pl.BlockSpec(
                (pl.Element(tile), pl.Element(width)),
                lambda b, t: (pl.multiple_of(b * (row_offset + L) + row_offset + t * tile, SUBLANE), 0))
    bases = [jnp.zeros(s.shape, s.dtype) for s in out_shapes[:5]] if row_offset else []
    n_base = len(bases)
    return pl.pallas_call(
        functools.partial(_proj_kernel, n_base=n_base, valid_rows=valid_rows, transposed_v=transposed_v),
        out_shape=out_shapes,
        grid=(B, nt),
        in_specs=[pl.BlockSpec(memory_space=pl.ANY)] * n_base + in_specs,
        out_specs=out_specs,
        input_output_aliases={i: i for i in range(n_base)},
        scratch_shapes=[pltpu.VMEM((1, LANE), F32)],
        compiler_params=pltpu.CompilerParams(
            dimension_semantics=("parallel", "arbitrary"), vmem_limit_bytes=V7X_VMEM_LIMIT),
        name="proj",
    )(*bases, x, carry_in, *tabs, w["g_mix"], w["w_in"], w["bias_f"], w["q_norm"], w["w_uq2"],
      w["kv_norm"], w["w_kexp"], w["w_uv"], tri, w["place_k"], w["ones_q"])


def _place_rows_kernel(*refs):
    n = len(refs) // 3
    for src, out in zip(refs[:n], refs[2 * n:]):
        out[0] = src[...]


def _place_rows_call(rows, dests):
    n = len(rows)
    B = dests[0].shape[0]
    in_specs = [pl.BlockSpec(r.shape, lambda b, nd=r.ndim: (0,) * nd) for r in rows]
    in_specs += [pl.BlockSpec(memory_space=pl.ANY) for _ in dests]
    out_specs = [pl.BlockSpec((1,) + r.shape, lambda b, nd=r.ndim: (b,) + (0,) * nd) for r in rows]
    return pl.pallas_call(
        _place_rows_kernel,
        out_shape=[jax.ShapeDtypeStruct(d.shape, d.dtype) for d in dests],
        grid=(B,),
        in_specs=in_specs,
        out_specs=out_specs,
        input_output_aliases={n + i: i for i in range(n)},
        compiler_params=pltpu.CompilerParams(dimension_semantics=("arbitrary",)),
        name="place_meta_rows",
    )(*rows, *dests)


def _softmax_update(h, s, v_pair, m_scr, l_scr, acc_scr):
    m_prev = m_scr[h]
    m_new = jnp.maximum(m_prev, jnp.max(s, axis=1, keepdims=True))
    alpha = jnp.exp2(m_prev - m_new)
    p = jnp.exp2(s - jnp.tile(m_new, (1, s.shape[1] // LANE)))
    l_scr[h] = alpha * l_scr[h] + jnp.sum(p, axis=1, keepdims=True)
    acc_scr[h] = alpha * acc_scr[h] + _dot(p.astype(BF16), v_pair)
    m_scr[h] = m_new


QK_LOOKAHEAD = 8
UNIT_Q = 256
UNIT_K = 256
ONES_ROWS = 16


def _with_ones_rows(v_t):
    row = lax.broadcasted_iota(jnp.int32, (ONES_ROWS, v_t.shape[1]), 0)
    return jnp.concatenate([v_t, jnp.where(row == 0, 1.0, 0.0).astype(v_t.dtype)], axis=0)


def _softmax_update_t(h, qs, s_t, v_t, m_scr, acc_scr):
    uk, uq = s_t.shape
    s3 = s_t.reshape(uk // SUBLANE, SUBLANE, uq)
    m_cur = jnp.max(s3, axis=0)
    for shift in (4, 2, 1):
        m_cur = jnp.maximum(m_cur, pltpu.roll(m_cur, shift, 0))
    m_prev = m_scr[h, :, qs]
    m_new = jnp.maximum(m_prev, m_cur)
    alpha = jnp.exp2(m_prev - m_new)
    p_t = jnp.exp2(s3 - m_new[None]).reshape(uk, uq)
    pv = _dot(v_t, p_t.astype(BF16))
    rows = pv.shape[0]
    acc = acc_scr[h, :, qs].reshape(rows // SUBLANE, SUBLANE, uq)
    acc_scr[h, :, qs] = (alpha[None] * acc).reshape(rows, uq) + pv
    m_scr[h, :, qs] = m_new


def _attn_kernel(qi_ref, kj_ref, q_ref, k_ref, vt_ref, kmeta_ref, vtmeta_ref, o_ref,
                 m_scr, acc_scr, *, chunk_mask, n_meta):
    p_id = pl.program_id(1)
    i = qi_ref[p_id]
    j = kj_ref[p_id]
    tq = q_ref.shape[1]
    tk = k_ref.shape[1]
    dv = acc_scr.shape[1] - ONES_ROWS

    def head(ref2d, h):
        return ref2d[:, h * HEAD_GROUP:(h + 1) * HEAD_GROUP]

    @pl.when(j == 0)
    def _():
        n_pad = kmeta_ref.shape[0]
        n_rows = -(-n_meta // BF16_SUBLANES) * BF16_SUBLANES
        row = lax.broadcasted_iota(jnp.int32, (n_rows, tq), 0)
        scores = [_nt_dot(kmeta_ref[:n_rows, h * HEAD_GROUP:(h + 1) * HEAD_GROUP], head(q_ref.at[0], h))
                  for h in range(HEADS)]
        for h in range(HEADS):
            s3 = jnp.where(row < n_meta, scores[h], NEG_INF).reshape(n_rows // SUBLANE, SUBLANE, tq)
            m_new = jnp.max(s3, axis=0)
            for shift in (4, 2, 1):
                m_new = jnp.maximum(m_new, pltpu.roll(m_new, shift, 0))
            p_t = jnp.exp2(s3 - m_new[None]).reshape(n_rows, tq).astype(BF16)
            p_t = jnp.concatenate([p_t, jnp.zeros((n_pad - n_rows, tq), BF16)], axis=0)
            acc_scr[h] = _dot(_with_ones_rows(vtmeta_ref[h * dv:(h + 1) * dv, :]), p_t)
            m_scr[h] = m_new

    uq = min(UNIT_Q, tq)
    uk = min(UNIT_K, tk)

    def block(key_shift):
        masked = key_shift is not None
        units = [(h, q0, k0) for h in range(HEADS) for k0 in range(0, tk, uk) for q0 in range(0, tq, uq)]
        if masked:
            units = [u for u in units if key_shift + u[2] < u[1] + uq]
            kk = lax.broadcasted_iota(jnp.int32, (uk, uq), 0) + key_shift
            qq = lax.broadcasted_iota(jnp.int32, (uk, uq), 1)

        def qk(u):
            h, q0, k0 = u
            return _nt_dot(k_ref[0, k0:k0 + uk, h * HEAD_GROUP:(h + 1) * HEAD_GROUP],
                           q_ref[0, q0:q0 + uq, h * HEAD_GROUP:(h + 1) * HEAD_GROUP])

        scores = {}
        for n in range(-QK_LOOKAHEAD, len(units)):
            if n + QK_LOOKAHEAD < len(units):
                scores[n + QK_LOOKAHEAD] = qk(units[n + QK_LOOKAHEAD])
            if n >= 0:
                h, q0, k0 = units[n]
                s_t = scores.pop(n)
                if masked and key_shift + k0 + uk > q0 + 1:
                    if chunk_mask:
                        valid = (kk + k0) // CHUNK <= (qq + q0) // CHUNK
                    else:
                        valid = kk + k0 <= qq + q0
                    s_t = jnp.where(valid, s_t, NEG_INF)
                v_t = _with_ones_rows(vt_ref[0, h * dv:(h + 1) * dv, k0:k0 + uk])
                _softmax_update_t(h, slice(q0, q0 + uq), s_t, v_t, m_scr, acc_scr)

    ratio = tq // tk
    d = j - ratio * i

    @pl.when(d < 0)
    def _():
        block(None)

    for dd in range(ratio):
        @pl.when(d == dd)
        def _(dd=dd):
            block(dd * tk)

    @pl.when(d == ratio - 1)
    def _():
        def normalised(h):
            return acc_scr[h, :dv, :] / acc_scr[h, dv:dv + 1, :]

        for hp in range(HEADS // 2):
            o_pair_t = jnp.concatenate([normalised(2 * hp), normalised(2 * hp + 1)], axis=0)
            o_ref[0, :, hp * 2 * dv:(hp + 1) * 2 * dv] = o_pair_t.T.astype(o_ref.dtype)


def _attn_call(q, k, v_t, k_meta, v_t_meta, *, chunk_mask, n_meta, tq, tk, name):
    B, F, _ = q.shape
    tq, tk = min(tq, F), min(tk, F)
    assert F % tq == 0 and tq % tk == 0 and tk % CHUNK == 0
    ratio = tq // tk
    pairs = [(i, j) for i in range(F // tq) for j in range(ratio * (i + 1))]
    qi = jnp.asarray([p[0] for p in pairs], jnp.int32)
    kj = jnp.asarray([p[1] for p in pairs], jnp.int32)
    vw = v_t.shape[1]
    grid_spec = pltpu.PrefetchScalarGridSpec(
        num_scalar_prefetch=2,
        grid=(B, len(pairs)),
        in_specs=[
            pl.BlockSpec((1, tq, _WIDE), lambda b, p, qi, kj: (b, qi[p], 0)),
            pl.BlockSpec((1, tk, _WIDE), lambda b, p, qi, kj: (b, kj[p], 0)),
            pl.BlockSpec((1, vw, tk), lambda b, p, qi, kj: (b, 0, kj[p])),
            pl.BlockSpec((LANE, _WIDE), lambda b, p, qi, kj: (0, 0)),
            pl.BlockSpec((vw, LANE), lambda b, p, qi, kj: (0, 0)),
        ],
        out_specs=pl.BlockSpec((1, tq, vw), lambda b, p, qi, kj: (b, qi[p], 0)),
        scratch_shapes=[pltpu.VMEM((HEADS, SUBLANE, tq), F32),
                        pltpu.VMEM((HEADS, vw // HEADS + ONES_ROWS, tq), F32)],
    )
    return pl.pallas_call(
        functools.partial(_attn_kernel, chunk_mask=chunk_mask, n_meta=n_meta),
        out_shape=jax.ShapeDtypeStruct((B, F, vw), BF16),
        grid_spec=grid_spec,
        compiler_params=pltpu.CompilerParams(
            dimension_semantics=("parallel", "arbitrary"), vmem_limit_bytes=V7X_VMEM_LIMIT),
        name=name,
    )(qi, kj, q, k, v_t, k_meta, v_t_meta)


def _ffn_kernel(x_ref, oa_ref, ob_ref, woa_ref, wob_ref, g_ref, wg_ref, wu_ref, wd_ref, gf_ref, y_ref):
    x1 = x_ref[...] + _dot(oa_ref[...], woa_ref[...]) + _dot(ob_ref[...], wob_ref[...])
    h = _rms(x1, g_ref[...]).astype(BF16)
    gate = _dot(h, wg_ref[...])
    up = _dot(h, wu_ref[...])
    act = (gate * jax.nn.sigmoid(gate) * up).astype(BF16)
    x2 = x1 + _dot(act, wd_ref[...])
    y_ref[...] = _rms(x2, gf_ref[...])


def _ffn_call(x, oa, ob, w, tile):
    R, D = x.shape
    tile = min(tile, R)
    assert R % tile == 0
    dm = oa.shape[-1]
    dff = w["w_gate"].shape[-1]
    row = lambda width: pl.BlockSpec((tile, width), lambda r: (r, 0))
    return pl.pallas_call(
        _ffn_kernel,
        out_shape=jax.ShapeDtypeStruct((R, D), F32),
        grid=(R // tile,),
        in_specs=[row(D), row(dm), row(dm),
                  _const_spec((dm, D)), _const_spec((dm, D)), _const_spec((1, D)),
                  _const_spec((D, dff)), _const_spec((D, dff)), _const_spec((dff, D)),
                  _const_spec((1, D))],
        out_specs=row(D),
        compiler_params=pltpu.CompilerParams(
            dimension_semantics=("parallel",), vmem_limit_bytes=V7X_VMEM_LIMIT),
        name="ffn",
    )(x, oa, ob, w["w_out_a"], w["w_out_b"], w["g_ffn"], w["w_gate"], w["w_up"], w["w_down"],
      w["g_final"])


_CUM_CHUNK = 256


def _cumsum_kernel(x_ref, u_ref, o_ref):
    rows, n = x_ref.shape
    u = u_ref[...]
    carry = jnp.zeros((rows, 1), F32)
    for c in range(n // _CUM_CHUNK):
        hi, mid, lo = _split3(x_ref[:, c * _CUM_CHUNK:(c + 1) * _CUM_CHUNK])
        y = _dot(hi, u) + _dot(mid, u) + _dot(lo, u) + carry
        o_ref[:, c * _CUM_CHUNK:(c + 1) * _CUM_CHUNK] = y
        carry = y[:, _CUM_CHUNK - 1:_CUM_CHUNK]


def _cumsum_call(x):
    rows, n = x.shape
    assert n % _CUM_CHUNK == 0
    u = jnp.asarray(np.triu(np.ones((_CUM_CHUNK, _CUM_CHUNK), np.float32)), BF16)
    return pl.pallas_call(
        _cumsum_kernel,
        out_shape=jax.ShapeDtypeStruct((rows, n), F32),
        compiler_params=pltpu.CompilerParams(vmem_limit_bytes=V7X_VMEM_LIMIT),
        name="cache_cumsum",
    )(x, u)


def _sample_fox_kernel(fq_ref, k_ref, v_ref, ck_ref, fkn_ref, fvn_ref, o_ref,
                       qm_scr, m_scr, l_scr, acc_scr):
    j = pl.program_id(1)
    nj = pl.num_programs(1)
    s_new = fq_ref.shape[1]
    lane = lax.broadcasted_iota(jnp.int32, (s_new, LANE), 1)

    @pl.when(j == 0)
    def _():
        m_scr[...] = jnp.full(m_scr.shape, NEG_INF, F32)
        l_scr[...] = jnp.zeros(l_scr.shape, F32)
        acc_scr[...] = jnp.zeros(acc_scr.shape, F32)
        for h in range(HEADS):
            q_h = jnp.where(lane < FOX_HEAD_DIM, fq_ref[0, :, h * HEAD_GROUP:(h + 1) * HEAD_GROUP], 0)
            if h % 2 == 1:
                q_h = pltpu.roll(q_h.astype(F32), FOX_HEAD_DIM, 1).astype(BF16)
            qm_scr[h] = q_h

    ck2 = ck_ref[0] * LOG2E
    for hp in range(HEADS // 2):
        k_pair = k_ref[0, :, hp * LANE:(hp + 1) * LANE].astype(BF16)
        v_pair = v_ref[0, :, hp * LANE:(hp + 1) * LANE].astype(BF16)
        for h in (2 * hp, 2 * hp + 1):
            s = _nt_dot(qm_scr[h], k_pair) - ck2[h:h + 1, :]
            _softmax_update(h, s, v_pair, m_scr, l_scr, acc_scr)

    @pl.when(j == nj - 1)
    def _():
        r = lax.broadcasted_iota(jnp.int32, (s_new, s_new), 0)
        c = lax.broadcasted_iota(jnp.int32, (s_new, s_new), 1)
        for h in range(HEADS):
            s = _nt_dot(fq_ref[0, :, h * HEAD_GROUP:(h + 1) * HEAD_GROUP],
                        fkn_ref[0, :, h * HEAD_GROUP:(h + 1) * HEAD_GROUP])
            s = jnp.where(c <= r, s, NEG_INF)
            m_prev = m_scr[h]
            m_new = jnp.maximum(m_prev, jnp.max(s, axis=1, keepdims=True))
            alpha = jnp.exp2(m_prev - m_new)
            p = jnp.exp2(s - m_new[:, :s_new])
            l_scr[h] = alpha * l_scr[h] + jnp.sum(p, axis=1, keepdims=True)
            pair = (h // 2) * LANE
            acc_scr[h] = alpha * acc_scr[h] + _dot(p.astype(BF16), fvn_ref[0, :, pair:pair + LANE])
        for hp in range(HEADS // 2):
            even = acc_scr[2 * hp] / l_scr[2 * hp]
            odd = acc_scr[2 * hp + 1] / l_scr[2 * hp + 1]
            o_ref[0, :, hp * LANE:(hp + 1) * LANE] = jnp.where(lane < LANE // 2, even, odd).astype(o_ref.dtype)


def _sample_fox_call(fq, cache_k, cache_v, cum_cache, fka_new, fv_new, tile):
    B, S, _ = fq.shape
    past = cache_k.shape[1]
    assert past % tile == 0
    w = cache_k.shape[-1]
    return pl.pallas_call(
        _sample_fox_kernel,
        out_shape=jax.ShapeDtypeStruct((B, S, w), BF16),
        grid=(B, past // tile),
        in_specs=[pl.BlockSpec((1, S, _WIDE), lambda b, j: (b, 0, 0)),
                  pl.BlockSpec((1, tile, w), lambda b, j: (b, j, 0)),
                  pl.BlockSpec((1, tile, w), lambda b, j: (b, j, 0)),
                  pl.BlockSpec((1, HEADS, tile), lambda b, j: (b, 0, j)),
                  pl.BlockSpec((1, S, _WIDE), lambda b, j: (b, 0, 0)),
                  pl.BlockSpec((1, S, w), lambda b, j: (b, 0, 0))],
        out_specs=pl.BlockSpec((1, S, w), lambda b, j: (b, 0, 0)),
        scratch_shapes=[pltpu.VMEM((HEADS, S, LANE), BF16),
                        pltpu.VMEM((HEADS, S, LANE), F32),
                        pltpu.VMEM((HEADS, S, LANE), F32),
                        pltpu.VMEM((HEADS, S, LANE), F32)],
        compiler_params=pltpu.CompilerParams(
            dimension_semantics=("parallel", "arbitrary"), vmem_limit_bytes=V7X_VMEM_LIMIT),
        name="sample_fox",
    )(fq, cache_k, cache_v, cum_cache, fka_new, fv_new)


def _sample_mla_kernel(q_ref, lat_ref, kr_ref, latn_ref, krn_ref, wabs_ref, prope_ref, wv_ref, o_ref,
                       ql_scr, qr_scr, m_scr, l_scr, acc_scr):
    j = pl.program_id(1)
    nj = pl.num_programs(1)
    s_new = q_ref.shape[1]

    @pl.when(j == 0)
    def _():
        m_scr[...] = jnp.full(m_scr.shape, NEG_INF, F32)
        l_scr[...] = jnp.zeros(l_scr.shape, F32)
        acc_scr[...] = jnp.zeros(acc_scr.shape, F32)
        for h in range(HEADS):
            q_h = q_ref[0, :, h * HEAD_GROUP:(h + 1) * HEAD_GROUP]
            ql_scr[h * s_new:(h + 1) * s_new, :] = _dot(q_h, wabs_ref[h]).astype(BF16)
            qr_scr[h * s_new:(h + 1) * s_new, :] = _dot(q_h, prope_ref[...]).astype(BF16)

    def update(lat, kr):
        s = _nt_dot(ql_scr[...], lat) + _nt_dot(qr_scr[...], kr)
        m_prev = m_scr[...]
        m_new = jnp.maximum(m_prev, jnp.max(s, axis=1, keepdims=True))
        alpha = jnp.exp2(m_prev - m_new)
        p = jnp.exp2(s - m_new[:, :1])
        l_scr[...] = alpha * l_scr[...] + jnp.sum(p, axis=1, keepdims=True)
        acc_scr[...] = jnp.tile(alpha, (1, acc_scr.shape[1] // LANE)) * acc_scr[...] + _dot(p.astype(BF16), lat)
        m_scr[...] = m_new

    def pad_rope(kr):
        return jnp.concatenate([kr, jnp.zeros((kr.shape[0], LANE - MLA_ROPE), kr.dtype)], axis=1)

    update(lat_ref[0].astype(BF16), pad_rope(kr_ref[0]).astype(BF16))

    @pl.when(j == nj - 1)
    def _():
        update(latn_ref[0].astype(BF16), pad_rope(krn_ref[0]).astype(BF16))
        o_lat = (acc_scr[...] / jnp.tile(l_scr[...], (1, acc_scr.shape[1] // LANE))).astype(BF16)
        out = _dot(o_lat[0:s_new], wv_ref[0])
        for h in range(1, HEADS):
            out = out + _dot(o_lat[h * s_new:(h + 1) * s_new], wv_ref[h])
        o_ref[0] = out.astype(o_ref.dtype)


def _sample_mla_call(q, cache_lat, cache_kr, lat_new, kr_new, w, tile):
    B, S, _ = q.shape
    past = cache_lat.shape[1]
    assert past % tile == 0
    c = cache_lat.shape[-1]
    ow = MLA_HEADS * MLA_V
    return pl.pallas_call(
        _sample_mla_kernel,
        out_shape=jax.ShapeDtypeStruct((B, S, ow), BF16),
        grid=(B, past // tile),
        in_specs=[pl.BlockSpec((1, S, _WIDE), lambda b, j: (b, 0, 0)),
                  pl.BlockSpec((1, tile, c), lambda b, j: (b, j, 0)),
                  pl.BlockSpec((1, tile, MLA_ROPE), lambda b, j: (b, j, 0)),
                  pl.BlockSpec((1, S, c), lambda b, j: (b, 0, 0)),
                  pl.BlockSpec((1, S, MLA_ROPE), lambda b, j: (b, 0, 0)),
                  _const_spec((HEADS, HEAD_GROUP, c)), _const_spec((HEAD_GROUP, LANE)),
                  _const_spec((HEADS, c, ow))],
        out_specs=pl.BlockSpec((1, S, ow), lambda b, j: (b, 0, 0)),
        scratch_shapes=[pltpu.VMEM((HEADS * S, c), BF16),
                        pltpu.VMEM((HEADS * S, LANE), BF16),
                        pltpu.VMEM((HEADS * S, LANE), F32),
                        pltpu.VMEM((HEADS * S, LANE), F32),
                        pltpu.VMEM((HEADS * S, c), F32)],
        compiler_params=pltpu.CompilerParams(
            dimension_semantics=("parallel", "arbitrary"), vmem_limit_bytes=V7X_VMEM_LIMIT),
        name="sample_mla",
    )(q, cache_lat, cache_kr, lat_new, kr_new, w["w_abs"], w["p_rope"], w["w_v_wide"])


def _gather_cols(w, src):
    src = np.asarray(src)
    pieces, start = [], 0
    for i in range(1, len(src) + 1):
        run_ends = i == len(src) or (src[i] != src[i - 1] + 1 if src[i - 1] >= 0 else src[i] >= 0)
        if run_ends:
            if src[start] >= 0:
                pieces.append(w[:, int(src[start]):int(src[start]) + i - start])
            else:
                pieces.append(jnp.zeros((w.shape[0], i - start), w.dtype))
            start = i
    return jnp.concatenate(pieces, axis=1)


def _prep_weights(norm_mix, w_in, b_forget, mla_q_norm, w_mla_uq, mla_kv_norm, w_mla_ukv, w_out,
                  norm_ffn, w_ffn_gate, w_ffn_up, w_ffn_down, norm_final):
    d_model = w_in.shape[0]
    o_kr = _MLA_Q_LORA + _MLA_KV_LORA
    o_fq = o_kr + MLA_ROPE
    o_fk = o_fq + _FOX_WIDTH
    o_fv = o_fk + _FOX_WIDTH
    o_fl = o_fv + _FOX_WIDTH
    half = MLA_ROPE // 2

    src = -np.ones((_NPROJ,), np.int64)
    src[_O_CQ:_O_CKV] = np.arange(0, _MLA_Q_LORA)
    src[_O_CKV:_O_FQ] = np.arange(_MLA_Q_LORA, o_kr)
    src[_O_FQ:_O_FK] = np.arange(o_fq, o_fk)
    src[_O_FK:_O_FV] = np.arange(o_fk, o_fv)
    src[_O_FV:_O_MISC] = np.arange(o_fv, o_fl)
    src[_O_MISC:_O_MISC + MLA_ROPE] = np.arange(o_kr, o_fq)
    src[_O_MISC + _LF_LANE:_O_MISC + _LF_LANE + FOX_HEADS] = np.arange(o_fl, o_fl + FOX_HEADS)
    w_in_r = _gather_cols(w_in, src).astype(BF16)

    src_a = -np.ones((_WIDE,), np.int64)
    for h in range(MLA_HEADS):
        src_a[h * HEAD_GROUP + np.arange(MLA_QK_DIM)] = h * MLA_QK_DIM + np.arange(MLA_QK_DIM)
    w_uq2 = _gather_cols(w_mla_uq, src_a).astype(BF16)

    src_k = -np.ones((_WIDE,), np.int64)
    src_v = np.zeros((MLA_HEADS * MLA_V,), np.int64)
    for h in range(MLA_HEADS):
        src_k[h * HEAD_GROUP + np.arange(MLA_NOPE)] = h * (MLA_NOPE + MLA_V) + np.arange(MLA_NOPE)
        src_v[h * MLA_V + np.arange(MLA_V)] = h * (MLA_NOPE + MLA_V) + MLA_NOPE + np.arange(MLA_V)
    place_r = np.zeros((LANE, _WIDE), np.float32)
    for h in range(MLA_HEADS):
        place_r[np.arange(MLA_ROPE), h * HEAD_GROUP + MLA_NOPE + np.arange(MLA_ROPE)] = 1.0
    w_uk_wide = _gather_cols(w_mla_ukv, src_k)
    w_kexp = jnp.concatenate([w_uk_wide, jnp.asarray(place_r)], axis=0).astype(BF16)
    w_uv = _gather_cols(w_mla_ukv, src_v).astype(BF16)

    w_uk = w_mla_ukv.reshape(_MLA_KV_LORA, MLA_HEADS, MLA_NOPE + MLA_V)[:, :, :MLA_NOPE]
    w_abs = jnp.zeros((MLA_HEADS, HEAD_GROUP, _MLA_KV_LORA), F32)
    w_abs = w_abs.at[:, :MLA_NOPE, :].set(jnp.transpose(w_uk, (1, 2, 0))).astype(BF16)
    p_rope = np.zeros((HEAD_GROUP, LANE), np.float32)
    p_rope[MLA_NOPE + np.arange(MLA_ROPE), np.arange(MLA_ROPE)] = 1.0
    w_uv_h = w_mla_ukv.reshape(_MLA_KV_LORA, MLA_HEADS, MLA_NOPE + MLA_V)[:, :, MLA_NOPE:]
    eye = jnp.asarray(np.eye(MLA_HEADS, dtype=np.float32))
    w_v_wide = (jnp.transpose(w_uv_h, (1, 0, 2))[:, :, None, :] * eye[:, None, :, None])
    w_v_wide = w_v_wide.reshape(MLA_HEADS, _MLA_KV_LORA, MLA_HEADS * MLA_V).astype(BF16)

    bias_f = jnp.zeros((1, LANE), F32).at[0, _LF_LANE:_LF_LANE + FOX_HEADS].set(b_forget.astype(F32))
    place_k = np.zeros((3 * LANE, _WIDE), np.float32)
    ones_q = np.zeros((1, _WIDE), np.float32)
    for h in range(FOX_HEADS):
        for part in range(3):
            place_k[part * LANE + _LF_LANE + h, h * HEAD_GROUP + FOX_HEAD_DIM + part] = -1.0
            ones_q[0, h * HEAD_GROUP + FOX_HEAD_DIM + part] = 1.0

    d_mla = MLA_HEADS * MLA_V
    return {
        "g_mix": norm_mix.reshape(1, d_model).astype(F32),
        "w_in": w_in_r,
        "bias_f": bias_f,
        "q_norm": mla_q_norm.reshape(1, -1).astype(F32),
        "w_uq2": w_uq2,
        "kv_norm": mla_kv_norm.reshape(1, -1).astype(F32),
        "w_kexp": w_kexp,
        "w_uv": w_uv,
        "place_k": jnp.asarray(place_k, BF16),
        "ones_q": jnp.asarray(ones_q),
        "w_abs": w_abs,
        "p_rope": jnp.asarray(p_rope, BF16),
        "w_v_wide": w_v_wide,
        "w_out_a": w_out[:d_mla].astype(BF16),
        "w_out_b": w_out[d_mla:].astype(BF16),
        "g_ffn": norm_ffn.reshape(1, d_model).astype(F32),
        "w_gate": w_ffn_gate.astype(BF16),
        "w_up": w_ffn_up.astype(BF16),
        "w_down": w_ffn_down.astype(BF16),
        "g_final": norm_final.reshape(1, d_model).astype(F32),
    }


def _rope_tables(pos):
    half = MLA_ROPE // 2
    inv_freq = ROPE_THETA ** (-jnp.arange(half, dtype=F32) / half)
    ang = pos.astype(F32)[:, None] * inv_freq[None, :]
    cos, sin = jnp.cos(ang), jnp.sin(ang)
    n = pos.shape[0]
    sc = MLA_SCALE * LOG2E
    zeros = lambda w: jnp.zeros((n, w), F32)
    tqc = jnp.concatenate([jnp.full((n, MLA_NOPE), sc, F32), sc * cos, sc * cos,
                           zeros(LANE - MLA_QK_DIM)], axis=1)
    tqs = jnp.concatenate([zeros(MLA_NOPE), -sc * sin, sc * sin, zeros(LANE - MLA_QK_DIM)], axis=1)
    tkc = jnp.concatenate([cos, cos, zeros(LANE - MLA_ROPE)], axis=1)
    tks = jnp.concatenate([-sin, sin, zeros(LANE - MLA_ROPE)], axis=1)
    return tqc, tqs, tkc, tks


def _pad_rows(a, rows):
    return jnp.concatenate([a, jnp.zeros((rows - a.shape[0],) + a.shape[1:], a.dtype)], axis=0)


PROJ_TILE = 256
ATTN_TQ = 1024
ATTN_TK = 512
FFN_TILE = 256
SAMPLE_TILE = 1024


def kernel(x_prompt, x_sample, cache_mla_latent, cache_mla_krope, cache_fox_k, cache_fox_v, cache_fox_logf, meta_tokens, norm_mix, w_in, b_forget, mla_q_norm, w_mla_uq, mla_kv_norm, w_mla_ukv, w_out, norm_ffn, w_ffn_gate, w_ffn_up, w_ffn_down, norm_final):
    depth = w_in.shape[0]
    assert depth == 1, "single-layer trunk: the meta rows' mixing output never reaches an output"
    B, seq, d_model = x_prompt.shape
    SB, s_new, _ = x_sample.shape
    n_meta = meta_tokens.shape[0]
    past = cache_mla_latent.shape[2]
    assert n_meta <= LANE

    w = _prep_weights(norm_mix[0], w_in[0], b_forget[0], mla_q_norm[0], w_mla_uq[0], mla_kv_norm[0],
                      w_mla_ukv[0], w_out[0], norm_ffn[0], w_ffn_gate[0], w_ffn_up[0], w_ffn_down[0],
                      norm_final)

    zero_carry = jnp.zeros((1, 1, LANE), F32)
    meta = _proj_call(_pad_rows(meta_tokens.astype(F32), LANE)[None], zero_carry,
                      _rope_tables(jnp.arange(LANE)), w, LANE, valid_rows=n_meta, transposed_v=True)
    (m_lat, m_kr, m_fk, m_fv, m_lf, _, m_km, m_vm, _, m_fka, m_fvb, m_carry) = meta
    frames = _proj_call(x_prompt, jnp.broadcast_to(m_carry, (B, 1, LANE)),
                        _rope_tables(n_meta + jnp.arange(seq)), w, PROJ_TILE, transposed_v=True,
                        row_offset=n_meta)
    f_qm, f_km, f_vm, f_fq, f_fka, f_fvb = frames[5:11]
    lat_p, kr_p, fk_p, fv_p, lf_p = _place_rows_call(
        [a[0, :n_meta] for a in (m_lat, m_kr, m_fk, m_fv, m_lf)],
        [a.reshape((B, n_meta + seq) + a.shape[1:]) for a in frames[:5]])

    o_mla = _attn_call(f_qm, f_km, f_vm, m_km[0], m_vm[0],
                       chunk_mask=True, n_meta=n_meta, tq=ATTN_TQ, tk=ATTN_TK, name="attn_mla")
    o_fox = _attn_call(f_fq, f_fka, f_fvb, m_fka[0], m_fvb[0],
                       chunk_mask=False, n_meta=n_meta, tq=ATTN_TQ, tk=ATTN_TK, name="attn_fox")
    y_prompt = _ffn_call(x_prompt.reshape(B * seq, d_model), o_mla.reshape(B * seq, -1),
                         o_fox.reshape(B * seq, -1), w, FFN_TILE).reshape(B, seq, d_model)

    lat_p, kr_p, lf_p = lat_p[None], kr_p[None], lf_p[None]
    fk_p = fk_p.reshape(1, B, n_meta + seq, FOX_HEADS, FOX_HEAD_DIM)
    fv_p = fv_p.reshape(1, B, n_meta + seq, FOX_HEADS, FOX_HEAD_DIM)

    logf_t = jnp.transpose(cache_fox_logf[0].astype(F32), (0, 2, 1))
    cum_cache = _cumsum_call(logf_t.reshape(SB * FOX_HEADS, past)).reshape(SB, FOX_HEADS, past)
    carry_s = jnp.zeros((SB, 1, LANE), F32).at[:, 0, _LF_LANE:_LF_LANE + FOX_HEADS].set(
        cum_cache[:, :, past - 1])
    samp = _proj_call(x_sample, carry_s, _rope_tables(past + jnp.arange(s_new)), w, s_new)
    (s_lat, s_kr, s_fk, s_fv, s_lf, s_qm, _, _, s_fq, s_fka, s_fvb, _) = samp

    so_mla = _sample_mla_call(s_qm, cache_mla_latent[0], cache_mla_krope[0], s_lat, s_kr, w, SAMPLE_TILE)
    so_fox = _sample_fox_call(s_fq, cache_fox_k[0].reshape(SB, past, _FOX_WIDTH),
                              cache_fox_v[0].reshape(SB, past, _FOX_WIDTH), cum_cache, s_fka, s_fvb,
                              SAMPLE_TILE)
    y_sample = _ffn_call(x_sample.reshape(SB * s_new, d_model), so_mla.reshape(SB * s_new, -1),
                         so_fox.reshape(SB * s_new, -1), w, FFN_TILE).reshape(SB, s_new, d_model)

    heads = lambda a: a.reshape(1, SB, s_new, FOX_HEADS, FOX_HEAD_DIM)
    return (y_prompt, y_sample, lat_p, kr_p, fk_p, fv_p, lf_p,
            s_lat[None], s_kr[None], heads(s_fk), heads(s_fv), s_lf[None])
```

```python
import functools

import numpy as np
import jax
import jax.numpy as jnp
from jax import lax
from jax.experimental import pallas as pl
from jax.experimental.pallas import tpu as pltpu

CHUNK = 64
EPS = 1e-6
NEG_INF = -1e30
MLA_HEADS = 8
MLA_NOPE = 64
MLA_ROPE = 32
MLA_V = 64
MLA_QK_DIM = MLA_NOPE + MLA_ROPE
MLA_SCALE = MLA_QK_DIM ** -0.5
ROPE_THETA = 10000.0
FOX_HEADS = 8
FOX_HEAD_DIM = 64
FOX_SCALE = FOX_HEAD_DIM ** -0.5
LOG2E = 1.4426950408889634

LANE = 128
SUBLANE = 8
BF16_SUBLANES = 16
HEAD_GROUP = LANE
HEADS = 8
V7X_VMEM_LIMIT = 56 * 1024 * 1024

BF16 = jnp.bfloat16
F32 = jnp.float32

_MLA_Q_LORA = 384
_MLA_KV_LORA = 256
_FOX_WIDTH = FOX_HEADS * FOX_HEAD_DIM
_WIDE = HEADS * HEAD_GROUP
_O_CQ = 0
_O_CKV = _O_CQ + _MLA_Q_LORA
_O_FQ = _O_CKV + _MLA_KV_LORA
_O_FK = _O_FQ + _FOX_WIDTH
_O_FV = _O_FK + _FOX_WIDTH
_O_MISC = _O_FV + _FOX_WIDTH
_NPROJ = _O_MISC + LANE
_LF_LANE = MLA_ROPE


def _nt_dot(a, b):
    return lax.dot_general(a, b, (((1,), (1,)), ((), ())), preferred_element_type=F32)


def _dot(a, b):
    return jnp.dot(a, b, preferred_element_type=F32)


def _rms(x, g):
    return x * lax.rsqrt(jnp.mean(x * x, axis=-1, keepdims=True) + EPS) * g


def _split3(x):
    hi = x.astype(BF16)
    r = x - hi.astype(F32)
    mid = r.astype(BF16)
    lo = (r - mid.astype(F32)).astype(BF16)
    return hi, mid, lo


def _proj_kernel(*refs, n_base, **static):
    _proj_body(*refs[n_base:], **static)


def _proj_body(x_ref, carry_in_ref, tqc_ref, tqs_ref, tkc_ref, tks_ref,
               g_ref, w_in_ref, bias_ref, qn_ref, w_uq_ref, kvn_ref, w_kexp_ref, w_uv_ref,
               tri_ref, place_ref, ones_q_ref,
               lat_ref, kr_ref, fk_ref, fv_ref, lf_ref,
               qm_ref, km_ref, vm_ref, fq_ref, fka_ref, fvb_ref, carry_out_ref,
               carry_scr, *, valid_rows, transposed_v):
    t = pl.program_id(1)

    @pl.when(t == 0)
    def _():
        carry_scr[...] = carry_in_ref[0]

    x = x_ref[0]
    h = _rms(x, g_ref[...]).astype(BF16)
    proj = _dot(h, w_in_ref[...])

    lane = lax.broadcasted_iota(jnp.int32, (x.shape[0], LANE), 1)
    half = MLA_ROPE // 2

    def rotary(a, first, cos_tab, sin_tab):
        partner = jnp.where(lane < first + half, pltpu.roll(a, LANE - half, 1), pltpu.roll(a, half, 1))
        return a * cos_tab + partner * sin_tab

    def put(ref, val):
        ref[...] = val.reshape(ref.shape)

    cqn = _rms(proj[:, _O_CQ:_O_CKV], qn_ref[...]).astype(BF16)
    q2 = _dot(cqn, w_uq_ref[...])
    tqc, tqs = tqc_ref[...], tqs_ref[...]
    qm_ref[0] = jnp.concatenate(
        [rotary(q2[:, h * HEAD_GROUP:(h + 1) * HEAD_GROUP], MLA_NOPE, tqc, tqs) for h in range(HEADS)],
        axis=1).astype(BF16)

    latent = _rms(proj[:, _O_CKV:_O_FQ], kvn_ref[...])
    put(lat_ref, latent)
    misc = proj[:, _O_MISC:_NPROJ]
    kr = rotary(misc, 0, tkc_ref[...], tks_ref[...])
    put(kr_ref, kr[:, :MLA_ROPE])
    lat_bf = latent.astype(BF16)
    kcat = jnp.concatenate([lat_bf, kr.astype(BF16)], axis=1)
    km_ref[0] = _dot(kcat, w_kexp_ref[...]).astype(BF16)
    v_mla = _dot(lat_bf, w_uv_ref[...])
    vm_ref[0] = (v_mla.T if transposed_v else v_mla).astype(BF16)

    fq = proj[:, _O_FQ:_O_FK]
    fk = proj[:, _O_FK:_O_FV]
    fv = proj[:, _O_FV:_O_MISC]
    fvb_ref[0] = (fv.T if transposed_v else fv).astype(BF16)

    def head_groups(a):
        low = lane < FOX_HEAD_DIM
        groups = []
        for p in range(FOX_HEADS // 2):
            pair = a[:, p * LANE:(p + 1) * LANE]
            groups.append(jnp.where(low, pair, 0.0))
            groups.append(jnp.where(low, pltpu.roll(pair, LANE - FOX_HEAD_DIM, 1), 0.0))
        return groups

    fk_groups = head_groups(fk)

    def put_heads(ref, a, groups):
        if ref.shape[-2:] == (FOX_HEADS, FOX_HEAD_DIM):
            per_head = jnp.swapaxes(jnp.stack(groups, axis=0), 0, 1)
            ref[...] = per_head[:, :, :FOX_HEAD_DIM]
        else:
            put(ref, a)

    put_heads(fk_ref, fk, fk_groups)
    put_heads(fv_ref, fv, head_groups(fv) if fv_ref.shape[-2:] == (FOX_HEADS, FOX_HEAD_DIM) else None)

    z = misc + bias_ref[...]
    keep = (lane >= _LF_LANE) & (lane < _LF_LANE + FOX_HEADS)
    if valid_rows is not None:
        keep = keep & (lax.broadcasted_iota(jnp.int32, z.shape, 0) < valid_rows)
    lf = jnp.where(keep, jnp.minimum(z, 0.0) - jnp.log1p(jnp.exp(-jnp.abs(z))), 0.0)
    put(lf_ref, lf[:, _LF_LANE:_LF_LANE + FOX_HEADS])
    hi, mid, lo = _split3(lf)
    tri = tri_ref[...]
    cum = _dot(tri, hi) + _dot(tri, mid) + _dot(tri, lo) + carry_scr[...]
    n_rows = cum.shape[0]
    carry_scr[...] = cum[n_rows - 1:n_rows, :]
    carry_out_ref[0] = cum[n_rows - 1:n_rows, :]
    c_hi, c_mid, c_lo = _split3(cum * LOG2E)
    kb = _dot(jnp.concatenate([c_hi, c_mid, c_lo], axis=1), place_ref[...])
    fq_wide = jnp.concatenate(head_groups(fq), axis=1)
    fq_ref[0] = (fq_wide * (FOX_SCALE * LOG2E) + ones_q_ref[...]).astype(BF16)
    fka_ref[0] = (jnp.concatenate(fk_groups, axis=1) + kb).astype(BF16)


def _const_spec(shape):
    zeros = (0,) * len(shape)
    return pl.BlockSpec(shape, lambda *_: zeros, pipeline_mode=pl.Buffered(1))


def _proj_call(x, carry_in, tabs, w, tile, valid_rows=None, transposed_v=False, row_offset=0):
    B, L, D = x.shape
    assert L % tile == 0
    nt = L // tile
    tri = jnp.asarray(np.tril(np.ones((tile, tile), np.float32)), BF16)

    row = lambda width: pl.BlockSpec((1, tile, width), lambda b, t: (b, t, 0))
    tab = pl.BlockSpec((tile, LANE), lambda b, t: (t, 0))
    in_specs = [row(D), pl.BlockSpec((1, 1, LANE), lambda b, t: (b, 0, 0)), tab, tab, tab, tab,
                _const_spec((1, D)), _const_spec((D, _NPROJ)), _const_spec((1, LANE)),
                _const_spec((1, _MLA_Q_LORA)), _const_spec((_MLA_Q_LORA, _WIDE)),
                _const_spec((1, _MLA_KV_LORA)), _const_spec((_MLA_KV_LORA + LANE, _WIDE)),
                _const_spec((_MLA_KV_LORA, MLA_HEADS * MLA_V)),
                _const_spec((tile, tile)), _const_spec((3 * LANE, _WIDE)), _const_spec((1, _WIDE))]
    out_shapes = [
        jax.ShapeDtypeStruct((B, L, _MLA_KV_LORA), F32),
        jax.ShapeDtypeStruct((B, L, MLA_ROPE), F32),
        jax.ShapeDtypeStruct((B, L, _FOX_WIDTH), F32),
        jax.ShapeDtypeStruct((B, L, _FOX_WIDTH), F32),
        jax.ShapeDtypeStruct((B, L, FOX_HEADS), F32),
        jax.ShapeDtypeStruct((B, L, _WIDE), BF16),
        jax.ShapeDtypeStruct((B, L, _WIDE), BF16),
        jax.ShapeDtypeStruct((B, L, MLA_HEADS * MLA_V), BF16),
        jax.ShapeDtypeStruct((B, L, _WIDE), BF16),
        jax.ShapeDtypeStruct((B, L, _WIDE), BF16),
        jax.ShapeDtypeStruct((B, L, _FOX_WIDTH), BF16),
        jax.ShapeDtypeStruct((B, 1, LANE), F32),
    ]
    out_specs = [row(s.shape[-1]) for s in out_shapes[:-1]]
    out_specs.append(pl.BlockSpec((1, 1, LANE), lambda b, t: (b, 0, 0)))
    if transposed_v:
        for idx in (7, 10):
            vw = out_shapes[idx].shape[-1]
            out_shapes[idx] = jax.ShapeDtypeStruct((B, vw, L), BF16)
            out_specs[idx] = pl.BlockSpec((1, vw, tile), lambda b, t: (b, 0, t))
    if row_offset:
        assert row_offset % SUBLANE == 0 and L % SUBLANE == 0
        for idx in range(5):
            width = out_shapes[idx].shape[-1]
            out_shapes[idx] = jax.ShapeDtypeStruct((B * (row_offset + L), width), F32)
            out_specs[idx] = pl.BlockSpec(
                (pl.Element(tile), pl.Element(width)),
                lambda b, t: (pl.multiple_of(b * (row_offset + L) + row_offset + t * tile, SUBLANE), 0))
        for idx in (2, 3):
            out_shapes[idx] = jax.ShapeDtypeStruct((B * (row_offset + L), FOX_HEADS, FOX_HEAD_DIM), F32)
            out_specs[idx] = pl.BlockSpec(
                (pl.Element(tile), pl.Element(FOX_HEADS), pl.Element(FOX_HEAD_DIM)),
                lambda b, t: (b * (row_offset + L) + row_offset + t * tile, 0, 0))
    bases = [jnp.zeros(s.shape, s.dtype) for s in out_shapes[:5]] if row_offset else []
    n_base = len(bases)
    return pl.pallas_call(
        functools.partial(_proj_kernel, n_base=n_base, valid_rows=valid_rows, transposed_v=transposed_v),
        out_shape=out_shapes,
        grid=(B, nt),
        in_specs=[pl.BlockSpec(memory_space=pl.ANY)] * n_base + in_specs,
        out_specs=out_specs,
        input_output_aliases={i: i for i in range(n_base)},
        scratch_shapes=[pltpu.VMEM((1, LANE), F32)],
        compiler_params=pltpu.CompilerParams(
            dimension_semantics=("parallel", "arbitrary"), vmem_limit_bytes=V7X_VMEM_LIMIT),
        name="proj",
    )(*bases, x, carry_in, *tabs, w["g_mix"], w["w_in"], w["bias_f"], w["q_norm"], w["w_uq2"],
      w["kv_norm"], w["w_kexp"], w["w_uv"], tri, w["place_k"], w["ones_q"])


def _place_rows_kernel(*refs):
    n = len(refs) // 3
    for src, out in zip(refs[:n], refs[2 * n:]):
        out[0] = src[...]


def _place_rows_call(rows, dests):
    n = len(rows)
    B = dests[0].shape[0]
    in_specs = [pl.BlockSpec(r.shape, lambda b, nd=r.ndim: (0,) * nd) for r in rows]
    in_specs += [pl.BlockSpec(memory_space=pl.ANY) for _ in dests]
    out_specs = [pl.BlockSpec((1,) + r.shape, lambda b, nd=r.ndim: (b,) + (0,) * nd) for r in rows]
    return pl.pallas_call(
        _place_rows_kernel,
        out_shape=[jax.ShapeDtypeStruct(d.shape, d.dtype) for d in dests],
        grid=(B,),
        in_specs=in_specs,
        out_specs=out_specs,
        input_output_aliases={n + i: i for i in range(n)},
        compiler_params=pltpu.CompilerParams(dimension_semantics=("arbitrary",)),
        name="place_meta_rows",
    )(*rows, *dests)


def _softmax_update(h, s, v_pair, m_scr, l_scr, acc_scr):
    m_prev = m_scr[h]
    m_new = jnp.maximum(m_prev, jnp.max(s, axis=1, keepdims=True))
    alpha = jnp.exp2(m_prev - m_new)
    p = jnp.exp2(s - jnp.tile(m_new, (1, s.shape[1] // LANE)))
    l_scr[h] = alpha * l_scr[h] + jnp.sum(p, axis=1, keepdims=True)
    acc_scr[h] = alpha * acc_scr[h] + _dot(p.astype(BF16), v_pair)
    m_scr[h] = m_new


QK_LOOKAHEAD = 8
UNIT_Q = 256
UNIT_K = 256
ONES_ROWS = 16


def _with_ones_rows(v_t):
    row = lax.broadcasted_iota(jnp.int32, (ONES_ROWS, v_t.shape[1]), 0)
    return jnp.concatenate([v_t, jnp.where(row == 0, 1.0, 0.0).astype(v_t.dtype)], axis=0)


def _softmax_update_t(h, qs, s_t, v_t, m_scr, acc_scr):
    uk, uq = s_t.shape
    s3 = s_t.reshape(uk // SUBLANE, SUBLANE, uq)
    m_cur = jnp.max(s3, axis=0)
    for shift in (4, 2, 1):
        m_cur = jnp.maximum(m_cur, pltpu.roll(m_cur, shift, 0))
    m_prev = m_scr[h, :, qs]
    m_new = jnp.maximum(m_prev, m_cur)
    alpha = jnp.exp2(m_prev - m_new)
    p_t = jnp.exp2(s3 - m_new[None]).reshape(uk, uq)
    pv = _dot(v_t, p_t.astype(BF16))
    rows = pv.shape[0]
    acc = acc_scr[h, :, qs].reshape(rows // SUBLANE, SUBLANE, uq)
    acc_scr[h, :, qs] = (alpha[None] * acc).reshape(rows, uq) + pv
    m_scr[h, :, qs] = m_new


def _attn_kernel(qi_ref, kj_ref, q_ref, k_ref, vt_ref, kmeta_ref, vtmeta_ref, o_ref,
                 m_scr, acc_scr, *, chunk_mask, n_meta):
    p_id = pl.program_id(1)
    i = qi_ref[p_id]
    j = kj_ref[p_id]
    tq = q_ref.shape[1]
    tk = k_ref.shape[1]
    dv = acc_scr.shape[1] - ONES_ROWS

    def head(ref2d, h):
        return ref2d[:, h * HEAD_GROUP:(h + 1) * HEAD_GROUP]

    @pl.when(j == 0)
    def _():
        n_pad = kmeta_ref.shape[0]
        n_rows = -(-n_meta // BF16_SUBLANES) * BF16_SUBLANES
        row = lax.broadcasted_iota(jnp.int32, (n_rows, tq), 0)
        scores = [_nt_dot(kmeta_ref[:n_rows, h * HEAD_GROUP:(h + 1) * HEAD_GROUP], head(q_ref.at[0], h))
                  for h in range(HEADS)]
        for h in range(HEADS):
            s3 = jnp.where(row < n_meta, scores[h], NEG_INF).reshape(n_rows // SUBLANE, SUBLANE, tq)
            m_new = jnp.max(s3, axis=0)
            for shift in (4, 2, 1):
                m_new = jnp.maximum(m_new, pltpu.roll(m_new, shift, 0))
            p_t = jnp.exp2(s3 - m_new[None]).reshape(n_rows, tq).astype(BF16)
            p_t = jnp.concatenate([p_t, jnp.zeros((n_pad - n_rows, tq), BF16)], axis=0)
            acc_scr[h] = _dot(_with_ones_rows(vtmeta_ref[h * dv:(h + 1) * dv, :]), p_t)
            m_scr[h] = m_new

    uq = min(UNIT_Q, tq)
    uk = min(UNIT_K, tk)

    def block(key_shift):
        masked = key_shift is not None
        units = [(h, q0, k0) for h in range(HEADS) for k0 in range(0, tk, uk) for q0 in range(0, tq, uq)]
        if masked:
            units = [u for u in units if key_shift + u[2] < u[1] + uq]
            kk = lax.broadcasted_iota(jnp.int32, (uk, uq), 0) + key_shift
            qq = lax.broadcasted_iota(jnp.int32, (uk, uq), 1)

        def qk(u):
            h, q0, k0 = u
            return _nt_dot(k_ref[0, k0:k0 + uk, h * HEAD_GROUP:(h + 1) * HEAD_GROUP],
                           q_ref[0, q0:q0 + uq, h * HEAD_GROUP:(h + 1) * HEAD_GROUP])

        scores = {}
        for n in range(-QK_LOOKAHEAD, len(units)):
            if n + QK_LOOKAHEAD < len(units):
                scores[n + QK_LOOKAHEAD] = qk(units[n + QK_LOOKAHEAD])
            if n >= 0:
                h, q0, k0 = units[n]
                s_t = scores.pop(n)
                if masked and key_shift + k0 + uk > q0 + 1:
                    if chunk_mask:
                        valid = (kk + k0) // CHUNK <= (qq + q0) // CHUNK
                    else:
                        valid = kk + k0 <= qq + q0
                    s_t = jnp.where(valid, s_t, NEG_INF)
                v_t = _with_ones_rows(vt_ref[0, h * dv:(h + 1) * dv, k0:k0 + uk])
                _softmax_update_t(h, slice(q0, q0 + uq), s_t, v_t, m_scr, acc_scr)

    ratio = tq // tk
    d = j - ratio * i

    @pl.when(d < 0)
    def _():
        block(None)

    for dd in range(ratio):
        @pl.when(d == dd)
        def _(dd=dd):
            block(dd * tk)

    @pl.when(d == ratio - 1)
    def _():
        def normalised(h):
            return acc_scr[h, :dv, :] / acc_scr[h, dv:dv + 1, :]

        for hp in range(HEADS // 2):
            o_pair_t = jnp.concatenate([normalised(2 * hp), normalised(2 * hp + 1)], axis=0)
            o_ref[0, :, hp * 2 * dv:(hp + 1) * 2 * dv] = o_pair_t.T.astype(o_ref.dtype)


def _attn_call(q, k, v_t, k_meta, v_t_meta, *, chunk_mask, n_meta, tq, tk, name):
    B, F, _ = q.shape
    tq, tk = min(tq, F), min(tk, F)
    assert F % tq == 0 and tq % tk == 0 and tk % CHUNK == 0
    ratio = tq // tk
    pairs = [(i, j) for i in range(F // tq) for j in range(ratio * (i + 1))]
    qi = jnp.asarray([p[0] for p in pairs], jnp.int32)
    kj = jnp.asarray([p[1] for p in pairs], jnp.int32)
    vw = v_t.shape[1]
    grid_spec = pltpu.PrefetchScalarGridSpec(
        num_scalar_prefetch=2,
        grid=(B, len(pairs)),
        in_specs=[
            pl.BlockSpec((1, tq, _WIDE), lambda b, p, qi, kj: (b, qi[p], 0)),
            pl.BlockSpec((1, tk, _WIDE), lambda b, p, qi, kj: (b, kj[p], 0)),
            pl.BlockSpec((1, vw, tk), lambda b, p, qi, kj: (b, 0, kj[p])),
            pl.BlockSpec((LANE, _WIDE), lambda b, p, qi, kj: (0, 0)),
            pl.BlockSpec((vw, LANE), lambda b, p, qi, kj: (0, 0)),
        ],
        out_specs=pl.BlockSpec((1, tq, vw), lambda b, p, qi, kj: (b, qi[p], 0)),
        scratch_shapes=[pltpu.VMEM((HEADS, SUBLANE, tq), F32),
                        pltpu.VMEM((HEADS, vw // HEADS + ONES_ROWS, tq), F32)],
    )
    return pl.pallas_call(
        functools.partial(_attn_kernel, chunk_mask=chunk_mask, n_meta=n_meta),
        out_shape=jax.ShapeDtypeStruct((B, F, vw), BF16),
        grid_spec=grid_spec,
        compiler_params=pltpu.CompilerParams(
            dimension_semantics=("parallel", "arbitrary"), vmem_limit_bytes=V7X_VMEM_LIMIT),
        name=name,
    )(qi, kj, q, k, v_t, k_meta, v_t_meta)


def _ffn_kernel(x_ref, oa_ref, ob_ref, woa_ref, wob_ref, g_ref, wg_ref, wu_ref, wd_ref, gf_ref, y_ref):
    x1 = x_ref[...] + _dot(oa_ref[...], woa_ref[...]) + _dot(ob_ref[...], wob_ref[...])
    h = _rms(x1, g_ref[...]).astype(BF16)
    gate = _dot(h, wg_ref[...])
    up = _dot(h, wu_ref[...])
    act = (gate * jax.nn.sigmoid(gate) * up).astype(BF16)
    x2 = x1 + _dot(act, wd_ref[...])
    y_ref[...] = _rms(x2, gf_ref[...])


def _ffn_call(x, oa, ob, w, tile):
    R, D = x.shape
    tile = min(tile, R)
    assert R % tile == 0
    dm = oa.shape[-1]
    dff = w["w_gate"].shape[-1]
    row = lambda width: pl.BlockSpec((tile, width), lambda r: (r, 0))
    return pl.pallas_call(
        _ffn_kernel,
        out_shape=jax.ShapeDtypeStruct((R, D), F32),
        grid=(R // tile,),
        in_specs=[row(D), row(dm), row(dm),
                  _const_spec((dm, D)), _const_spec((dm, D)), _const_spec((1, D)),
                  _const_spec((D, dff)), _const_spec((D, dff)), _const_spec((dff, D)),
                  _const_spec((1, D))],
        out_specs=row(D),
        compiler_params=pltpu.CompilerParams(
            dimension_semantics=("parallel",), vmem_limit_bytes=V7X_VMEM_LIMIT),
        name="ffn",
    )(x, oa, ob, w["w_out_a"], w["w_out_b"], w["g_ffn"], w["w_gate"], w["w_up"], w["w_down"],
      w["g_final"])


_CUM_CHUNK = 256


def _cumsum_kernel(x_ref, u_ref, o_ref):
    rows, n = x_ref.shape
    u = u_ref[...]
    carry = jnp.zeros((rows, 1), F32)
    for c in range(n // _CUM_CHUNK):
        hi, mid, lo = _split3(x_ref[:, c * _CUM_CHUNK:(c + 1) * _CUM_CHUNK])
        y = _dot(hi, u) + _dot(mid, u) + _dot(lo, u) + carry
        o_ref[:, c * _CUM_CHUNK:(c + 1) * _CUM_CHUNK] = y
        carry = y[:, _CUM_CHUNK - 1:_CUM_CHUNK]


def _cumsum_call(x):
    rows, n = x.shape
    assert n % _CUM_CHUNK == 0
    u = jnp.asarray(np.triu(np.ones((_CUM_CHUNK, _CUM_CHUNK), np.float32)), BF16)
    return pl.pallas_call(
        _cumsum_kernel,
        out_shape=jax.ShapeDtypeStruct((rows, n), F32),
        compiler_params=pltpu.CompilerParams(vmem_limit_bytes=V7X_VMEM_LIMIT),
        name="cache_cumsum",
    )(x, u)


def _sample_fox_kernel(fq_ref, k_ref, v_ref, ck_ref, fkn_ref, fvn_ref, o_ref,
                       qm_scr, m_scr, l_scr, acc_scr):
    j = pl.program_id(1)
    nj = pl.num_programs(1)
    s_new = fq_ref.shape[1]
    lane = lax.broadcasted_iota(jnp.int32, (s_new, LANE), 1)

    @pl.when(j == 0)
    def _():
        m_scr[...] = jnp.full(m_scr.shape, NEG_INF, F32)
        l_scr[...] = jnp.zeros(l_scr.shape, F32)
        acc_scr[...] = jnp.zeros(acc_scr.shape, F32)
        for h in range(HEADS):
            q_h = jnp.where(lane < FOX_HEAD_DIM, fq_ref[0, :, h * HEAD_GROUP:(h + 1) * HEAD_GROUP], 0)
            if h % 2 == 1:
                q_h = pltpu.roll(q_h.astype(F32), FOX_HEAD_DIM, 1).astype(BF16)
            qm_scr[h] = q_h

    ck2 = ck_ref[0] * LOG2E
    for hp in range(HEADS // 2):
        k_pair = k_ref[0, :, hp * LANE:(hp + 1) * LANE].astype(BF16)
        v_pair = v_ref[0, :, hp * LANE:(hp + 1) * LANE].astype(BF16)
        for h in (2 * hp, 2 * hp + 1):
            s = _nt_dot(qm_scr[h], k_pair) - ck2[h:h + 1, :]
            _softmax_update(h, s, v_pair, m_scr, l_scr, acc_scr)

    @pl.when(j == nj - 1)
    def _():
        r = lax.broadcasted_iota(jnp.int32, (s_new, s_new), 0)
        c = lax.broadcasted_iota(jnp.int32, (s_new, s_new), 1)
        for h in range(HEADS):
            s = _nt_dot(fq_ref[0, :, h * HEAD_GROUP:(h + 1) * HEAD_GROUP],
                        fkn_ref[0, :, h * HEAD_GROUP:(h + 1) * HEAD_GROUP])
            s = jnp.where(c <= r, s, NEG_INF)
            m_prev = m_scr[h]
            m_new = jnp.maximum(m_prev, jnp.max(s, axis=1, keepdims=True))
            alpha = jnp.exp2(m_prev - m_new)
            p = jnp.exp2(s - m_new[:, :s_new])
            l_scr[h] = alpha * l_scr[h] + jnp.sum(p, axis=1, keepdims=True)
            pair = (h // 2) * LANE
            acc_scr[h] = alpha * acc_scr[h] + _dot(p.astype(BF16), fvn_ref[0, :, pair:pair + LANE])
        for hp in range(HEADS // 2):
            even = acc_scr[2 * hp] / l_scr[2 * hp]
            odd = acc_scr[2 * hp + 1] / l_scr[2 * hp + 1]
            o_ref[0, :, hp * LANE:(hp + 1) * LANE] = jnp.where(lane < LANE // 2, even, odd).astype(o_ref.dtype)


def _sample_fox_call(fq, cache_k, cache_v, cum_cache, fka_new, fv_new, tile):
    B, S, _ = fq.shape
    past = cache_k.shape[1]
    assert past % tile == 0
    w = cache_k.shape[-1]
    return pl.pallas_call(
        _sample_fox_kernel,
        out_shape=jax.ShapeDtypeStruct((B, S, w), BF16),
        grid=(B, past // tile),
        in_specs=[pl.BlockSpec((1, S, _WIDE), lambda b, j: (b, 0, 0)),
                  pl.BlockSpec((1, tile, w), lambda b, j: (b, j, 0)),
                  pl.BlockSpec((1, tile, w), lambda b, j: (b, j, 0)),
                  pl.BlockSpec((1, HEADS, tile), lambda b, j: (b, 0, j)),
                  pl.BlockSpec((1, S, _WIDE), lambda b, j: (b, 0, 0)),
                  pl.BlockSpec((1, S, w), lambda b, j: (b, 0, 0))],
        out_specs=pl.BlockSpec((1, S, w), lambda b, j: (b, 0, 0)),
        scratch_shapes=[pltpu.VMEM((HEADS, S, LANE), BF16),
                        pltpu.VMEM((HEADS, S, LANE), F32),
                        pltpu.VMEM((HEADS, S, LANE), F32),
                        pltpu.VMEM((HEADS, S, LANE), F32)],
        compiler_params=pltpu.CompilerParams(
            dimension_semantics=("parallel", "arbitrary"), vmem_limit_bytes=V7X_VMEM_LIMIT),
        name="sample_fox",
    )(fq, cache_k, cache_v, cum_cache, fka_new, fv_new)


def _sample_mla_kernel(q_ref, lat_ref, kr_ref, latn_ref, krn_ref, wabs_ref, prope_ref, wv_ref, o_ref,
                       ql_scr, qr_scr, m_scr, l_scr, acc_scr):
    j = pl.program_id(1)
    nj = pl.num_programs(1)
    s_new = q_ref.shape[1]

    @pl.when(j == 0)
    def _():
        m_scr[...] = jnp.full(m_scr.shape, NEG_INF, F32)
        l_scr[...] = jnp.zeros(l_scr.shape, F32)
        acc_scr[...] = jnp.zeros(acc_scr.shape, F32)
        for h in range(HEADS):
            q_h = q_ref[0, :, h * HEAD_GROUP:(h + 1) * HEAD_GROUP]
            ql_scr[h * s_new:(h + 1) * s_new, :] = _dot(q_h, wabs_ref[h]).astype(BF16)
            qr_scr[h * s_new:(h + 1) * s_new, :] = _dot(q_h, prope_ref[...]).astype(BF16)

    def update(lat, kr):
        s = _nt_dot(ql_scr[...], lat) + _nt_dot(qr_scr[...], kr)
        m_prev = m_scr[...]
        m_new = jnp.maximum(m_prev, jnp.max(s, axis=1, keepdims=True))
        alpha = jnp.exp2(m_prev - m_new)
        p = jnp.exp2(s - m_new[:, :1])
        l_scr[...] = alpha * l_scr[...] + jnp.sum(p, axis=1, keepdims=True)
        acc_scr[...] = jnp.tile(alpha, (1, acc_scr.shape[1] // LANE)) * acc_scr[...] + _dot(p.astype(BF16), lat)
        m_scr[...] = m_new

    def pad_rope(kr):
        return jnp.concatenate([kr, jnp.zeros((kr.shape[0], LANE - MLA_ROPE), kr.dtype)], axis=1)

    update(lat_ref[0].astype(BF16), pad_rope(kr_ref[0]).astype(BF16))

    @pl.when(j == nj - 1)
    def _():
        update(latn_ref[0].astype(BF16), pad_rope(krn_ref[0]).astype(BF16))
        o_lat = (acc_scr[...] / jnp.tile(l_scr[...], (1, acc_scr.shape[1] // LANE))).astype(BF16)
        out = _dot(o_lat[0:s_new], wv_ref[0])
        for h in range(1, HEADS):
            out = out + _dot(o_lat[h * s_new:(h + 1) * s_new], wv_ref[h])
        o_ref[0] = out.astype(o_ref.dtype)


def _sample_mla_call(q, cache_lat, cache_kr, lat_new, kr_new, w, tile):
    B, S, _ = q.shape
    past = cache_lat.shape[1]
    assert past % tile == 0
    c = cache_lat.shape[-1]
    ow = MLA_HEADS * MLA_V
    return pl.pallas_call(
        _sample_mla_kernel,
        out_shape=jax.ShapeDtypeStruct((B, S, ow), BF16),
        grid=(B, past // tile),
        in_specs=[pl.BlockSpec((1, S, _WIDE), lambda b, j: (b, 0, 0)),
                  pl.BlockSpec((1, tile, c), lambda b, j: (b, j, 0)),
                  pl.BlockSpec((1, tile, MLA_ROPE), lambda b, j: (b, j, 0)),
                  pl.BlockSpec((1, S, c), lambda b, j: (b, 0, 0)),
                  pl.BlockSpec((1, S, MLA_ROPE), lambda b, j: (b, 0, 0)),
                  _const_spec((HEADS, HEAD_GROUP, c)), _const_spec((HEAD_GROUP, LANE)),
                  _const_spec((HEADS, c, ow))],
        out_specs=pl.BlockSpec((1, S, ow), lambda b, j: (b, 0, 0)),
        scratch_shapes=[pltpu.VMEM((HEADS * S, c), BF16),
                        pltpu.VMEM((HEADS * S, LANE), BF16),
                        pltpu.VMEM((HEADS * S, LANE), F32),
                        pltpu.VMEM((HEADS * S, LANE), F32),
                        pltpu.VMEM((HEADS * S, c), F32)],
        compiler_params=pltpu.CompilerParams(
            dimension_semantics=("parallel", "arbitrary"), vmem_limit_bytes=V7X_VMEM_LIMIT),
        name="sample_mla",
    )(q, cache_lat, cache_kr, lat_new, kr_new, w["w_abs"], w["p_rope"], w["w_v_wide"])


def _gather_cols(w, src):
    src = np.asarray(src)
    pieces, start = [], 0
    for i in range(1, len(src) + 1):
        run_ends = i == len(src) or (src[i] != src[i - 1] + 1 if src[i - 1] >= 0 else src[i] >= 0)
        if run_ends:
            if src[start] >= 0:
                pieces.append(w[:, int(src[start]):int(src[start]) + i - start])
            else:
                pieces.append(jnp.zeros((w.shape[0], i - start), w.dtype))
            start = i
    return jnp.concatenate(pieces, axis=1)


def _prep_weights(norm_mix, w_in, b_forget, mla_q_norm, w_mla_uq, mla_kv_norm, w_mla_ukv, w_out,
                  norm_ffn, w_ffn_gate, w_ffn_up, w_ffn_down, norm_final):
    d_model = w_in.shape[0]
    o_kr = _MLA_Q_LORA + _MLA_KV_LORA
    o_fq = o_kr + MLA_ROPE
    o_fk = o_fq + _FOX_WIDTH
    o_fv = o_fk + _FOX_WIDTH
    o_fl = o_fv + _FOX_WIDTH
    half = MLA_ROPE // 2

    src = -np.ones((_NPROJ,), np.int64)
    src[_O_CQ:_O_CKV] = np.arange(0, _MLA_Q_LORA)
    src[_O_CKV:_O_FQ] = np.arange(_MLA_Q_LORA, o_kr)
    src[_O_FQ:_O_FK] = np.arange(o_fq, o_fk)
    src[_O_FK:_O_FV] = np.arange(o_fk, o_fv)
    src[_O_FV:_O_MISC] = np.arange(o_fv, o_fl)
    src[_O_MISC:_O_MISC + MLA_ROPE] = np.arange(o_kr, o_fq)
    src[_O_MISC + _LF_LANE:_O_MISC + _LF_LANE + FOX_HEADS] = np.arange(o_fl, o_fl + FOX_HEADS)
    w_in_r = _gather_cols(w_in, src).astype(BF16)

    src_a = -np.ones((_WIDE,), np.int64)
    for h in range(MLA_HEADS):
        src_a[h * HEAD_GROUP + np.arange(MLA_QK_DIM)] = h * MLA_QK_DIM + np.arange(MLA_QK_DIM)
    w_uq2 = _gather_cols(w_mla_uq, src_a).astype(BF16)

    src_k = -np.ones((_WIDE,), np.int64)
    src_v = np.zeros((MLA_HEADS * MLA_V,), np.int64)
    for h in range(MLA_HEADS):
        src_k[h * HEAD_GROUP + np.arange(MLA_NOPE)] = h * (MLA_NOPE + MLA_V) + np.arange(MLA_NOPE)
        src_v[h * MLA_V + np.arange(MLA_V)] = h * (MLA_NOPE + MLA_V) + MLA_NOPE + np.arange(MLA_V)
    place_r = np.zeros((LANE, _WIDE), np.float32)
    for h in range(MLA_HEADS):
        place_r[np.arange(MLA_ROPE), h * HEAD_GROUP + MLA_NOPE + np.arange(MLA_ROPE)] = 1.0
    w_uk_wide = _gather_cols(w_mla_ukv, src_k)
    w_kexp = jnp.concatenate([w_uk_wide, jnp.asarray(place_r)], axis=0).astype(BF16)
    w_uv = _gather_cols(w_mla_ukv, src_v).astype(BF16)

    w_uk = w_mla_ukv.reshape(_MLA_KV_LORA, MLA_HEADS, MLA_NOPE + MLA_V)[:, :, :MLA_NOPE]
    w_abs = jnp.zeros((MLA_HEADS, HEAD_GROUP, _MLA_KV_LORA), F32)
    w_abs = w_abs.at[:, :MLA_NOPE, :].set(jnp.transpose(w_uk, (1, 2, 0))).astype(BF16)
    p_rope = np.zeros((HEAD_GROUP, LANE), np.float32)
    p_rope[MLA_NOPE + np.arange(MLA_ROPE), np.arange(MLA_ROPE)] = 1.0
    w_uv_h = w_mla_ukv.reshape(_MLA_KV_LORA, MLA_HEADS, MLA_NOPE + MLA_V)[:, :, MLA_NOPE:]
    eye = jnp.asarray(np.eye(MLA_HEADS, dtype=np.float32))
    w_v_wide = (jnp.transpose(w_uv_h, (1, 0, 2))[:, :, None, :] * eye[:, None, :, None])
    w_v_wide = w_v_wide.reshape(MLA_HEADS, _MLA_KV_LORA, MLA_HEADS * MLA_V).astype(BF16)

    bias_f = jnp.zeros((1, LANE), F32).at[0, _LF_LANE:_LF_LANE + FOX_HEADS].set(b_forget.astype(F32))
    place_k = np.zeros((3 * LANE, _WIDE), np.float32)
    ones_q = np.zeros((1, _WIDE), np.float32)
    for h in range(FOX_HEADS):
        for part in range(3):
            place_k[part * LANE + _LF_LANE + h, h * HEAD_GROUP + FOX_HEAD_DIM + part] = -1.0
            ones_q[0, h * HEAD_GROUP + FOX_HEAD_DIM + part] = 1.0

    d_mla = MLA_HEADS * MLA_V
    return {
        "g_mix": norm_mix.reshape(1, d_model).astype(F32),
        "w_in": w_in_r,
        "bias_f": bias_f,
        "q_norm": mla_q_norm.reshape(1, -1).astype(F32),
        "w_uq2": w_uq2,
        "kv_norm": mla_kv_norm.reshape(1, -1).astype(F32),
        "w_kexp": w_kexp,
        "w_uv": w_uv,
        "place_k": jnp.asarray(place_k, BF16),
        "ones_q": jnp.asarray(ones_q),
        "w_abs": w_abs,
        "p_rope": jnp.asarray(p_rope, BF16),
        "w_v_wide": w_v_wide,
        "w_out_a": w_out[:d_mla].astype(BF16),
        "w_out_b": w_out[d_mla:].astype(BF16),
        "g_ffn": norm_ffn.reshape(1, d_model).astype(F32),
        "w_gate": w_ffn_gate.astype(BF16),
        "w_up": w_ffn_up.astype(BF16),
        "w_down": w_ffn_down.astype(BF16),
        "g_final": norm_final.reshape(1, d_model).astype(F32),
    }


def _rope_tables(pos):
    half = MLA_ROPE // 2
    inv_freq = ROPE_THETA ** (-jnp.arange(half, dtype=F32) / half)
    ang = pos.astype(F32)[:, None] * inv_freq[None, :]
    cos, sin = jnp.cos(ang), jnp.sin(ang)
    n = pos.shape[0]
    sc = MLA_SCALE * LOG2E
    zeros = lambda w: jnp.zeros((n, w), F32)
    tqc = jnp.concatenate([jnp.full((n, MLA_NOPE), sc, F32), sc * cos, sc * cos,
                           zeros(LANE - MLA_QK_DIM)], axis=1)
    tqs = jnp.concatenate([zeros(MLA_NOPE), -sc * sin, sc * sin, zeros(LANE - MLA_QK_DIM)], axis=1)
    tkc = jnp.concatenate([cos, cos, zeros(LANE - MLA_ROPE)], axis=1)
    tks = jnp.concatenate([-sin, sin, zeros(LANE - MLA_ROPE)], axis=1)
    return tqc, tqs, tkc, tks


def _pad_rows(a, rows):
    return jnp.concatenate([a, jnp.zeros((rows - a.shape[0],) + a.shape[1:], a.dtype)], axis=0)


PROJ_TILE = 256
ATTN_TQ = 1024
ATTN_TK = 512
FFN_TILE = 256
SAMPLE_TILE = 1024


def kernel(x_prompt, x_sample, cache_mla_latent, cache_mla_krope, cache_fox_k, cache_fox_v, cache_fox_logf, meta_tokens, norm_mix, w_in, b_forget, mla_q_norm, w_mla_uq, mla_kv_norm, w_mla_ukv, w_out, norm_ffn, w_ffn_gate, w_ffn_up, w_ffn_down, norm_final):
    depth = w_in.shape[0]
    assert depth == 1, "single-layer trunk: the meta rows' mixing output never reaches an output"
    B, seq, d_model = x_prompt.shape
    SB, s_new, _ = x_sample.shape
    n_meta = meta_tokens.shape[0]
    past = cache_mla_latent.shape[2]
    assert n_meta <= LANE

    w = _prep_weights(norm_mix[0], w_in[0], b_forget[0], mla_q_norm[0], w_mla_uq[0], mla_kv_norm[0],
                      w_mla_ukv[0], w_out[0], norm_ffn[0], w_ffn_gate[0], w_ffn_up[0], w_ffn_down[0],
                      norm_final)

    zero_carry = jnp.zeros((1, 1, LANE), F32)
    meta = _proj_call(_pad_rows(meta_tokens.astype(F32), LANE)[None], zero_carry,
                      _rope_tables(jnp.arange(LANE)), w, LANE, valid_rows=n_meta, transposed_v=True)
    (m_lat, m_kr, m_fk, m_fv, m_lf, _, m_km, m_vm, _, m_fka, m_fvb, m_carry) = meta
    frames = _proj_call(x_prompt, jnp.broadcast_to(m_carry, (B, 1, LANE)),
                        _rope_tables(n_meta + jnp.arange(seq)), w, PROJ_TILE, transposed_v=True,
                        row_offset=n_meta)
    f_qm, f_km, f_vm, f_fq, f_fka, f_fvb = frames[5:11]
    per_head = lambda a: a.reshape(a.shape[:-1] + (FOX_HEADS, FOX_HEAD_DIM))
    lat_p, kr_p, fk_p, fv_p, lf_p = _place_rows_call(
        [m_lat[0, :n_meta], m_kr[0, :n_meta], per_head(m_fk[0, :n_meta]), per_head(m_fv[0, :n_meta]),
         m_lf[0, :n_meta]],
        [a.reshape((B, n_meta + seq) + a.shape[1:]) for a in frames[:5]])

    o_mla = _attn_call(f_qm, f_km, f_vm, m_km[0], m_vm[0],
                       chunk_mask=True, n_meta=n_meta, tq=ATTN_TQ, tk=ATTN_TK, name="attn_mla")
    o_fox = _attn_call(f_fq, f_fka, f_fvb, m_fka[0], m_fvb[0],
                       chunk_mask=False, n_meta=n_meta, tq=ATTN_TQ, tk=ATTN_TK, name="attn_fox")
    y_prompt = _ffn_call(x_prompt.reshape(B * seq, d_model), o_mla.reshape(B * seq, -1),
                         o_fox.reshape(B * seq, -1), w, FFN_TILE).reshape(B, seq, d_model)

    lat_p, kr_p, lf_p = lat_p[None], kr_p[None], lf_p[None]
    fk_p, fv_p = fk_p[None], fv_p[None]

    logf_t = jnp.transpose(cache_fox_logf[0].astype(F32), (0, 2, 1))
    cum_cache = _cumsum_call(logf_t.reshape(SB * FOX_HEADS, past)).reshape(SB, FOX_HEADS, past)
    carry_s = jnp.zeros((SB, 1, LANE), F32).at[:, 0, _LF_LANE:_LF_LANE + FOX_HEADS].set(
        cum_cache[:, :, past - 1])
    samp = _proj_call(x_sample, carry_s, _rope_tables(past + jnp.arange(s_new)), w, s_new)
    (s_lat, s_kr, s_fk, s_fv, s_lf, s_qm, _, _, s_fq, s_fka, s_fvb, _) = samp

    so_mla = _sample_mla_call(s_qm, cache_mla_latent[0], cache_mla_krope[0], s_lat, s_kr, w, SAMPLE_TILE)
    so_fox = _sample_fox_call(s_fq, cache_fox_k[0].reshape(SB, past, _FOX_WIDTH),
                              cache_fox_v[0].reshape(SB, past, _FOX_WIDTH), cum_cache, s_fka, s_fvb,
                              SAMPLE_TILE)
    y_sample = _ffn_call(x_sample.reshape(SB * s_new, d_model), so_mla.reshape(SB * s_new, -1),
                         so_fox.reshape(SB * s_new, -1), w, FFN_TILE).reshape(SB, s_new, d_model)

    heads = lambda a: a.reshape(1, SB, s_new, FOX_HEADS, FOX_HEAD_DIM)
    return (y_prompt, y_sample, lat_p, kr_p, fk_p, fv_p, lf_p,
            s_lat[None], s_kr[None], heads(s_fk), heads(s_fv), s_lf[None])
```

```python
import functools

import numpy as np
import jax
import jax.numpy as jnp
from jax import lax
from jax.experimental import pallas as pl
from jax.experimental.pallas import tpu as pltpu

CHUNK = 64
EPS = 1e-6
NEG_INF = -1e30
MLA_HEADS = 8
MLA_NOPE = 64
MLA_ROPE = 32
MLA_V = 64
MLA_QK_DIM = MLA_NOPE + MLA_ROPE
MLA_SCALE = MLA_QK_DIM ** -0.5
ROPE_THETA = 10000.0
FOX_HEADS = 8
FOX_HEAD_DIM = 64
FOX_SCALE = FOX_HEAD_DIM ** -0.5
LOG2E = 1.4426950408889634

LANE = 128
SUBLANE = 8
BF16_SUBLANES = 16
HEAD_GROUP = LANE
HEADS = 8
V7X_VMEM_LIMIT = 56 * 1024 * 1024

BF16 = jnp.bfloat16
F32 = jnp.float32

_MLA_Q_LORA = 384
_MLA_KV_LORA = 256
_FOX_WIDTH = FOX_HEADS * FOX_HEAD_DIM
_WIDE = HEADS * HEAD_GROUP
_O_CQ = 0
_O_CKV = _O_CQ + _MLA_Q_LORA
_O_FQ = _O_CKV + _MLA_KV_LORA
_O_FK = _O_FQ + _FOX_WIDTH
_O_FV = _O_FK + _FOX_WIDTH
_O_MISC = _O_FV + _FOX_WIDTH
_NPROJ = _O_MISC + LANE
_LF_LANE = MLA_ROPE


def _nt_dot(a, b):
    return lax.dot_general(a, b, (((1,), (1,)), ((), ())), preferred_element_type=F32)


def _dot(a, b):
    return jnp.dot(a, b, preferred_element_type=F32)


def _rms(x, g):
    return x * lax.rsqrt(jnp.mean(x * x, axis=-1, keepdims=True) + EPS) * g


def _split3(x):
    hi = x.astype(BF16)
    r = x - hi.astype(F32)
    mid = r.astype(BF16)
    lo = (r - mid.astype(F32)).astype(BF16)
    return hi, mid, lo


def _proj_kernel(*refs, n_base, **static):
    _proj_body(*refs[n_base:], **static)


def _proj_body(x_ref, carry_in_ref, tqc_ref, tqs_ref, tkc_ref, tks_ref,
               g_ref, w_in_ref, bias_ref, qn_ref, w_uq_ref, kvn_ref, w_kexp_ref, w_uv_ref,
               tri_ref, place_ref, ones_q_ref,
               lat_ref, kr_ref, fk_ref, fv_ref, lf_ref,
               qm_ref, km_ref, vm_ref, fq_ref, fka_ref, fvb_ref, carry_out_ref,
               carry_scr, *, valid_rows, transposed_v):
    t = pl.program_id(1)

    @pl.when(t == 0)
    def _():
        carry_scr[...] = carry_in_ref[0]

    x = x_ref[0]
    h = _rms(x, g_ref[...]).astype(BF16)
    proj = _dot(h, w_in_ref[...])

    lane = lax.broadcasted_iota(jnp.int32, (x.shape[0], LANE), 1)
    half = MLA_ROPE // 2

    def rotary(a, first, cos_tab, sin_tab):
        partner = jnp.where(lane < first + half, pltpu.roll(a, LANE - half, 1), pltpu.roll(a, half, 1))
        return a * cos_tab + partner * sin_tab

    def put(ref, val):
        ref[...] = val.reshape(ref.shape)

    cqn = _rms(proj[:, _O_CQ:_O_CKV], qn_ref[...]).astype(BF16)
    q2 = _dot(cqn, w_uq_ref[...])
    tqc, tqs = tqc_ref[...], tqs_ref[...]
    qm_ref[0] = jnp.concatenate(
        [rotary(q2[:, h * HEAD_GROUP:(h + 1) * HEAD_GROUP], MLA_NOPE, tqc, tqs) for h in range(HEADS)],
        axis=1).astype(BF16)

    latent = _rms(proj[:, _O_CKV:_O_FQ], kvn_ref[...])
    put(lat_ref, latent)
    misc = proj[:, _O_MISC:_NPROJ]
    kr = rotary(misc, 0, tkc_ref[...], tks_ref[...])
    put(kr_ref, kr[:, :MLA_ROPE])
    lat_bf = latent.astype(BF16)
    kcat = jnp.concatenate([lat_bf, kr.astype(BF16)], axis=1)
    km_ref[0] = _dot(kcat, w_kexp_ref[...]).astype(BF16)
    v_mla = _dot(lat_bf, w_uv_ref[...])
    vm_ref[0] = (v_mla.T if transposed_v else v_mla).astype(BF16)

    fq = proj[:, _O_FQ:_O_FK]
    fk = proj[:, _O_FK:_O_FV]
    fv = proj[:, _O_FV:_O_MISC]
    fvb_ref[0] = (fv.T if transposed_v else fv).astype(BF16)

    def head_groups(a):
        low = lane < FOX_HEAD_DIM
        groups = []
        for p in range(FOX_HEADS // 2):
            pair = a[:, p * LANE:(p + 1) * LANE]
            groups.append(jnp.where(low, pair, 0.0))
            groups.append(jnp.where(low, pltpu.roll(pair, LANE - FOX_HEAD_DIM, 1), 0.0))
        return groups

    put(fk_ref, fk)
    put(fv_ref, fv)

    z = misc + bias_ref[...]
    keep = (lane >= _LF_LANE) & (lane < _LF_LANE + FOX_HEADS)
    if valid_rows is not None:
        keep = keep & (lax.broadcasted_iota(jnp.int32, z.shape, 0) < valid_rows)
    lf = jnp.where(keep, jnp.minimum(z, 0.0) - jnp.log1p(jnp.exp(-jnp.abs(z))), 0.0)
    put(lf_ref, lf[:, _LF_LANE:_LF_LANE + FOX_HEADS])
    hi, mid, lo = _split3(lf)
    tri = tri_ref[...]
    cum = _dot(tri, hi) + _dot(tri, mid) + _dot(tri, lo) + carry_scr[...]
    n_rows = cum.shape[0]
    carry_scr[...] = cum[n_rows - 1:n_rows, :]
    carry_out_ref[0] = cum[n_rows - 1:n_rows, :]
    c_hi, c_mid, c_lo = _split3(cum * LOG2E)
    kb = _dot(jnp.concatenate([c_hi, c_mid, c_lo], axis=1), place_ref[...])
    fq_wide = jnp.concatenate(head_groups(fq), axis=1)
    fq_ref[0] = (fq_wide * (FOX_SCALE * LOG2E) + ones_q_ref[...]).astype(BF16)
    fka_ref[0] = (jnp.concatenate(head_groups(fk), axis=1) + kb).astype(BF16)


def _const_spec(shape):
    zeros = (0,) * len(shape)
    return pl.BlockSpec(shape, lambda *_: zeros, pipeline_mode=pl.Buffered(1))


def _proj_call(x, carry_in, tabs, w, tile, valid_rows=None, transposed_v=False, row_offset=0):
    B, L, D = x.shape
    assert L % tile == 0
    nt = L // tile
    tri = jnp.asarray(np.tril(np.ones((tile, tile), np.float32)), BF16)

    row = lambda width: pl.BlockSpec((1, tile, width), lambda b, t: (b, t, 0))
    tab = pl.BlockSpec((tile, LANE), lambda b, t: (t, 0))
    in_specs = [row(D), pl.BlockSpec((1, 1, LANE), lambda b, t: (b, 0, 0)), tab, tab, tab, tab,
                _const_spec((1, D)), _const_spec((D, _NPROJ)), _const_spec((1, LANE)),
                _const_spec((1, _MLA_Q_LORA)), _const_spec((_MLA_Q_LORA, _WIDE)),
                _const_spec((1, _MLA_KV_LORA)), _const_spec((_MLA_KV_LORA + LANE, _WIDE)),
                _const_spec((_MLA_KV_LORA, MLA_HEADS * MLA_V)),
                _const_spec((tile, tile)), _const_spec((3 * LANE, _WIDE)), _const_spec((1, _WIDE))]
    out_shapes = [
        jax.ShapeDtypeStruct((B, L, _MLA_KV_LORA), F32),
        jax.ShapeDtypeStruct((B, L, MLA_ROPE), F32),
        jax.ShapeDtypeStruct((B, L, _FOX_WIDTH), F32),
        jax.ShapeDtypeStruct((B, L, _FOX_WIDTH), F32),
        jax.ShapeDtypeStruct((B, L, FOX_HEADS), F32),
        jax.ShapeDtypeStruct((B, L, _WIDE), BF16),
        jax.ShapeDtypeStruct((B, L, _WIDE), BF16),
        jax.ShapeDtypeStruct((B, L, MLA_HEADS * MLA_V), BF16),
        jax.ShapeDtypeStruct((B, L, _WIDE), BF16),
        jax.ShapeDtypeStruct((B, L, _WIDE), BF16),
        jax.ShapeDtypeStruct((B, L, _FOX_WIDTH), BF16),
        jax.ShapeDtypeStruct((B, 1, LANE), F32),
    ]
    out_specs = [row(s.shape[-1]) for s in out_shapes[:-1]]
    out_specs.append(pl.BlockSpec((1, 1, LANE), lambda b, t: (b, 0, 0)))
    if transposed_v:
        for idx in (7, 10):
            vw = out_shapes[idx].shape[-1]
            out_shapes[idx] = jax.ShapeDtypeStruct((B, vw, L), BF16)
            out_specs[idx] = pl.BlockSpec((1, vw, tile), lambda b, t: (b, 0, t))
    if row_offset:
        assert row_offset % SUBLANE == 0 and L % SUBLANE == 0
        for idx in range(5):
            width = out_shapes[idx].shape[-1]
            out_shapes[idx] = jax.ShapeDtypeStruct((B * (row_offset + L), width), F32)
            out_specs[idx] = pl.BlockSpec(
                (pl.Element(tile), pl.Element(width)),
                lambda b, t: (pl.multiple_of(b * (row_offset + L) + row_offset + t * tile, SUBLANE), 0))
    bases = [jnp.zeros(s.shape, s.dtype) for s in out_shapes[:5]] if row_offset else []
    n_base = len(bases)
    return pl.pallas_call(
        functools.partial(_proj_kernel, n_base=n_base, valid_rows=valid_rows, transposed_v=transposed_v),
        out_shape=out_shapes,
        grid=(B, nt),
        in_specs=[pl.BlockSpec(memory_space=pl.ANY)] * n_base + in_specs,
        out_specs=out_specs,
        input_output_aliases={i: i for i in range(n_base)},
        scratch_shapes=[pltpu.VMEM((1, LANE), F32)],
        compiler_params=pltpu.CompilerParams(
            dimension_semantics=("parallel", "arbitrary"), vmem_limit_bytes=V7X_VMEM_LIMIT),
        name="proj",
    )(*bases, x, carry_in, *tabs, w["g_mix"], w["w_in"], w["bias_f"], w["q_norm"], w["w_uq2"],
      w["kv_norm"], w["w_kexp"], w["w_uv"], tri, w["place_k"], w["ones_q"])


def _place_rows_kernel(*refs):
    n = len(refs) // 3
    for src, out in zip(refs[:n], refs[2 * n:]):
        out[0] = src[...]


def _place_rows_call(rows, dests):
    n = len(rows)
    B = dests[0].shape[0]
    in_specs = [pl.BlockSpec(r.shape, lambda b, nd=r.ndim: (0,) * nd) for r in rows]
    in_specs += [pl.BlockSpec(memory_space=pl.ANY) for _ in dests]
    out_specs = [pl.BlockSpec((1,) + r.shape, lambda b, nd=r.ndim: (b,) + (0,) * nd) for r in rows]
    return pl.pallas_call(
        _place_rows_kernel,
        out_shape=[jax.ShapeDtypeStruct(d.shape, d.dtype) for d in dests],
        grid=(B,),
        in_specs=in_specs,
        out_specs=out_specs,
        input_output_aliases={n + i: i for i in range(n)},
        compiler_params=pltpu.CompilerParams(dimension_semantics=("arbitrary",)),
        name="place_meta_rows",
    )(*rows, *dests)


def _softmax_update(h, s, v_pair, m_scr, l_scr, acc_scr):
    m_prev = m_scr[h]
    m_new = jnp.maximum(m_prev, jnp.max(s, axis=1, keepdims=True))
    alpha = jnp.exp2(m_prev - m_new)
    p = jnp.exp2(s - jnp.tile(m_new, (1, s.shape[1] // LANE)))
    l_scr[h] = alpha * l_scr[h] + jnp.sum(p, axis=1, keepdims=True)
    acc_scr[h] = alpha * acc_scr[h] + _dot(p.astype(BF16), v_pair)
    m_scr[h] = m_new


QK_LOOKAHEAD = 8
UNIT_Q = 256
UNIT_K = 256
ONES_ROWS = 16


def _with_ones_rows(v_t):
    row = lax.broadcasted_iota(jnp.int32, (ONES_ROWS, v_t.shape[1]), 0)
    return jnp.concatenate([v_t, jnp.where(row == 0, 1.0, 0.0).astype(v_t.dtype)], axis=0)


def _softmax_update_t(h, qs, s_t, v_t, m_scr, acc_scr):
    uk, uq = s_t.shape
    s3 = s_t.reshape(uk // SUBLANE, SUBLANE, uq)
    m_cur = jnp.max(s3, axis=0)
    for shift in (4, 2, 1):
        m_cur = jnp.maximum(m_cur, pltpu.roll(m_cur, shift, 0))
    m_prev = m_scr[h, :, qs]
    m_new = jnp.maximum(m_prev, m_cur)
    alpha = jnp.exp2(m_prev - m_new)
    p_t = jnp.exp2(s3 - m_new[None]).reshape(uk, uq)
    pv = _dot(v_t, p_t.astype(BF16))
    rows = pv.shape[0]
    acc = acc_scr[h, :, qs].reshape(rows // SUBLANE, SUBLANE, uq)
    acc_scr[h, :, qs] = (alpha[None] * acc).reshape(rows, uq) + pv
    m_scr[h, :, qs] = m_new


def _attn_kernel(qi_ref, kj_ref, q_ref, k_ref, vt_ref, kmeta_ref, vtmeta_ref, o_ref,
                 m_scr, acc_scr, *, chunk_mask, n_meta):
    p_id = pl.program_id(1)
    i = qi_ref[p_id]
    j = kj_ref[p_id]
    tq = q_ref.shape[1]
    tk = k_ref.shape[1]
    dv = acc_scr.shape[1] - ONES_ROWS

    def head(ref2d, h):
        return ref2d[:, h * HEAD_GROUP:(h + 1) * HEAD_GROUP]

    @pl.when(j == 0)
    def _():
        n_pad = kmeta_ref.shape[0]
        n_rows = -(-n_meta // BF16_SUBLANES) * BF16_SUBLANES
        row = lax.broadcasted_iota(jnp.int32, (n_rows, tq), 0)
        scores = [_nt_dot(kmeta_ref[:n_rows, h * HEAD_GROUP:(h + 1) * HEAD_GROUP], head(q_ref.at[0], h))
                  for h in range(HEADS)]
        for h in range(HEADS):
            s3 = jnp.where(row < n_meta, scores[h], NEG_INF).reshape(n_rows // SUBLANE, SUBLANE, tq)
            m_new = jnp.max(s3, axis=0)
            for shift in (4, 2, 1):
                m_new = jnp.maximum(m_new, pltpu.roll(m_new, shift, 0))
            p_t = jnp.exp2(s3 - m_new[None]).reshape(n_rows, tq).astype(BF16)
            p_t = jnp.concatenate([p_t, jnp.zeros((n_pad - n_rows, tq), BF16)], axis=0)
            acc_scr[h] = _dot(_with_ones_rows(vtmeta_ref[h * dv:(h + 1) * dv, :]), p_t)
            m_scr[h] = m_new

    uq = min(UNIT_Q, tq)
    uk = min(UNIT_K, tk)

    def block(key_shift):
        masked = key_shift is not None
        units = [(h, q0, k0) for h in range(HEADS) for k0 in range(0, tk, uk) for q0 in range(0, tq, uq)]
        if masked:
            units = [u for u in units if key_shift + u[2] < u[1] + uq]
            kk = lax.broadcasted_iota(jnp.int32, (uk, uq), 0) + key_shift
            qq = lax.broadcasted_iota(jnp.int32, (uk, uq), 1)

        def qk(u):
            h, q0, k0 = u
            return _nt_dot(k_ref[0, k0:k0 + uk, h * HEAD_GROUP:(h + 1) * HEAD_GROUP],
                           q_ref[0, q0:q0 + uq, h * HEAD_GROUP:(h + 1) * HEAD_GROUP])

        scores = {}
        for n in range(-QK_LOOKAHEAD, len(units)):
            if n + QK_LOOKAHEAD < len(units):
                scores[n + QK_LOOKAHEAD] = qk(units[n + QK_LOOKAHEAD])
            if n >= 0:
                h, q0, k0 = units[n]
                s_t = scores.pop(n)
                if masked and key_shift + k0 + uk > q0 + 1:
                    if chunk_mask:
                        valid = (kk + k0) // CHUNK <= (qq + q0) // CHUNK
                    else:
                        valid = kk + k0 <= qq + q0
                    s_t = jnp.where(valid, s_t, NEG_INF)
                v_t = _with_ones_rows(vt_ref[0, h * dv:(h + 1) * dv, k0:k0 + uk])
                _softmax_update_t(h, slice(q0, q0 + uq), s_t, v_t, m_scr, acc_scr)

    ratio = tq // tk
    d = j - ratio * i

    @pl.when(d < 0)
    def _():
        block(None)

    for dd in range(ratio):
        @pl.when(d == dd)
        def _(dd=dd):
            block(dd * tk)

    @pl.when(d == ratio - 1)
    def _():
        def normalised(h):
            return acc_scr[h, :dv, :] / acc_scr[h, dv:dv + 1, :]

        for hp in range(HEADS // 2):
            o_pair_t = jnp.concatenate([normalised(2 * hp), normalised(2 * hp + 1)], axis=0)
            o_ref[0, :, hp * 2 * dv:(hp + 1) * 2 * dv] = o_pair_t.T.astype(o_ref.dtype)


def _attn_call(q, k, v_t, k_meta, v_t_meta, *, chunk_mask, n_meta, tq, tk, name):
    B, F, _ = q.shape
    tq, tk = min(tq, F), min(tk, F)
    assert F % tq == 0 and tq % tk == 0 and tk % CHUNK == 0
    ratio = tq // tk
    pairs = [(i, j) for i in range(F // tq) for j in range(ratio * (i + 1))]
    qi = jnp.asarray([p[0] for p in pairs], jnp.int32)
    kj = jnp.asarray([p[1] for p in pairs], jnp.int32)
    vw = v_t.shape[1]
    grid_spec = pltpu.PrefetchScalarGridSpec(
        num_scalar_prefetch=2,
        grid=(B, len(pairs)),
        in_specs=[
            pl.BlockSpec((1, tq, _WIDE), lambda b, p, qi, kj: (b, qi[p], 0)),
            pl.BlockSpec((1, tk, _WIDE), lambda b, p, qi, kj: (b, kj[p], 0)),
            pl.BlockSpec((1, vw, tk), lambda b, p, qi, kj: (b, 0, kj[p])),
            pl.BlockSpec((LANE, _WIDE), lambda b, p, qi, kj: (0, 0)),
            pl.BlockSpec((vw, LANE), lambda b, p, qi, kj: (0, 0)),
        ],
        out_specs=pl.BlockSpec((1, tq, vw), lambda b, p, qi, kj: (b, qi[p], 0)),
        scratch_shapes=[pltpu.VMEM((HEADS, SUBLANE, tq), F32),
                        pltpu.VMEM((HEADS, vw // HEADS + ONES_ROWS, tq), F32)],
    )
    return pl.pallas_call(
        functools.partial(_attn_kernel, chunk_mask=chunk_mask, n_meta=n_meta),
        out_shape=jax.ShapeDtypeStruct((B, F, vw), BF16),
        grid_spec=grid_spec,
        compiler_params=pltpu.CompilerParams(
            dimension_semantics=("parallel", "arbitrary"), vmem_limit_bytes=V7X_VMEM_LIMIT),
        name=name,
    )(qi, kj, q, k, v_t, k_meta, v_t_meta)


def _ffn_kernel(x_ref, oa_ref, ob_ref, woa_ref, wob_ref, g_ref, wg_ref, wu_ref, wd_ref, gf_ref, y_ref):
    x1 = x_ref[...] + _dot(oa_ref[...], woa_ref[...]) + _dot(ob_ref[...], wob_ref[...])
    h = _rms(x1, g_ref[...]).astype(BF16)
    gate = _dot(h, wg_ref[...])
    up = _dot(h, wu_ref[...])
    act = (gate * jax.nn.sigmoid(gate) * up).astype(BF16)
    x2 = x1 + _dot(act, wd_ref[...])
    y_ref[...] = _rms(x2, gf_ref[...])


def _ffn_call(x, oa, ob, w, tile):
    R, D = x.shape
    tile = min(tile, R)
    assert R % tile == 0
    dm = oa.shape[-1]
    dff = w["w_gate"].shape[-1]
    row = lambda width: pl.BlockSpec((tile, width), lambda r: (r, 0))
    return pl.pallas_call(
        _ffn_kernel,
        out_shape=jax.ShapeDtypeStruct((R, D), F32),
        grid=(R // tile,),
        in_specs=[row(D), row(dm), row(dm),
                  _const_spec((dm, D)), _const_spec((dm, D)), _const_spec((1, D)),
                  _const_spec((D, dff)), _const_spec((D, dff)), _const_spec((dff, D)),
                  _const_spec((1, D))],
        out_specs=row(D),
        compiler_params=pltpu.CompilerParams(
            dimension_semantics=("parallel",), vmem_limit_bytes=V7X_VMEM_LIMIT),
        name="ffn",
    )(x, oa, ob, w["w_out_a"], w["w_out_b"], w["g_ffn"], w["w_gate"], w["w_up"], w["w_down"],
      w["g_final"])


_CUM_CHUNK = 256


def _cumsum_kernel(x_ref, u_ref, o_ref):
    rows, n = x_ref.shape
    u = u_ref[...]
    carry = jnp.zeros((rows, 1), F32)
    for c in range(n // _CUM_CHUNK):
        hi, mid, lo = _split3(x_ref[:, c * _CUM_CHUNK:(c + 1) * _CUM_CHUNK])
        y = _dot(hi, u) + _dot(mid, u) + _dot(lo, u) + carry
        o_ref[:, c * _CUM_CHUNK:(c + 1) * _CUM_CHUNK] = y
        carry = y[:, _CUM_CHUNK - 1:_CUM_CHUNK]


def _cumsum_call(x):
    rows, n = x.shape
    assert n % _CUM_CHUNK == 0
    u = jnp.asarray(np.triu(np.ones((_CUM_CHUNK, _CUM_CHUNK), np.float32)), BF16)
    return pl.pallas_call(
        _cumsum_kernel,
        out_shape=jax.ShapeDtypeStruct((rows, n), F32),
        compiler_params=pltpu.CompilerParams(vmem_limit_bytes=V7X_VMEM_LIMIT),
        name="cache_cumsum",
    )(x, u)


def _sample_fox_kernel(fq_ref, k_ref, v_ref, ck_ref, fkn_ref, fvn_ref, o_ref, m_scr, l_scr, acc_scr):
    j = pl.program_id(1)
    nj = pl.num_programs(1)
    s_new = fq_ref.shape[1]
    dim = acc_scr.shape[-1]

    @pl.when(j == 0)
    def _():
        m_scr[...] = jnp.full(m_scr.shape, NEG_INF, F32)
        l_scr[...] = jnp.zeros(l_scr.shape, F32)
        acc_scr[...] = jnp.zeros(acc_scr.shape, F32)

    def update(h, s, v_h):
        m_prev = m_scr[h]
        m_new = jnp.maximum(m_prev, jnp.max(s, axis=1, keepdims=True))
        alpha = jnp.exp2(m_prev - m_new)
        p = jnp.exp2(s - m_new[:, :1])
        l_scr[h] = alpha * l_scr[h] + jnp.sum(p, axis=1, keepdims=True)
        acc_scr[h] = alpha[:, :dim] * acc_scr[h] + _dot(p.astype(BF16), v_h)
        m_scr[h] = m_new

    ck2 = ck_ref[0] * LOG2E
    k_heads = jnp.swapaxes(k_ref[0], 0, 1)
    v_heads = jnp.swapaxes(v_ref[0], 0, 1)
    scores = [_nt_dot(fq_ref[0, :, h * HEAD_GROUP:h * HEAD_GROUP + dim], k_heads[h].astype(BF16))
              for h in range(HEADS)]
    for h in range(HEADS):
        update(h, scores[h] - ck2[h:h + 1, :], v_heads[h].astype(BF16))

    @pl.when(j == nj - 1)
    def _():
        r = lax.broadcasted_iota(jnp.int32, (s_new, s_new), 0)
        c = lax.broadcasted_iota(jnp.int32, (s_new, s_new), 1)
        new_scores = [_nt_dot(fq_ref[0, :, h * HEAD_GROUP:(h + 1) * HEAD_GROUP],
                              fkn_ref[0, :, h * HEAD_GROUP:(h + 1) * HEAD_GROUP]) for h in range(HEADS)]
        for h in range(HEADS):
            update(h, jnp.where(c <= r, new_scores[h], NEG_INF), fvn_ref[0, :, h * dim:(h + 1) * dim])
        o_ref[0] = jnp.concatenate([acc_scr[h] / l_scr[h][:, :dim] for h in range(HEADS)],
                                   axis=1).astype(o_ref.dtype)


def _sample_fox_call(fq, cache_k, cache_v, cum_cache, fka_new, fv_new, tile):
    B, S, _ = fq.shape
    _, past, heads, dim = cache_k.shape
    assert past % tile == 0 and heads == HEADS
    w = heads * dim
    cache_spec = pl.BlockSpec((1, tile, heads, dim), lambda b, j: (b, j, 0, 0))
    return pl.pallas_call(
        _sample_fox_kernel,
        out_shape=jax.ShapeDtypeStruct((B, S, w), BF16),
        grid=(B, past // tile),
        in_specs=[pl.BlockSpec((1, S, _WIDE), lambda b, j: (b, 0, 0)),
                  cache_spec, cache_spec,
                  pl.BlockSpec((1, HEADS, tile), lambda b, j: (b, 0, j)),
                  pl.BlockSpec((1, S, _WIDE), lambda b, j: (b, 0, 0)),
                  pl.BlockSpec((1, S, w), lambda b, j: (b, 0, 0))],
        out_specs=pl.BlockSpec((1, S, w), lambda b, j: (b, 0, 0)),
        scratch_shapes=[pltpu.VMEM((HEADS, S, LANE), F32),
                        pltpu.VMEM((HEADS, S, LANE), F32),
                        pltpu.VMEM((HEADS, S, dim), F32)],
        compiler_params=pltpu.CompilerParams(
            dimension_semantics=("parallel", "arbitrary"), vmem_limit_bytes=V7X_VMEM_LIMIT),
        name="sample_fox",
    )(fq, cache_k, cache_v, cum_cache, fka_new, fv_new)


def _sample_mla_kernel(q_ref, lat_ref, kr_ref, latn_ref, krn_ref, wabs_ref, prope_ref, wv_ref, o_ref,
                       ql_scr, qr_scr, m_scr, l_scr, acc_scr):
    j = pl.program_id(1)
    nj = pl.num_programs(1)
    s_new = q_ref.shape[1]

    @pl.when(j == 0)
    def _():
        m_scr[...] = jnp.full(m_scr.shape, NEG_INF, F32)
        l_scr[...] = jnp.zeros(l_scr.shape, F32)
        acc_scr[...] = jnp.zeros(acc_scr.shape, F32)
        for h in range(HEADS):
            q_h = q_ref[0, :, h * HEAD_GROUP:(h + 1) * HEAD_GROUP]
            ql_scr[h * s_new:(h + 1) * s_new, :] = _dot(q_h, wabs_ref[h]).astype(BF16)
            qr_scr[h * s_new:(h + 1) * s_new, :] = _dot(q_h, prope_ref[...]).astype(BF16)

    def update(lat, kr):
        s = _nt_dot(ql_scr[...], lat) + _nt_dot(qr_scr[...], kr)
        m_prev = m_scr[...]
        m_new = jnp.maximum(m_prev, jnp.max(s, axis=1, keepdims=True))
        alpha = jnp.exp2(m_prev - m_new)
        p = jnp.exp2(s - m_new[:, :1])
        l_scr[...] = alpha * l_scr[...] + jnp.sum(p, axis=1, keepdims=True)
        acc_scr[...] = jnp.tile(alpha, (1, acc_scr.shape[1] // LANE)) * acc_scr[...] + _dot(p.astype(BF16), lat)
        m_scr[...] = m_new

    def pad_rope(kr):
        return jnp.concatenate([kr, jnp.zeros((kr.shape[0], LANE - MLA_ROPE), kr.dtype)], axis=1)

    update(lat_ref[0].astype(BF16), pad_rope(kr_ref[0]).astype(BF16))

    @pl.when(j == nj - 1)
    def _():
        update(latn_ref[0].astype(BF16), pad_rope(krn_ref[0]).astype(BF16))
        o_lat = (acc_scr[...] / jnp.tile(l_scr[...], (1, acc_scr.shape[1] // LANE))).astype(BF16)
        out = _dot(o_lat[0:s_new], wv_ref[0])
        for h in range(1, HEADS):
            out = out + _dot(o_lat[h * s_new:(h + 1) * s_new], wv_ref[h])
        o_ref[0] = out.astype(o_ref.dtype)


def _sample_mla_call(q, cache_lat, cache_kr, lat_new, kr_new, w, tile):
    B, S, _ = q.shape
    past = cache_lat.shape[1]
    assert past % tile == 0
    c = cache_lat.shape[-1]
    ow = MLA_HEADS * MLA_V
    return pl.pallas_call(
        _sample_mla_kernel,
        out_shape=jax.ShapeDtypeStruct((B, S, ow), BF16),
        grid=(B, past // tile),
        in_specs=[pl.BlockSpec((1, S, _WIDE), lambda b, j: (b, 0, 0)),
                  pl.BlockSpec((1, tile, c), lambda b, j: (b, j, 0)),
                  pl.BlockSpec((1, tile, MLA_ROPE), lambda b, j: (b, j, 0)),
                  pl.BlockSpec((1, S, c), lambda b, j: (b, 0, 0)),
                  pl.BlockSpec((1, S, MLA_ROPE), lambda b, j: (b, 0, 0)),
                  _const_spec((HEADS, HEAD_GROUP, c)), _const_spec((HEAD_GROUP, LANE)),
                  _const_spec((HEADS, c, ow))],
        out_specs=pl.BlockSpec((1, S, ow), lambda b, j: (b, 0, 0)),
        scratch_shapes=[pltpu.VMEM((HEADS * S, c), BF16),
                        pltpu.VMEM((HEADS * S, LANE), BF16),
                        pltpu.VMEM((HEADS * S, LANE), F32),
                        pltpu.VMEM((HEADS * S, LANE), F32),
                        pltpu.VMEM((HEADS * S, c), F32)],
        compiler_params=pltpu.CompilerParams(
            dimension_semantics=("parallel", "arbitrary"), vmem_limit_bytes=V7X_VMEM_LIMIT),
        name="sample_mla",
    )(q, cache_lat, cache_kr, lat_new, kr_new, w["w_abs"], w["p_rope"], w["w_v_wide"])


def _gather_cols(w, src):
    src = np.asarray(src)
    pieces, start = [], 0
    for i in range(1, len(src) + 1):
        run_ends = i == len(src) or (src[i] != src[i - 1] + 1 if src[i - 1] >= 0 else src[i] >= 0)
        if run_ends:
            if src[start] >= 0:
                pieces.append(w[:, int(src[start]):int(src[start]) + i - start])
            else:
                pieces.append(jnp.zeros((w.shape[0], i - start), w.dtype))
            start = i
    return jnp.concatenate(pieces, axis=1)


def _prep_weights(norm_mix, w_in, b_forget, mla_q_norm, w_mla_uq, mla_kv_norm, w_mla_ukv, w_out,
                  norm_ffn, w_ffn_gate, w_ffn_up, w_ffn_down, norm_final):
    d_model = w_in.shape[0]
    o_kr = _MLA_Q_LORA + _MLA_KV_LORA
    o_fq = o_kr + MLA_ROPE
    o_fk = o_fq + _FOX_WIDTH
    o_fv = o_fk + _FOX_WIDTH
    o_fl = o_fv + _FOX_WIDTH
    half = MLA_ROPE // 2

    src = -np.ones((_NPROJ,), np.int64)
    src[_O_CQ:_O_CKV] = np.arange(0, _MLA_Q_LORA)
    src[_O_CKV:_O_FQ] = np.arange(_MLA_Q_LORA, o_kr)
    src[_O_FQ:_O_FK] = np.arange(o_fq, o_fk)
    src[_O_FK:_O_FV] = np.arange(o_fk, o_fv)
    src[_O_FV:_O_MISC] = np.arange(o_fv, o_fl)
    src[_O_MISC:_O_MISC + MLA_ROPE] = np.arange(o_kr, o_fq)
    src[_O_MISC + _LF_LANE:_O_MISC + _LF_LANE + FOX_HEADS] = np.arange(o_fl, o_fl + FOX_HEADS)
    w_in_r = _gather_cols(w_in, src).astype(BF16)

    src_a = -np.ones((_WIDE,), np.int64)
    for h in range(MLA_HEADS):
        src_a[h * HEAD_GROUP + np.arange(MLA_QK_DIM)] = h * MLA_QK_DIM + np.arange(MLA_QK_DIM)
    w_uq2 = _gather_cols(w_mla_uq, src_a).astype(BF16)

    src_k = -np.ones((_WIDE,), np.int64)
    src_v = np.zeros((MLA_HEADS * MLA_V,), np.int64)
    for h in range(MLA_HEADS):
        src_k[h * HEAD_GROUP + np.arange(MLA_NOPE)] = h * (MLA_NOPE + MLA_V) + np.arange(MLA_NOPE)
        src_v[h * MLA_V + np.arange(MLA_V)] = h * (MLA_NOPE + MLA_V) + MLA_NOPE + np.arange(MLA_V)
    place_r = np.zeros((LANE, _WIDE), np.float32)
    for h in range(MLA_HEADS):
        place_r[np.arange(MLA_ROPE), h * HEAD_GROUP + MLA_NOPE + np.arange(MLA_ROPE)] = 1.0
    w_uk_wide = _gather_cols(w_mla_ukv, src_k)
    w_kexp = jnp.concatenate([w_uk_wide, jnp.asarray(place_r)], axis=0).astype(BF16)
    w_uv = _gather_cols(w_mla_ukv, src_v).astype(BF16)

    w_uk = w_mla_ukv.reshape(_MLA_KV_LORA, MLA_HEADS, MLA_NOPE + MLA_V)[:, :, :MLA_NOPE]
    w_abs = jnp.zeros((MLA_HEADS, HEAD_GROUP, _MLA_KV_LORA), F32)
    w_abs = w_abs.at[:, :MLA_NOPE, :].set(jnp.transpose(w_uk, (1, 2, 0))).astype(BF16)
    p_rope = np.zeros((HEAD_GROUP, LANE), np.float32)
    p_rope[MLA_NOPE + np.arange(MLA_ROPE), np.arange(MLA_ROPE)] = 1.0
    w_uv_h = w_mla_ukv.reshape(_MLA_KV_LORA, MLA_HEADS, MLA_NOPE + MLA_V)[:, :, MLA_NOPE:]
    eye = jnp.asarray(np.eye(MLA_HEADS, dtype=np.float32))
    w_v_wide = (jnp.transpose(w_uv_h, (1, 0, 2))[:, :, None, :] * eye[:, None, :, None])
    w_v_wide = w_v_wide.reshape(MLA_HEADS, _MLA_KV_LORA, MLA_HEADS * MLA_V).astype(BF16)

    bias_f = jnp.zeros((1, LANE), F32).at[0, _LF_LANE:_LF_LANE + FOX_HEADS].set(b_forget.astype(F32))
    place_k = np.zeros((3 * LANE, _WIDE), np.float32)
    ones_q = np.zeros((1, _WIDE), np.float32)
    for h in range(FOX_HEADS):
        for part in range(3):
            place_k[part * LANE + _LF_LANE + h, h * HEAD_GROUP + FOX_HEAD_DIM + part] = -1.0
            ones_q[0, h * HEAD_GROUP + FOX_HEAD_DIM + part] = 1.0

    d_mla = MLA_HEADS * MLA_V
    return {
        "g_mix": norm_mix.reshape(1, d_model).astype(F32),
        "w_in": w_in_r,
        "bias_f": bias_f,
        "q_norm": mla_q_norm.reshape(1, -1).astype(F32),
        "w_uq2": w_uq2,
        "kv_norm": mla_kv_norm.reshape(1, -1).astype(F32),
        "w_kexp": w_kexp,
        "w_uv": w_uv,
        "place_k": jnp.asarray(place_k, BF16),
        "ones_q": jnp.asarray(ones_q),
        "w_abs": w_abs,
        "p_rope": jnp.asarray(p_rope, BF16),
        "w_v_wide": w_v_wide,
        "w_out_a": w_out[:d_mla].astype(BF16),
        "w_out_b": w_out[d_mla:].astype(BF16),
        "g_ffn": norm_ffn.reshape(1, d_model).astype(F32),
        "w_gate": w_ffn_gate.astype(BF16),
        "w_up": w_ffn_up.astype(BF16),
        "w_down": w_ffn_down.astype(BF16),
        "g_final": norm_final.reshape(1, d_model).astype(F32),
    }


def _rope_tables(pos):
    half = MLA_ROPE // 2
    inv_freq = ROPE_THETA ** (-jnp.arange(half, dtype=F32) / half)
    ang = pos.astype(F32)[:, None] * inv_freq[None, :]
    cos, sin = jnp.cos(ang), jnp.sin(ang)
    n = pos.shape[0]
    sc = MLA_SCALE * LOG2E
    zeros = lambda w: jnp.zeros((n, w), F32)
    tqc = jnp.concatenate([jnp.full((n, MLA_NOPE), sc, F32), sc * cos, sc * cos,
                           zeros(LANE - MLA_QK_DIM)], axis=1)
    tqs = jnp.concatenate([zeros(MLA_NOPE), -sc * sin, sc * sin, zeros(LANE - MLA_QK_DIM)], axis=1)
    tkc = jnp.concatenate([cos, cos, zeros(LANE - MLA_ROPE)], axis=1)
    tks = jnp.concatenate([-sin, sin, zeros(LANE - MLA_ROPE)], axis=1)
    return tqc, tqs, tkc, tks


def _pad_rows(a, rows):
    return jnp.concatenate([a, jnp.zeros((rows - a.shape[0],) + a.shape[1:], a.dtype)], axis=0)


PROJ_TILE = 256
ATTN_TQ = 1024
ATTN_TK = 512
FFN_TILE = 256
SAMPLE_TILE = 1024


def kernel(x_prompt, x_sample, cache_mla_latent, cache_mla_krope, cache_fox_k, cache_fox_v, cache_fox_logf, meta_tokens, norm_mix, w_in, b_forget, mla_q_norm, w_mla_uq, mla_kv_norm, w_mla_ukv, w_out, norm_ffn, w_ffn_gate, w_ffn_up, w_ffn_down, norm_final):
    depth = w_in.shape[0]
    assert depth == 1, "single-layer trunk: the meta rows' mixing output never reaches an output"
    B, seq, d_model = x_prompt.shape
    SB, s_new, _ = x_sample.shape
    n_meta = meta_tokens.shape[0]
    past = cache_mla_latent.shape[2]
    assert n_meta <= LANE

    w = _prep_weights(norm_mix[0], w_in[0], b_forget[0], mla_q_norm[0], w_mla_uq[0], mla_kv_norm[0],
                      w_mla_ukv[0], w_out[0], norm_ffn[0], w_ffn_gate[0], w_ffn_up[0], w_ffn_down[0],
                      norm_final)

    zero_carry = jnp.zeros((1, 1, LANE), F32)
    meta = _proj_call(_pad_rows(meta_tokens.astype(F32), LANE)[None], zero_carry,
                      _rope_tables(jnp.arange(LANE)), w, LANE, valid_rows=n_meta, transposed_v=True)
    (m_lat, m_kr, m_fk, m_fv, m_lf, _, m_km, m_vm, _, m_fka, m_fvb, m_carry) = meta
    frames = _proj_call(x_prompt, jnp.broadcast_to(m_carry, (B, 1, LANE)),
                        _rope_tables(n_meta + jnp.arange(seq)), w, PROJ_TILE, transposed_v=True,
                        row_offset=n_meta)
    f_qm, f_km, f_vm, f_fq, f_fka, f_fvb = frames[5:11]
    lat_p, kr_p, fk_p, fv_p, lf_p = _place_rows_call(
        [a[0, :n_meta] for a in (m_lat, m_kr, m_fk, m_fv, m_lf)],
        [a.reshape((B, n_meta + seq) + a.shape[1:]) for a in frames[:5]])

    o_mla = _attn_call(f_qm, f_km, f_vm, m_km[0], m_vm[0],
                       chunk_mask=True, n_meta=n_meta, tq=ATTN_TQ, tk=ATTN_TK, name="attn_mla")
    o_fox = _attn_call(f_fq, f_fka, f_fvb, m_fka[0], m_fvb[0],
                       chunk_mask=False, n_meta=n_meta, tq=ATTN_TQ, tk=ATTN_TK, name="attn_fox")
    y_prompt = _ffn_call(x_prompt.reshape(B * seq, d_model), o_mla.reshape(B * seq, -1),
                         o_fox.reshape(B * seq, -1), w, FFN_TILE).reshape(B, seq, d_model)

    lat_p, kr_p, lf_p = lat_p[None], kr_p[None], lf_p[None]
    fk_p = fk_p.reshape(1, B, n_meta + seq, FOX_HEADS, FOX_HEAD_DIM)
    fv_p = fv_p.reshape(1, B, n_meta + seq, FOX_HEADS, FOX_HEAD_DIM)

    logf_t = jnp.transpose(cache_fox_logf[0].astype(F32), (0, 2, 1))
    cum_cache = _cumsum_call(logf_t.reshape(SB * FOX_HEADS, past)).reshape(SB, FOX_HEADS, past)
    carry_s = jnp.zeros((SB, 1, LANE), F32).at[:, 0, _LF_LANE:_LF_LANE + FOX_HEADS].set(
        cum_cache[:, :, past - 1])
    samp = _proj_call(x_sample, carry_s, _rope_tables(past + jnp.arange(s_new)), w, s_new)
    (s_lat, s_kr, s_fk, s_fv, s_lf, s_qm, _, _, s_fq, s_fka, s_fvb, _) = samp

    so_mla = _sample_mla_call(s_qm, cache_mla_latent[0], cache_mla_krope[0], s_lat, s_kr, w, SAMPLE_TILE)
    so_fox = _sample_fox_call(s_fq, cache_fox_k[0], cache_fox_v[0], cum_cache, s_fka, s_fvb, SAMPLE_TILE)
    y_sample = _ffn_call(x_sample.reshape(SB * s_new, d_model), so_mla.reshape(SB * s_new, -1),
                         so_fox.reshape(SB * s_new, -1), w, FFN_TILE).reshape(SB, s_new, d_model)

    heads = lambda a: a.reshape(1, SB, s_new, FOX_HEADS, FOX_HEAD_DIM)
    return (y_prompt, y_sample, lat_p, kr_p, fk_p, fv_p, lf_p,
            s_lat[None], s_kr[None], heads(s_fk), heads(s_fv), s_lf[None])
```

```python
import functools

import numpy as np
import jax
import jax.numpy as jnp
from jax import lax
from jax.experimental import pallas as pl
from jax.experimental.pallas import tpu as pltpu

CHUNK = 64
EPS = 1e-6
NEG_INF = -1e30
MLA_HEADS = 8
MLA_NOPE = 64
MLA_ROPE = 32
MLA_V = 64
MLA_QK_DIM = MLA_NOPE + MLA_ROPE
MLA_SCALE = MLA_QK_DIM ** -0.5
ROPE_THETA = 10000.0
FOX_HEADS = 8
FOX_HEAD_DIM = 64
FOX_SCALE = FOX_HEAD_DIM ** -0.5
LOG2E = 1.4426950408889634

LANE = 128
SUBLANE = 8
BF16_SUBLANES = 16
HEAD_GROUP = LANE
HEADS = 8
V7X_VMEM_LIMIT = 56 * 1024 * 1024

BF16 = jnp.bfloat16
F32 = jnp.float32

_MLA_Q_LORA = 384
_MLA_KV_LORA = 256
_FOX_WIDTH = FOX_HEADS * FOX_HEAD_DIM
_WIDE = HEADS * HEAD_GROUP
_O_CQ = 0
_O_CKV = _O_CQ + _MLA_Q_LORA
_O_FQ = _O_CKV + _MLA_KV_LORA
_O_FK = _O_FQ + _FOX_WIDTH
_O_FV = _O_FK + _FOX_WIDTH
_O_MISC = _O_FV + _FOX_WIDTH
_NPROJ = _O_MISC + LANE
_LF_LANE = MLA_ROPE


def _nt_dot(a, b):
    return lax.dot_general(a, b, (((1,), (1,)), ((), ())), preferred_element_type=F32)


def _dot(a, b):
    return jnp.dot(a, b, preferred_element_type=F32)


def _rms(x, g):
    return x * lax.rsqrt(jnp.mean(x * x, axis=-1, keepdims=True) + EPS) * g


def _split3(x):
    hi = x.astype(BF16)
    r = x - hi.astype(F32)
    mid = r.astype(BF16)
    lo = (r - mid.astype(F32)).astype(BF16)
    return hi, mid, lo


def _proj_kernel(*refs, n_base, **static):
    _proj_body(*refs[n_base:], **static)


def _proj_body(x_ref, carry_in_ref, tqc_ref, tqs_ref, tkc_ref, tks_ref,
               g_ref, w_in_ref, bias_ref, qn_ref, w_uq_ref, kvn_ref, w_kexp_ref, w_uv_ref,
               tri_ref, place_ref, ones_q_ref,
               lat_ref, kr_ref, fk_ref, fv_ref, lf_ref,
               qm_ref, km_ref, vm_ref, fq_ref, fka_ref, fvb_ref, carry_out_ref,
               carry_scr, *, valid_rows, transposed_v):
    t = pl.program_id(1)

    @pl.when(t == 0)
    def _():
        carry_scr[...] = carry_in_ref[0]

    x = x_ref[0]
    h = _rms(x, g_ref[...]).astype(BF16)
    proj = _dot(h, w_in_ref[...])

    lane = lax.broadcasted_iota(jnp.int32, (x.shape[0], LANE), 1)
    half = MLA_ROPE // 2

    def rotary(a, first, cos_tab, sin_tab):
        partner = jnp.where(lane < first + half, pltpu.roll(a, LANE - half, 1), pltpu.roll(a, half, 1))
        return a * cos_tab + partner * sin_tab

    def put(ref, val):
        ref[...] = val.reshape(ref.shape)

    cqn = _rms(proj[:, _O_CQ:_O_CKV], qn_ref[...]).astype(BF16)
    q2 = _dot(cqn, w_uq_ref[...])
    tqc, tqs = tqc_ref[...], tqs_ref[...]
    qm_ref[0] = jnp.concatenate(
        [rotary(q2[:, h * HEAD_GROUP:(h + 1) * HEAD_GROUP], MLA_NOPE, tqc, tqs) for h in range(HEADS)],
        axis=1).astype(BF16)

    latent = _rms(proj[:, _O_CKV:_O_FQ], kvn_ref[...])
    put(lat_ref, latent)
    misc = proj[:, _O_MISC:_NPROJ]
    kr = rotary(misc, 0, tkc_ref[...], tks_ref[...])
    put(kr_ref, kr[:, :MLA_ROPE])
    lat_bf = latent.astype(BF16)
    kcat = jnp.concatenate([lat_bf, kr.astype(BF16)], axis=1)
    km_ref[0] = _dot(kcat, w_kexp_ref[...]).astype(BF16)
    v_mla = _dot(lat_bf, w_uv_ref[...])
    vm_ref[0] = (v_mla.T if transposed_v else v_mla).astype(BF16)

    fq = proj[:, _O_FQ:_O_FK]
    fk = proj[:, _O_FK:_O_FV]
    fv = proj[:, _O_FV:_O_MISC]
    fvb_ref[0] = (fv.T if transposed_v else fv).astype(BF16)

    def head_groups(a):
        low = lane < FOX_HEAD_DIM
        groups = []
        for p in range(FOX_HEADS // 2):
            pair = a[:, p * LANE:(p + 1) * LANE]
            groups.append(jnp.where(low, pair, 0.0))
            groups.append(jnp.where(low, pltpu.roll(pair, LANE - FOX_HEAD_DIM, 1), 0.0))
        return groups

    put(fk_ref, fk)
    put(fv_ref, fv)

    z = misc + bias_ref[...]
    keep = (lane >= _LF_LANE) & (lane < _LF_LANE + FOX_HEADS)
    if valid_rows is not None:
        keep = keep & (lax.broadcasted_iota(jnp.int32, z.shape, 0) < valid_rows)
    lf = jnp.where(keep, jnp.minimum(z, 0.0) - jnp.log1p(jnp.exp(-jnp.abs(z))), 0.0)
    put(lf_ref, lf[:, _LF_LANE:_LF_LANE + FOX_HEADS])
    hi, mid, lo = _split3(lf)
    tri = tri_ref[...]
    cum = _dot(tri, hi) + _dot(tri, mid) + _dot(tri, lo) + carry_scr[...]
    n_rows = cum.shape[0]
    carry_scr[...] = cum[n_rows - 1:n_rows, :]
    carry_out_ref[0] = cum[n_rows - 1:n_rows, :]
    c_hi, c_mid, c_lo = _split3(cum * LOG2E)
    kb = _dot(jnp.concatenate([c_hi, c_mid, c_lo], axis=1), place_ref[...])
    fq_wide = jnp.concatenate(head_groups(fq), axis=1)
    fq_ref[0] = (fq_wide * (FOX_SCALE * LOG2E) + ones_q_ref[...]).astype(BF16)
    fka_ref[0] = (jnp.concatenate(head_groups(fk), axis=1) + kb).astype(BF16)


def _const_spec(shape):
    zeros = (0,) * len(shape)
    return pl.BlockSpec(shape, lambda *_: zeros, pipeline_mode=pl.Buffered(1))


def _proj_call(x, carry_in, tabs, w, tile, valid_rows=None, transposed_v=False, row_offset=0):
    B, L, D = x.shape
    assert L % tile == 0
    nt = L // tile
    tri = jnp.asarray(np.tril(np.ones((tile, tile), np.float32)), BF16)

    row = lambda width: pl.BlockSpec((1, tile, width), lambda b, t: (b, t, 0))
    tab = pl.BlockSpec((tile, LANE), lambda b, t: (t, 0))
    in_specs = [row(D), pl.BlockSpec((1, 1, LANE), lambda b, t: (b, 0, 0)), tab, tab, tab, tab,
                _const_spec((1, D)), _const_spec((D, _NPROJ)), _const_spec((1, LANE)),
                _const_spec((1, _MLA_Q_LORA)), _const_spec((_MLA_Q_LORA, _WIDE)),
                _const_spec((1, _MLA_KV_LORA)), _const_spec((_MLA_KV_LORA + LANE, _WIDE)),
                _const_spec((_MLA_KV_LORA, MLA_HEADS * MLA_V)),
                _const_spec((tile, tile)), _const_spec((3 * LANE, _WIDE)), _const_spec((1, _WIDE))]
    out_shapes = [
        jax.ShapeDtypeStruct((B, L, _MLA_KV_LORA), F32),
        jax.ShapeDtypeStruct((B, L, MLA_ROPE), F32),
        jax.ShapeDtypeStruct((B, L, _FOX_WIDTH), F32),
        jax.ShapeDtypeStruct((B, L, _FOX_WIDTH), F32),
        jax.ShapeDtypeStruct((B, L, FOX_HEADS), F32),
        jax.ShapeDtypeStruct((B, L, _WIDE), BF16),
        jax.ShapeDtypeStruct((B, L, _WIDE), BF16),
        jax.ShapeDtypeStruct((B, L, MLA_HEADS * MLA_V), BF16),
        jax.ShapeDtypeStruct((B, L, _WIDE), BF16),
        jax.ShapeDtypeStruct((B, L, _WIDE), BF16),
        jax.ShapeDtypeStruct((B, L, _FOX_WIDTH), BF16),
        jax.ShapeDtypeStruct((B, 1, LANE), F32),
    ]
    out_specs = [row(s.shape[-1]) for s in out_shapes[:-1]]
    out_specs.append(pl.BlockSpec((1, 1, LANE), lambda b, t: (b, 0, 0)))
    if transposed_v:
        for idx in (7, 10):
            vw = out_shapes[idx].shape[-1]
            out_shapes[idx] = jax.ShapeDtypeStruct((B, vw, L), BF16)
            out_specs[idx] = pl.BlockSpec((1, vw, tile), lambda b, t: (b, 0, t))
    if row_offset:
        assert row_offset % SUBLANE == 0 and L % SUBLANE == 0
        for idx in range(5):
            width = out_shapes[idx].shape[-1]
            out_shapes[idx] = jax.ShapeDtypeStruct((B * (row_offset + L), width), F32)
            out_specs[idx] = pl.BlockSpec(
                (pl.Element(tile), pl.Element(width)),
                lambda b, t: (pl.multiple_of(b * (row_offset + L) + row_offset + t * tile, SUBLANE), 0))
    bases = [jnp.zeros(s.shape, s.dtype) for s in out_shapes[:5]] if row_offset else []
    n_base = len(bases)
    return pl.pallas_call(
        functools.partial(_proj_kernel, n_base=n_base, valid_rows=valid_rows, transposed_v=transposed_v),
        out_shape=out_shapes,
        grid=(B, nt),
        in_specs=[pl.BlockSpec(memory_space=pl.ANY)] * n_base + in_specs,
        out_specs=out_specs,
        input_output_aliases={i: i for i in range(n_base)},
        scratch_shapes=[pltpu.VMEM((1, LANE), F32)],
        compiler_params=pltpu.CompilerParams(
            dimension_semantics=("parallel", "arbitrary"), vmem_limit_bytes=V7X_VMEM_LIMIT),
        name="proj",
    )(*bases, x, carry_in, *tabs, w["g_mix"], w["w_in"], w["bias_f"], w["q_norm"], w["w_uq2"],
      w["kv_norm"], w["w_kexp"], w["w_uv"], tri, w["place_k"], w["ones_q"])


def _place_rows_kernel(*refs):
    n = len(refs) // 3
    for src, out in zip(refs[:n], refs[2 * n:]):
        out[0] = src[...]


def _place_rows_call(rows, dests):
    n = len(rows)
    B = dests[0].shape[0]
    in_specs = [pl.BlockSpec(r.shape, lambda b, nd=r.ndim: (0,) * nd) for r in rows]
    in_specs += [pl.BlockSpec(memory_space=pl.ANY) for _ in dests]
    out_specs = [pl.BlockSpec((1,) + r.shape, lambda b, nd=r.ndim: (b,) + (0,) * nd) for r in rows]
    return pl.pallas_call(
        _place_rows_kernel,
        out_shape=[jax.ShapeDtypeStruct(d.shape, d.dtype) for d in dests],
        grid=(B,),
        in_specs=in_specs,
        out_specs=out_specs,
        input_output_aliases={n + i: i for i in range(n)},
        compiler_params=pltpu.CompilerParams(dimension_semantics=("arbitrary",)),
        name="place_meta_rows",
    )(*rows, *dests)


def _softmax_update(h, s, v_pair, m_scr, l_scr, acc_scr):
    m_prev = m_scr[h]
    m_new = jnp.maximum(m_prev, jnp.max(s, axis=1, keepdims=True))
    alpha = jnp.exp2(m_prev - m_new)
    p = jnp.exp2(s - jnp.tile(m_new, (1, s.shape[1] // LANE)))
    l_scr[h] = alpha * l_scr[h] + jnp.sum(p, axis=1, keepdims=True)
    acc_scr[h] = alpha * acc_scr[h] + _dot(p.astype(BF16), v_pair)
    m_scr[h] = m_new


QK_LOOKAHEAD = 8
UNIT_Q = 256
UNIT_K = 256
ONES_ROWS = 16


def _with_ones_rows(v_t):
    row = lax.broadcasted_iota(jnp.int32, (ONES_ROWS, v_t.shape[1]), 0)
    return jnp.concatenate([v_t, jnp.where(row == 0, 1.0, 0.0).astype(v_t.dtype)], axis=0)


def _softmax_update_t(h, qs, s_t, v_t, m_scr, acc_scr):
    uk, uq = s_t.shape
    s3 = s_t.reshape(uk // SUBLANE, SUBLANE, uq)
    m_cur = jnp.max(s3, axis=0)
    for shift in (4, 2, 1):
        m_cur = jnp.maximum(m_cur, pltpu.roll(m_cur, shift, 0))
    m_prev = m_scr[h, :, qs]
    m_new = jnp.maximum(m_prev, m_cur)
    alpha = jnp.exp2(m_prev - m_new)
    p_t = jnp.exp2(s3 - m_new[None]).reshape(uk, uq)
    pv = _dot(v_t, p_t.astype(BF16))
    rows = pv.shape[0]
    acc = acc_scr[h, :, qs].reshape(rows // SUBLANE, SUBLANE, uq)
    acc_scr[h, :, qs] = (alpha[None] * acc).reshape(rows, uq) + pv
    m_scr[h, :, qs] = m_new


def _attn_kernel(qi_ref, kj_ref, q_ref, k_ref, vt_ref, kmeta_ref, vtmeta_ref, o_ref,
                 m_scr, acc_scr, *, chunk_mask, n_meta):
    p_id = pl.program_id(1)
    i = qi_ref[p_id]
    j = kj_ref[p_id]
    tq = q_ref.shape[1]
    tk = k_ref.shape[1]
    dv = acc_scr.shape[1] - ONES_ROWS

    def head(ref2d, h):
        return ref2d[:, h * HEAD_GROUP:(h + 1) * HEAD_GROUP]

    @pl.when(j == 0)
    def _():
        n_pad = kmeta_ref.shape[0]
        n_rows = -(-n_meta // BF16_SUBLANES) * BF16_SUBLANES
        row = lax.broadcasted_iota(jnp.int32, (n_rows, tq), 0)
        scores = [_nt_dot(kmeta_ref[:n_rows, h * HEAD_GROUP:(h + 1) * HEAD_GROUP], head(q_ref.at[0], h))
                  for h in range(HEADS)]
        for h in range(HEADS):
            s3 = jnp.where(row < n_meta, scores[h], NEG_INF).reshape(n_rows // SUBLANE, SUBLANE, tq)
            m_new = jnp.max(s3, axis=0)
            for shift in (4, 2, 1):
                m_new = jnp.maximum(m_new, pltpu.roll(m_new, shift, 0))
            p_t = jnp.exp2(s3 - m_new[None]).reshape(n_rows, tq).astype(BF16)
            p_t = jnp.concatenate([p_t, jnp.zeros((n_pad - n_rows, tq), BF16)], axis=0)
            acc_scr[h] = _dot(_with_ones_rows(vtmeta_ref[h * dv:(h + 1) * dv, :]), p_t)
            m_scr[h] = m_new

    uq = min(UNIT_Q, tq)
    uk = min(UNIT_K, tk)

    def block(key_shift):
        masked = key_shift is not None
        units = [(h, q0, k0) for h in range(HEADS) for k0 in range(0, tk, uk) for q0 in range(0, tq, uq)]
        if masked:
            units = [u for u in units if key_shift + u[2] < u[1] + uq]
            kk = lax.broadcasted_iota(jnp.int32, (uk, uq), 0) + key_shift
            qq = lax.broadcasted_iota(jnp.int32, (uk, uq), 1)

        def qk(u):
            h, q0, k0 = u
            return _nt_dot(k_ref[0, k0:k0 + uk, h * HEAD_GROUP:(h + 1) * HEAD_GROUP],
                           q_ref[0, q0:q0 + uq, h * HEAD_GROUP:(h + 1) * HEAD_GROUP])

        scores = {}
        for n in range(-QK_LOOKAHEAD, len(units)):
            if n + QK_LOOKAHEAD < len(units):
                scores[n + QK_LOOKAHEAD] = qk(units[n + QK_LOOKAHEAD])
            if n >= 0:
                h, q0, k0 = units[n]
                s_t = scores.pop(n)
                if masked and key_shift + k0 + uk > q0 + 1:
                    if chunk_mask:
                        valid = (kk + k0) // CHUNK <= (qq + q0) // CHUNK
                    else:
                        valid = kk + k0 <= qq + q0
                    s_t = jnp.where(valid, s_t, NEG_INF)
                v_t = _with_ones_rows(vt_ref[0, h * dv:(h + 1) * dv, k0:k0 + uk])
                _softmax_update_t(h, slice(q0, q0 + uq), s_t, v_t, m_scr, acc_scr)

    ratio = tq // tk
    d = j - ratio * i

    @pl.when(d < 0)
    def _():
        block(None)

    for dd in range(ratio):
        @pl.when(d == dd)
        def _(dd=dd):
            block(dd * tk)

    @pl.when(d == ratio - 1)
    def _():
        def normalised(h):
            return acc_scr[h, :dv, :] / acc_scr[h, dv:dv + 1, :]

        for hp in range(HEADS // 2):
            o_pair_t = jnp.concatenate([normalised(2 * hp), normalised(2 * hp + 1)], axis=0)
            o_ref[0, :, hp * 2 * dv:(hp + 1) * 2 * dv] = o_pair_t.T.astype(o_ref.dtype)


def _attn_call(q, k, v_t, k_meta, v_t_meta, *, chunk_mask, n_meta, tq, tk, name):
    B, F, _ = q.shape
    tq, tk = min(tq, F), min(tk, F)
    assert F % tq == 0 and tq % tk == 0 and tk % CHUNK == 0
    ratio = tq // tk
    pairs = [(i, j) for i in range(F // tq) for j in range(ratio * (i + 1))]
    qi = jnp.asarray([p[0] for p in pairs], jnp.int32)
    kj = jnp.asarray([p[1] for p in pairs], jnp.int32)
    vw = v_t.shape[1]
    grid_spec = pltpu.PrefetchScalarGridSpec(
        num_scalar_prefetch=2,
        grid=(B, len(pairs)),
        in_specs=[
            pl.BlockSpec((1, tq, _WIDE), lambda b, p, qi, kj: (b, qi[p], 0)),
            pl.BlockSpec((1, tk, _WIDE), lambda b, p, qi, kj: (b, kj[p], 0)),
            pl.BlockSpec((1, vw, tk), lambda b, p, qi, kj: (b, 0, kj[p])),
            pl.BlockSpec((LANE, _WIDE), lambda b, p, qi, kj: (0, 0)),
            pl.BlockSpec((vw, LANE), lambda b, p, qi, kj: (0, 0)),
        ],
        out_specs=pl.BlockSpec((1, tq, vw), lambda b, p, qi, kj: (b, qi[p], 0)),
        scratch_shapes=[pltpu.VMEM((HEADS, SUBLANE, tq), F32),
                        pltpu.VMEM((HEADS, vw // HEADS + ONES_ROWS, tq), F32)],
    )
    return pl.pallas_call(
        functools.partial(_attn_kernel, chunk_mask=chunk_mask, n_meta=n_meta),
        out_shape=jax.ShapeDtypeStruct((B, F, vw), BF16),
        grid_spec=grid_spec,
        compiler_params=pltpu.CompilerParams(
            dimension_semantics=("parallel", "arbitrary"), vmem_limit_bytes=V7X_VMEM_LIMIT),
        name=name,
    )(qi, kj, q, k, v_t, k_meta, v_t_meta)


def _ffn_kernel(x_ref, oa_ref, ob_ref, woa_ref, wob_ref, g_ref, wg_ref, wu_ref, wd_ref, gf_ref, y_ref):
    x1 = x_ref[...] + _dot(oa_ref[...], woa_ref[...]) + _dot(ob_ref[...], wob_ref[...])
    h = _rms(x1, g_ref[...]).astype(BF16)
    gate = _dot(h, wg_ref[...])
    up = _dot(h, wu_ref[...])
    act = (gate * jax.nn.sigmoid(gate) * up).astype(BF16)
    x2 = x1 + _dot(act, wd_ref[...])
    y_ref[...] = _rms(x2, gf_ref[...])


def _ffn_call(x, oa, ob, w, tile):
    R, D = x.shape
    tile = min(tile, R)
    assert R % tile == 0
    dm = oa.shape[-1]
    dff = w["w_gate"].shape[-1]
    row = lambda width: pl.BlockSpec((tile, width), lambda r: (r, 0))
    return pl.pallas_call(
        _ffn_kernel,
        out_shape=jax.ShapeDtypeStruct((R, D), F32),
        grid=(R // tile,),
        in_specs=[row(D), row(dm), row(dm),
                  _const_spec((dm, D)), _const_spec((dm, D)), _const_spec((1, D)),
                  _const_spec((D, dff)), _const_spec((D, dff)), _const_spec((dff, D)),
                  _const_spec((1, D))],
        out_specs=row(D),
        compiler_params=pltpu.CompilerParams(
            dimension_semantics=("parallel",), vmem_limit_bytes=V7X_VMEM_LIMIT),
        name="ffn",
    )(x, oa, ob, w["w_out_a"], w["w_out_b"], w["g_ffn"], w["w_gate"], w["w_up"], w["w_down"],
      w["g_final"])


_CUM_CHUNK = 256


def _cumsum_kernel(x_ref, u_ref, o_ref):
    rows, n = x_ref.shape
    u = u_ref[...]
    carry = jnp.zeros((rows, 1), F32)
    for c in range(n // _CUM_CHUNK):
        hi, mid, lo = _split3(x_ref[:, c * _CUM_CHUNK:(c + 1) * _CUM_CHUNK])
        y = _dot(hi, u) + _dot(mid, u) + _dot(lo, u) + carry
        o_ref[:, c * _CUM_CHUNK:(c + 1) * _CUM_CHUNK] = y
        carry = y[:, _CUM_CHUNK - 1:_CUM_CHUNK]


def _cumsum_call(x):
    rows, n = x.shape
    assert n % _CUM_CHUNK == 0
    u = jnp.asarray(np.triu(np.ones((_CUM_CHUNK, _CUM_CHUNK), np.float32)), BF16)
    return pl.pallas_call(
        _cumsum_kernel,
        out_shape=jax.ShapeDtypeStruct((rows, n), F32),
        compiler_params=pltpu.CompilerParams(vmem_limit_bytes=V7X_VMEM_LIMIT),
        name="cache_cumsum",
    )(x, u)


def _sample_fox_kernel(fq_ref, k_ref, v_ref, ck_ref, fkn_ref, fvn_ref, o_ref, m_scr, l_scr, acc_scr):
    j = pl.program_id(1)
    nj = pl.num_programs(1)
    s_new = fq_ref.shape[1]
    dim = acc_scr.shape[-1]

    @pl.when(j == 0)
    def _():
        m_scr[...] = jnp.full(m_scr.shape, NEG_INF, F32)
        l_scr[...] = jnp.zeros(l_scr.shape, F32)
        acc_scr[...] = jnp.zeros(acc_scr.shape, F32)

    def update(h, s, v_h, v_transposed):
        m_prev = m_scr[h]
        m_new = jnp.maximum(m_prev, jnp.max(s, axis=1, keepdims=True))
        alpha = jnp.exp2(m_prev - m_new)
        p = jnp.exp2(s - m_new[:, :1])
        l_scr[h] = alpha * l_scr[h] + jnp.sum(p, axis=1, keepdims=True)
        p = p.astype(BF16)
        pv = _nt_dot(p, v_h) if v_transposed else _dot(p, v_h)
        acc_scr[h] = alpha[:, :dim] * acc_scr[h] + pv
        m_scr[h] = m_new

    ck2 = ck_ref[0] * LOG2E
    scores = [_dot(fq_ref[0, :, h * HEAD_GROUP:h * HEAD_GROUP + dim], k_ref[0, h].astype(BF16))
              for h in range(HEADS)]
    for h in range(HEADS):
        update(h, scores[h] - ck2[h:h + 1, :], v_ref[0, h].astype(BF16), True)

    @pl.when(j == nj - 1)
    def _():
        r = lax.broadcasted_iota(jnp.int32, (s_new, s_new), 0)
        c = lax.broadcasted_iota(jnp.int32, (s_new, s_new), 1)
        new_scores = [_nt_dot(fq_ref[0, :, h * HEAD_GROUP:(h + 1) * HEAD_GROUP],
                              fkn_ref[0, :, h * HEAD_GROUP:(h + 1) * HEAD_GROUP]) for h in range(HEADS)]
        for h in range(HEADS):
            update(h, jnp.where(c <= r, new_scores[h], NEG_INF), fvn_ref[0, :, h * dim:(h + 1) * dim], False)
        o_ref[0] = jnp.concatenate([acc_scr[h] / l_scr[h][:, :dim] for h in range(HEADS)],
                                   axis=1).astype(o_ref.dtype)


def _sample_fox_call(fq, cache_k, cache_v, cum_cache, fka_new, fv_new, tile):
    B, S, _ = fq.shape
    _, heads, dim, past = cache_k.shape
    assert past % tile == 0 and heads == HEADS
    w = heads * dim
    cache_spec = pl.BlockSpec((1, heads, dim, tile), lambda b, j: (b, 0, 0, j))
    return pl.pallas_call(
        _sample_fox_kernel,
        out_shape=jax.ShapeDtypeStruct((B, S, w), BF16),
        grid=(B, past // tile),
        in_specs=[pl.BlockSpec((1, S, _WIDE), lambda b, j: (b, 0, 0)),
                  cache_spec, cache_spec,
                  pl.BlockSpec((1, HEADS, tile), lambda b, j: (b, 0, j)),
                  pl.BlockSpec((1, S, _WIDE), lambda b, j: (b, 0, 0)),
                  pl.BlockSpec((1, S, w), lambda b, j: (b, 0, 0))],
        out_specs=pl.BlockSpec((1, S, w), lambda b, j: (b, 0, 0)),
        scratch_shapes=[pltpu.VMEM((HEADS, S, LANE), F32),
                        pltpu.VMEM((HEADS, S, LANE), F32),
                        pltpu.VMEM((HEADS, S, dim), F32)],
        compiler_params=pltpu.CompilerParams(
            dimension_semantics=("parallel", "arbitrary"), vmem_limit_bytes=V7X_VMEM_LIMIT),
        name="sample_fox",
    )(fq, cache_k, cache_v, cum_cache, fka_new, fv_new)


def _sample_mla_kernel(q_ref, lat_ref, kr_ref, latn_ref, krn_ref, wabs_ref, prope_ref, wv_ref, o_ref,
                       ql_scr, qr_scr, m_scr, l_scr, acc_scr):
    j = pl.program_id(1)
    nj = pl.num_programs(1)
    s_new = q_ref.shape[1]

    @pl.when(j == 0)
    def _():
        m_scr[...] = jnp.full(m_scr.shape, NEG_INF, F32)
        l_scr[...] = jnp.zeros(l_scr.shape, F32)
        acc_scr[...] = jnp.zeros(acc_scr.shape, F32)
        for h in range(HEADS):
            q_h = q_ref[0, :, h * HEAD_GROUP:(h + 1) * HEAD_GROUP]
            ql_scr[h * s_new:(h + 1) * s_new, :] = _dot(q_h, wabs_ref[h]).astype(BF16)
            qr_scr[h * s_new:(h + 1) * s_new, :] = _dot(q_h, prope_ref[...]).astype(BF16)

    def update(lat, rope_scores):
        s = _nt_dot(ql_scr[...], lat) + rope_scores
        m_prev = m_scr[...]
        m_new = jnp.maximum(m_prev, jnp.max(s, axis=1, keepdims=True))
        alpha = jnp.exp2(m_prev - m_new)
        p = jnp.exp2(s - m_new[:, :1])
        l_scr[...] = alpha * l_scr[...] + jnp.sum(p, axis=1, keepdims=True)
        acc_scr[...] = jnp.tile(alpha, (1, acc_scr.shape[1] // LANE)) * acc_scr[...] + _dot(p.astype(BF16), lat)
        m_scr[...] = m_new

    q_rope = qr_scr[:, :MLA_ROPE]
    update(lat_ref[0].astype(BF16), _dot(q_rope, kr_ref[0].astype(BF16)))

    @pl.when(j == nj - 1)
    def _():
        update(latn_ref[0].astype(BF16), _nt_dot(q_rope, krn_ref[0].astype(BF16)))
        o_lat = (acc_scr[...] / jnp.tile(l_scr[...], (1, acc_scr.shape[1] // LANE))).astype(BF16)
        out = _dot(o_lat[0:s_new], wv_ref[0])
        for h in range(1, HEADS):
            out = out + _dot(o_lat[h * s_new:(h + 1) * s_new], wv_ref[h])
        o_ref[0] = out.astype(o_ref.dtype)


def _sample_mla_call(q, cache_lat, cache_kr, lat_new, kr_new, w, tile):
    B, S, _ = q.shape
    past = cache_lat.shape[1]
    assert past % tile == 0
    c = cache_lat.shape[-1]
    ow = MLA_HEADS * MLA_V
    return pl.pallas_call(
        _sample_mla_kernel,
        out_shape=jax.ShapeDtypeStruct((B, S, ow), BF16),
        grid=(B, past // tile),
        in_specs=[pl.BlockSpec((1, S, _WIDE), lambda b, j: (b, 0, 0)),
                  pl.BlockSpec((1, tile, c), lambda b, j: (b, j, 0)),
                  pl.BlockSpec((1, MLA_ROPE, tile), lambda b, j: (b, 0, j)),
                  pl.BlockSpec((1, S, c), lambda b, j: (b, 0, 0)),
                  pl.BlockSpec((1, S, MLA_ROPE), lambda b, j: (b, 0, 0)),
                  _const_spec((HEADS, HEAD_GROUP, c)), _const_spec((HEAD_GROUP, LANE)),
                  _const_spec((HEADS, c, ow))],
        out_specs=pl.BlockSpec((1, S, ow), lambda b, j: (b, 0, 0)),
        scratch_shapes=[pltpu.VMEM((HEADS * S, c), BF16),
                        pltpu.VMEM((HEADS * S, LANE), BF16),
                        pltpu.VMEM((HEADS * S, LANE), F32),
                        pltpu.VMEM((HEADS * S, LANE), F32),
                        pltpu.VMEM((HEADS * S, c), F32)],
        compiler_params=pltpu.CompilerParams(
            dimension_semantics=("parallel", "arbitrary"), vmem_limit_bytes=V7X_VMEM_LIMIT),
        name="sample_mla",
    )(q, cache_lat, cache_kr, lat_new, kr_new, w["w_abs"], w["p_rope"], w["w_v_wide"])


def _gather_cols(w, src):
    src = np.asarray(src)
    pieces, start = [], 0
    for i in range(1, len(src) + 1):
        run_ends = i == len(src) or (src[i] != src[i - 1] + 1 if src[i - 1] >= 0 else src[i] >= 0)
        if run_ends:
            if src[start] >= 0:
                pieces.append(w[:, int(src[start]):int(src[start]) + i - start])
            else:
                pieces.append(jnp.zeros((w.shape[0], i - start), w.dtype))
            start = i
    return jnp.concatenate(pieces, axis=1)


def _prep_weights(norm_mix, w_in, b_forget, mla_q_norm, w_mla_uq, mla_kv_norm, w_mla_ukv, w_out,
                  norm_ffn, w_ffn_gate, w_ffn_up, w_ffn_down, norm_final):
    d_model = w_in.shape[0]
    o_kr = _MLA_Q_LORA + _MLA_KV_LORA
    o_fq = o_kr + MLA_ROPE
    o_fk = o_fq + _FOX_WIDTH
    o_fv = o_fk + _FOX_WIDTH
    o_fl = o_fv + _FOX_WIDTH
    half = MLA_ROPE // 2

    src = -np.ones((_NPROJ,), np.int64)
    src[_O_CQ:_O_CKV] = np.arange(0, _MLA_Q_LORA)
    src[_O_CKV:_O_FQ] = np.arange(_MLA_Q_LORA, o_kr)
    src[_O_FQ:_O_FK] = np.arange(o_fq, o_fk)
    src[_O_FK:_O_FV] = np.arange(o_fk, o_fv)
    src[_O_FV:_O_MISC] = np.arange(o_fv, o_fl)
    src[_O_MISC:_O_MISC + MLA_ROPE] = np.arange(o_kr, o_fq)
    src[_O_MISC + _LF_LANE:_O_MISC + _LF_LANE + FOX_HEADS] = np.arange(o_fl, o_fl + FOX_HEADS)
    w_in_r = _gather_cols(w_in, src).astype(BF16)

    src_a = -np.ones((_WIDE,), np.int64)
    for h in range(MLA_HEADS):
        src_a[h * HEAD_GROUP + np.arange(MLA_QK_DIM)] = h * MLA_QK_DIM + np.arange(MLA_QK_DIM)
    w_uq2 = _gather_cols(w_mla_uq, src_a).astype(BF16)

    src_k = -np.ones((_WIDE,), np.int64)
    src_v = np.zeros((MLA_HEADS * MLA_V,), np.int64)
    for h in range(MLA_HEADS):
        src_k[h * HEAD_GROUP + np.arange(MLA_NOPE)] = h * (MLA_NOPE + MLA_V) + np.arange(MLA_NOPE)
        src_v[h * MLA_V + np.arange(MLA_V)] = h * (MLA_NOPE + MLA_V) + MLA_NOPE + np.arange(MLA_V)
    place_r = np.zeros((LANE, _WIDE), np.float32)
    for h in range(MLA_HEADS):
        place_r[np.arange(MLA_ROPE), h * HEAD_GROUP + MLA_NOPE + np.arange(MLA_ROPE)] = 1.0
    w_uk_wide = _gather_cols(w_mla_ukv, src_k)
    w_kexp = jnp.concatenate([w_uk_wide, jnp.asarray(place_r)], axis=0).astype(BF16)
    w_uv = _gather_cols(w_mla_ukv, src_v).astype(BF16)

    w_uk = w_mla_ukv.reshape(_MLA_KV_LORA, MLA_HEADS, MLA_NOPE + MLA_V)[:, :, :MLA_NOPE]
    w_abs = jnp.zeros((MLA_HEADS, HEAD_GROUP, _MLA_KV_LORA), F32)
    w_abs = w_abs.at[:, :MLA_NOPE, :].set(jnp.transpose(w_uk, (1, 2, 0))).astype(BF16)
    p_rope = np.zeros((HEAD_GROUP, LANE), np.float32)
    p_rope[MLA_NOPE + np.arange(MLA_ROPE), np.arange(MLA_ROPE)] = 1.0
    w_uv_h = w_mla_ukv.reshape(_MLA_KV_LORA, MLA_HEADS, MLA_NOPE + MLA_V)[:, :, MLA_NOPE:]
    eye = jnp.asarray(np.eye(MLA_HEADS, dtype=np.float32))
    w_v_wide = (jnp.transpose(w_uv_h, (1, 0, 2))[:, :, None, :] * eye[:, None, :, None])
    w_v_wide = w_v_wide.reshape(MLA_HEADS, _MLA_KV_LORA, MLA_HEADS * MLA_V).astype(BF16)

    bias_f = jnp.zeros((1, LANE), F32).at[0, _LF_LANE:_LF_LANE + FOX_HEADS].set(b_forget.astype(F32))
    place_k = np.zeros((3 * LANE, _WIDE), np.float32)
    ones_q = np.zeros((1, _WIDE), np.float32)
    for h in range(FOX_HEADS):
        for part in range(3):
            place_k[part * LANE + _LF_LANE + h, h * HEAD_GROUP + FOX_HEAD_DIM + part] = -1.0
            ones_q[0, h * HEAD_GROUP + FOX_HEAD_DIM + part] = 1.0

    d_mla = MLA_HEADS * MLA_V
    return {
        "g_mix": norm_mix.reshape(1, d_model).astype(F32),
        "w_in": w_in_r,
        "bias_f": bias_f,
        "q_norm": mla_q_norm.reshape(1, -1).astype(F32),
        "w_uq2": w_uq2,
        "kv_norm": mla_kv_norm.reshape(1, -1).astype(F32),
        "w_kexp": w_kexp,
        "w_uv": w_uv,
        "place_k": jnp.asarray(place_k, BF16),
        "ones_q": jnp.asarray(ones_q),
        "w_abs": w_abs,
        "p_rope": jnp.asarray(p_rope, BF16),
        "w_v_wide": w_v_wide,
        "w_out_a": w_out[:d_mla].astype(BF16),
        "w_out_b": w_out[d_mla:].astype(BF16),
        "g_ffn": norm_ffn.reshape(1, d_model).astype(F32),
        "w_gate": w_ffn_gate.astype(BF16),
        "w_up": w_ffn_up.astype(BF16),
        "w_down": w_ffn_down.astype(BF16),
        "g_final": norm_final.reshape(1, d_model).astype(F32),
    }


def _rope_tables(pos):
    half = MLA_ROPE // 2
    inv_freq = ROPE_THETA ** (-jnp.arange(half, dtype=F32) / half)
    ang = pos.astype(F32)[:, None] * inv_freq[None, :]
    cos, sin = jnp.cos(ang), jnp.sin(ang)
    n = pos.shape[0]
    sc = MLA_SCALE * LOG2E
    zeros = lambda w: jnp.zeros((n, w), F32)
    tqc = jnp.concatenate([jnp.full((n, MLA_NOPE), sc, F32), sc * cos, sc * cos,
                           zeros(LANE - MLA_QK_DIM)], axis=1)
    tqs = jnp.concatenate([zeros(MLA_NOPE), -sc * sin, sc * sin, zeros(LANE - MLA_QK_DIM)], axis=1)
    tkc = jnp.concatenate([cos, cos, zeros(LANE - MLA_ROPE)], axis=1)
    tks = jnp.concatenate([-sin, sin, zeros(LANE - MLA_ROPE)], axis=1)
    return tqc, tqs, tkc, tks


def _pad_rows(a, rows):
    return jnp.concatenate([a, jnp.zeros((rows - a.shape[0],) + a.shape[1:], a.dtype)], axis=0)


PROJ_TILE = 256
ATTN_TQ = 1024
ATTN_TK = 512
FFN_TILE = 256
SAMPLE_TILE = 1024


def kernel(x_prompt, x_sample, cache_mla_latent, cache_mla_krope, cache_fox_k, cache_fox_v, cache_fox_logf, meta_tokens, norm_mix, w_in, b_forget, mla_q_norm, w_mla_uq, mla_kv_norm, w_mla_ukv, w_out, norm_ffn, w_ffn_gate, w_ffn_up, w_ffn_down, norm_final):
    depth = w_in.shape[0]
    assert depth == 1, "single-layer trunk: the meta rows' mixing output never reaches an output"
    B, seq, d_model = x_prompt.shape
    SB, s_new, _ = x_sample.shape
    n_meta = meta_tokens.shape[0]
    past = cache_mla_latent.shape[2]
    assert n_meta <= LANE

    w = _prep_weights(norm_mix[0], w_in[0], b_forget[0], mla_q_norm[0], w_mla_uq[0], mla_kv_norm[0],
                      w_mla_ukv[0], w_out[0], norm_ffn[0], w_ffn_gate[0], w_ffn_up[0], w_ffn_down[0],
                      norm_final)

    zero_carry = jnp.zeros((1, 1, LANE), F32)
    meta = _proj_call(_pad_rows(meta_tokens.astype(F32), LANE)[None], zero_carry,
                      _rope_tables(jnp.arange(LANE)), w, LANE, valid_rows=n_meta, transposed_v=True)
    (m_lat, m_kr, m_fk, m_fv, m_lf, _, m_km, m_vm, _, m_fka, m_fvb, m_carry) = meta
    frames = _proj_call(x_prompt, jnp.broadcast_to(m_carry, (B, 1, LANE)),
                        _rope_tables(n_meta + jnp.arange(seq)), w, PROJ_TILE, transposed_v=True,
                        row_offset=n_meta)
    f_qm, f_km, f_vm, f_fq, f_fka, f_fvb = frames[5:11]
    lat_p, kr_p, fk_p, fv_p, lf_p = _place_rows_call(
        [a[0, :n_meta] for a in (m_lat, m_kr, m_fk, m_fv, m_lf)],
        [a.reshape((B, n_meta + seq) + a.shape[1:]) for a in frames[:5]])

    o_mla = _attn_call(f_qm, f_km, f_vm, m_km[0], m_vm[0],
                       chunk_mask=True, n_meta=n_meta, tq=ATTN_TQ, tk=ATTN_TK, name="attn_mla")
    o_fox = _attn_call(f_fq, f_fka, f_fvb, m_fka[0], m_fvb[0],
                       chunk_mask=False, n_meta=n_meta, tq=ATTN_TQ, tk=ATTN_TK, name="attn_fox")
    y_prompt = _ffn_call(x_prompt.reshape(B * seq, d_model), o_mla.reshape(B * seq, -1),
                         o_fox.reshape(B * seq, -1), w, FFN_TILE).reshape(B, seq, d_model)

    lat_p, kr_p, lf_p = lat_p[None], kr_p[None], lf_p[None]
    fk_p = fk_p.reshape(1, B, n_meta + seq, FOX_HEADS, FOX_HEAD_DIM)
    fv_p = fv_p.reshape(1, B, n_meta + seq, FOX_HEADS, FOX_HEAD_DIM)

    logf_t = jnp.transpose(cache_fox_logf[0].astype(F32), (0, 2, 1))
    cum_cache = _cumsum_call(logf_t.reshape(SB * FOX_HEADS, past)).reshape(SB, FOX_HEADS, past)
    carry_s = jnp.zeros((SB, 1, LANE), F32).at[:, 0, _LF_LANE:_LF_LANE + FOX_HEADS].set(
        cum_cache[:, :, past - 1])
    samp = _proj_call(x_sample, carry_s, _rope_tables(past + jnp.arange(s_new)), w, s_new)
    (s_lat, s_kr, s_fk, s_fv, s_lf, s_qm, _, _, s_fq, s_fka, s_fvb, _) = samp

    positions_last = lambda a: jnp.moveaxis(a, 1, -1)
    so_mla = _sample_mla_call(s_qm, cache_mla_latent[0], positions_last(cache_mla_krope[0]), s_lat, s_kr, w,
                              SAMPLE_TILE)
    so_fox = _sample_fox_call(s_fq, positions_last(cache_fox_k[0]), positions_last(cache_fox_v[0]), cum_cache,
                              s_fka, s_fvb, SAMPLE_TILE)
    y_sample = _ffn_call(x_sample.reshape(SB * s_new, d_model), so_mla.reshape(SB * s_new, -1),
                         so_fox.reshape(SB * s_new, -1), w, FFN_TILE).reshape(SB, s_new, d_model)

    heads = lambda a: a.reshape(1, SB, s_new, FOX_HEADS, FOX_HEAD_DIM)
    return (y_prompt, y_sample, lat_p, kr_p, fk_p, fv_p, lf_p,
            s_lat[None], s_kr[None], heads(s_fk), heads(s_fv), s_lf[None])
```

```python
import functools

import numpy as np
import jax
import jax.numpy as jnp
from jax import lax
from jax.experimental import pallas as pl
from jax.experimental.pallas import tpu as pltpu

CHUNK = 64
EPS = 1e-6
NEG_INF = -1e30
MLA_HEADS = 8
MLA_NOPE = 64
MLA_ROPE = 32
MLA_V = 64
MLA_QK_DIM = MLA_NOPE + MLA_ROPE
MLA_SCALE = MLA_QK_DIM ** -0.5
ROPE_THETA = 10000.0
FOX_HEADS = 8
FOX_HEAD_DIM = 64
FOX_SCALE = FOX_HEAD_DIM ** -0.5
LOG2E = 1.4426950408889634

LANE = 128
SUBLANE = 8
BF16_SUBLANES = 16
HEAD_GROUP = LANE
HEADS = 8
V7X_VMEM_LIMIT = 56 * 1024 * 1024

BF16 = jnp.bfloat16
F32 = jnp.float32

_MLA_Q_LORA = 384
_MLA_KV_LORA = 256
_FOX_WIDTH = FOX_HEADS * FOX_HEAD_DIM
_WIDE = HEADS * HEAD_GROUP
_O_CQ = 0
_O_CKV = _O_CQ + _MLA_Q_LORA
_O_FQ = _O_CKV + _MLA_KV_LORA
_O_FK = _O_FQ + _FOX_WIDTH
_O_FV = _O_FK + _FOX_WIDTH
_O_MISC = _O_FV + _FOX_WIDTH
_NPROJ = _O_MISC + LANE
_LF_LANE = MLA_ROPE


def _nt_dot(a, b):
    return lax.dot_general(a, b, (((1,), (1,)), ((), ())), preferred_element_type=F32)


def _dot(a, b):
    return jnp.dot(a, b, preferred_element_type=F32)


def _rms(x, g):
    return x * lax.rsqrt(jnp.mean(x * x, axis=-1, keepdims=True) + EPS) * g


def _split3(x):
    hi = x.astype(BF16)
    r = x - hi.astype(F32)
    mid = r.astype(BF16)
    lo = (r - mid.astype(F32)).astype(BF16)
    return hi, mid, lo


def _proj_kernel(*refs, n_base, **static):
    _proj_body(*refs[n_base:], **static)


def _proj_body(x_ref, carry_in_ref, tqc_ref, tqs_ref, tkc_ref, tks_ref,
               g_ref, w_in_ref, bias_ref, qn_ref, w_uq_ref, kvn_ref, w_kexp_ref, w_uv_ref,
               tri_ref, place_ref, ones_q_ref,
               lat_ref, kr_ref, fk_ref, fv_ref, lf_ref,
               qm_ref, km_ref, vm_ref, fq_ref, fka_ref, fvb_ref, carry_out_ref,
               carry_scr, *, valid_rows, transposed_v):
    t = pl.program_id(1)

    @pl.when(t == 0)
    def _():
        carry_scr[...] = carry_in_ref[0]

    x = x_ref[0]
    h = _rms(x, g_ref[...]).astype(BF16)
    proj = _dot(h, w_in_ref[...])

    lane = lax.broadcasted_iota(jnp.int32, (x.shape[0], LANE), 1)
    half = MLA_ROPE // 2

    def rotary(a, first, cos_tab, sin_tab):
        partner = jnp.where(lane < first + half, pltpu.roll(a, LANE - half, 1), pltpu.roll(a, half, 1))
        return a * cos_tab + partner * sin_tab

    def put(ref, val):
        ref[...] = val.reshape(ref.shape)

    cqn = _rms(proj[:, _O_CQ:_O_CKV], qn_ref[...]).astype(BF16)
    q2 = _dot(cqn, w_uq_ref[...])
    tqc, tqs = tqc_ref[...], tqs_ref[...]
    qm_ref[0] = jnp.concatenate(
        [rotary(q2[:, h * HEAD_GROUP:(h + 1) * HEAD_GROUP], MLA_NOPE, tqc, tqs) for h in range(HEADS)],
        axis=1).astype(BF16)

    latent = _rms(proj[:, _O_CKV:_O_FQ], kvn_ref[...])
    put(lat_ref, latent)
    misc = proj[:, _O_MISC:_NPROJ]
    kr = rotary(misc, 0, tkc_ref[...], tks_ref[...])
    put(kr_ref, kr[:, :MLA_ROPE])
    lat_bf = latent.astype(BF16)
    kcat = jnp.concatenate([lat_bf, kr.astype(BF16)], axis=1)
    km_ref[0] = _dot(kcat, w_kexp_ref[...]).astype(BF16)
    v_mla = _dot(lat_bf, w_uv_ref[...])
    vm_ref[0] = (v_mla.T if transposed_v else v_mla).astype(BF16)

    fq = proj[:, _O_FQ:_O_FK]
    fk = proj[:, _O_FK:_O_FV]
    fv = proj[:, _O_FV:_O_MISC]
    fvb_ref[0] = (fv.T if transposed_v else fv).astype(BF16)

    def head_groups(a):
        low = lane < FOX_HEAD_DIM
        groups = []
        for p in range(FOX_HEADS // 2):
            pair = a[:, p * LANE:(p + 1) * LANE]
            groups.append(jnp.where(low, pair, 0.0))
            groups.append(jnp.where(low, pltpu.roll(pair, LANE - FOX_HEAD_DIM, 1), 0.0))
        return groups

    put(fk_ref, fk)
    put(fv_ref, fv)

    z = misc + bias_ref[...]
    keep = (lane >= _LF_LANE) & (lane < _LF_LANE + FOX_HEADS)
    if valid_rows is not None:
        keep = keep & (lax.broadcasted_iota(jnp.int32, z.shape, 0) < valid_rows)
    lf = jnp.where(keep, jnp.minimum(z, 0.0) - jnp.log1p(jnp.exp(-jnp.abs(z))), 0.0)
    put(lf_ref, lf[:, _LF_LANE:_LF_LANE + FOX_HEADS])
    hi, mid, lo = _split3(lf)
    tri = tri_ref[...]
    cum = _dot(tri, hi) + _dot(tri, mid) + _dot(tri, lo) + carry_scr[...]
    n_rows = cum.shape[0]
    carry_scr[...] = cum[n_rows - 1:n_rows, :]
    carry_out_ref[0] = cum[n_rows - 1:n_rows, :]
    c_hi, c_mid, c_lo = _split3(cum * LOG2E)
    kb = _dot(jnp.concatenate([c_hi, c_mid, c_lo], axis=1), place_ref[...])
    fq_wide = jnp.concatenate(head_groups(fq), axis=1)
    fq_ref[0] = (fq_wide * (FOX_SCALE * LOG2E) + ones_q_ref[...]).astype(BF16)
    fka_ref[0] = (jnp.concatenate(head_groups(fk), axis=1) + kb).astype(BF16)


def _const_spec(shape):
    zeros = (0,) * len(shape)
    return pl.BlockSpec(shape, lambda *_: zeros, pipeline_mode=pl.Buffered(1))


def _proj_call(x, carry_in, tabs, w, tile, valid_rows=None, transposed_v=False, row_offset=0):
    B, L, D = x.shape
    assert L % tile == 0
    nt = L // tile
    tri = jnp.asarray(np.tril(np.ones((tile, tile), np.float32)), BF16)

    row = lambda width: pl.BlockSpec((1, tile, width), lambda b, t: (b, t, 0))
    tab = pl.BlockSpec((tile, LANE), lambda b, t: (t, 0))
    in_specs = [row(D), pl.BlockSpec((1, 1, LANE), lambda b, t: (b, 0, 0)), tab, tab, tab, tab,
                _const_spec((1, D)), _const_spec((D, _NPROJ)), _const_spec((1, LANE)),
                _const_spec((1, _MLA_Q_LORA)), _const_spec((_MLA_Q_LORA, _WIDE)),
                _const_spec((1, _MLA_KV_LORA)), _const_spec((_MLA_KV_LORA + LANE, _WIDE)),
                _const_spec((_MLA_KV_LORA, MLA_HEADS * MLA_V)),
                _const_spec((tile, tile)), _const_spec((3 * LANE, _WIDE)), _const_spec((1, _WIDE))]
    out_shapes = [
        jax.ShapeDtypeStruct((B, L, _MLA_KV_LORA), F32),
        jax.ShapeDtypeStruct((B, L, MLA_ROPE), F32),
        jax.ShapeDtypeStruct((B, L, _FOX_WIDTH), F32),
        jax.ShapeDtypeStruct((B, L, _FOX_WIDTH), F32),
        jax.ShapeDtypeStruct((B, L, FOX_HEADS), F32),
        jax.ShapeDtypeStruct((B, L, _WIDE), BF16),
        jax.ShapeDtypeStruct((B, L, _WIDE), BF16),
        jax.ShapeDtypeStruct((B, L, MLA_HEADS * MLA_V), BF16),
        jax.ShapeDtypeStruct((B, L, _WIDE), BF16),
        jax.ShapeDtypeStruct((B, L, _WIDE), BF16),
        jax.ShapeDtypeStruct((B, L, _FOX_WIDTH), BF16),
        jax.ShapeDtypeStruct((B, 1, LANE), F32),
    ]
    out_specs = [row(s.shape[-1]) for s in out_shapes[:-1]]
    out_specs.append(pl.BlockSpec((1, 1, LANE), lambda b, t: (b, 0, 0)))
    if transposed_v:
        for idx in (7, 10):
            vw = out_shapes[idx].shape[-1]
            out_shapes[idx] = jax.ShapeDtypeStruct((B, vw, L), BF16)
            out_specs[idx] = pl.BlockSpec((1, vw, tile), lambda b, t: (b, 0, t))
    if row_offset:
        assert row_offset % SUBLANE == 0 and L % SUBLANE == 0
        for idx in range(5):
            width = out_shapes[idx].shape[-1]
            out_shapes[idx] = jax.ShapeDtypeStruct((B * (row_offset + L), width), F32)
            out_specs[idx] = pl.BlockSpec(
                (pl.Element(tile), pl.Element(width)),
                lambda b, t: (pl.multiple_of(b * (row_offset + L) + row_offset + t * tile, SUBLANE), 0))
    bases = [jnp.zeros(s.shape, s.dtype) for s in out_shapes[:5]] if row_offset else []
    n_base = len(bases)
    return pl.pallas_call(
        functools.partial(_proj_kernel, n_base=n_base, valid_rows=valid_rows, transposed_v=transposed_v),
        out_shape=out_shapes,
        grid=(B, nt),
        in_specs=[pl.BlockSpec(memory_space=pl.ANY)] * n_base + in_specs,
        out_specs=out_specs,
        input_output_aliases={i: i for i in range(n_base)},
        scratch_shapes=[pltpu.VMEM((1, LANE), F32)],
        compiler_params=pltpu.CompilerParams(
            dimension_semantics=("parallel", "arbitrary"), vmem_limit_bytes=V7X_VMEM_LIMIT),
        name="proj",
    )(*bases, x, carry_in, *tabs, w["g_mix"], w["w_in"], w["bias_f"], w["q_norm"], w["w_uq2"],
      w["kv_norm"], w["w_kexp"], w["w_uv"], tri, w["place_k"], w["ones_q"])


def _place_rows_kernel(*refs):
    n = len(refs) // 3
    for src, out in zip(refs[:n], refs[2 * n:]):
        out[0] = src[...]


def _place_rows_call(rows, dests):
    n = len(rows)
    B = dests[0].shape[0]
    in_specs = [pl.BlockSpec(r.shape, lambda b, nd=r.ndim: (0,) * nd) for r in rows]
    in_specs += [pl.BlockSpec(memory_space=pl.ANY) for _ in dests]
    out_specs = [pl.BlockSpec((1,) + r.shape, lambda b, nd=r.ndim: (b,) + (0,) * nd) for r in rows]
    return pl.pallas_call(
        _place_rows_kernel,
        out_shape=[jax.ShapeDtypeStruct(d.shape, d.dtype) for d in dests],
        grid=(B,),
        in_specs=in_specs,
        out_specs=out_specs,
        input_output_aliases={n + i: i for i in range(n)},
        compiler_params=pltpu.CompilerParams(dimension_semantics=("arbitrary",)),
        name="place_meta_rows",
    )(*rows, *dests)


def _softmax_update(h, s, v_pair, m_scr, l_scr, acc_scr):
    m_prev = m_scr[h]
    m_new = jnp.maximum(m_prev, jnp.max(s, axis=1, keepdims=True))
    alpha = jnp.exp2(m_prev - m_new)
    p = jnp.exp2(s - jnp.tile(m_new, (1, s.shape[1] // LANE)))
    l_scr[h] = alpha * l_scr[h] + jnp.sum(p, axis=1, keepdims=True)
    acc_scr[h] = alpha * acc_scr[h] + _dot(p.astype(BF16), v_pair)
    m_scr[h] = m_new


QK_LOOKAHEAD = 8
UNIT_Q = 256
UNIT_K = 256
ONES_ROWS = 16


def _with_ones_rows(v_t):
    row = lax.broadcasted_iota(jnp.int32, (ONES_ROWS, v_t.shape[1]), 0)
    return jnp.concatenate([v_t, jnp.where(row == 0, 1.0, 0.0).astype(v_t.dtype)], axis=0)


def _softmax_update_t(h, qs, s_t, v_t, m_scr, acc_scr):
    uk, uq = s_t.shape
    s3 = s_t.reshape(uk // SUBLANE, SUBLANE, uq)
    m_cur = jnp.max(s3, axis=0)
    for shift in (4, 2, 1):
        m_cur = jnp.maximum(m_cur, pltpu.roll(m_cur, shift, 0))
    m_prev = m_scr[h, :, qs]
    m_new = jnp.maximum(m_prev, m_cur)
    alpha = jnp.exp2(m_prev - m_new)
    p_t = jnp.exp2(s3 - m_new[None]).reshape(uk, uq)
    pv = _dot(v_t, p_t.astype(BF16))
    rows = pv.shape[0]
    acc = acc_scr[h, :, qs].reshape(rows // SUBLANE, SUBLANE, uq)
    acc_scr[h, :, qs] = (alpha[None] * acc).reshape(rows, uq) + pv
    m_scr[h, :, qs] = m_new


def _attn_kernel(qi_ref, kj_ref, q_ref, k_ref, vt_ref, kmeta_ref, vtmeta_ref, o_ref,
                 m_scr, acc_scr, *, chunk_mask, n_meta):
    p_id = pl.program_id(1)
    i = qi_ref[p_id]
    j = kj_ref[p_id]
    tq = q_ref.shape[1]
    tk = k_ref.shape[1]
    dv = acc_scr.shape[1] - ONES_ROWS

    def head(ref2d, h):
        return ref2d[:, h * HEAD_GROUP:(h + 1) * HEAD_GROUP]

    @pl.when(j == 0)
    def _():
        n_pad = kmeta_ref.shape[0]
        n_rows = -(-n_meta // BF16_SUBLANES) * BF16_SUBLANES
        row = lax.broadcasted_iota(jnp.int32, (n_rows, tq), 0)
        scores = [_nt_dot(kmeta_ref[:n_rows, h * HEAD_GROUP:(h + 1) * HEAD_GROUP], head(q_ref.at[0], h))
                  for h in range(HEADS)]
        for h in range(HEADS):
            s3 = jnp.where(row < n_meta, scores[h], NEG_INF).reshape(n_rows // SUBLANE, SUBLANE, tq)
            m_new = jnp.max(s3, axis=0)
            for shift in (4, 2, 1):
                m_new = jnp.maximum(m_new, pltpu.roll(m_new, shift, 0))
            p_t = jnp.exp2(s3 - m_new[None]).reshape(n_rows, tq).astype(BF16)
            p_t = jnp.concatenate([p_t, jnp.zeros((n_pad - n_rows, tq), BF16)], axis=0)
            acc_scr[h] = _dot(_with_ones_rows(vtmeta_ref[h * dv:(h + 1) * dv, :]), p_t)
            m_scr[h] = m_new

    uq = min(UNIT_Q, tq)
    uk = min(UNIT_K, tk)

    def block(key_shift):
        masked = key_shift is not None
        units = [(h, q0, k0) for h in range(HEADS) for k0 in range(0, tk, uk) for q0 in range(0, tq, uq)]
        if masked:
            units = [u for u in units if key_shift + u[2] < u[1] + uq]
            kk = lax.broadcasted_iota(jnp.int32, (uk, uq), 0) + key_shift
            qq = lax.broadcasted_iota(jnp.int32, (uk, uq), 1)

        def qk(u):
            h, q0, k0 = u
            return _nt_dot(k_ref[0, k0:k0 + uk, h * HEAD_GROUP:(h + 1) * HEAD_GROUP],
                           q_ref[0, q0:q0 + uq, h * HEAD_GROUP:(h + 1) * HEAD_GROUP])

        scores = {}
        for n in range(-QK_LOOKAHEAD, len(units)):
            if n + QK_LOOKAHEAD < len(units):
                scores[n + QK_LOOKAHEAD] = qk(units[n + QK_LOOKAHEAD])
            if n >= 0:
                h, q0, k0 = units[n]
                s_t = scores.pop(n)
                if masked and key_shift + k0 + uk > q0 + 1:
                    if chunk_mask:
                        valid = (kk + k0) // CHUNK <= (qq + q0) // CHUNK
                    else:
                        valid = kk + k0 <= qq + q0
                    s_t = jnp.where(valid, s_t, NEG_INF)
                v_t = _with_ones_rows(vt_ref[0, h * dv:(h + 1) * dv, k0:k0 + uk])
                _softmax_update_t(h, slice(q0, q0 + uq), s_t, v_t, m_scr, acc_scr)

    ratio = tq // tk
    d = j - ratio * i

    @pl.when(d < 0)
    def _():
        block(None)

    for dd in range(ratio):
        @pl.when(d == dd)
        def _(dd=dd):
            block(dd * tk)

    @pl.when(d == ratio - 1)
    def _():
        def normalised(h):
            return acc_scr[h, :dv, :] / acc_scr[h, dv:dv + 1, :]

        for hp in range(HEADS // 2):
            o_pair_t = jnp.concatenate([normalised(2 * hp), normalised(2 * hp + 1)], axis=0)
            o_ref[0, :, hp * 2 * dv:(hp + 1) * 2 * dv] = o_pair_t.T.astype(o_ref.dtype)


def _attn_call(q, k, v_t, k_meta, v_t_meta, *, chunk_mask, n_meta, tq, tk, name):
    B, F, _ = q.shape
    tq, tk = min(tq, F), min(tk, F)
    assert F % tq == 0 and tq % tk == 0 and tk % CHUNK == 0
    ratio = tq // tk
    pairs = [(i, j) for i in range(F // tq) for j in range(ratio * (i + 1))]
    qi = jnp.asarray([p[0] for p in pairs], jnp.int32)
    kj = jnp.asarray([p[1] for p in pairs], jnp.int32)
    vw = v_t.shape[1]
    grid_spec = pltpu.PrefetchScalarGridSpec(
        num_scalar_prefetch=2,
        grid=(B, len(pairs)),
        in_specs=[
            pl.BlockSpec((1, tq, _WIDE), lambda b, p, qi, kj: (b, qi[p], 0)),
            pl.BlockSpec((1, tk, _WIDE), lambda b, p, qi, kj: (b, kj[p], 0)),
            pl.BlockSpec((1, vw, tk), lambda b, p, qi, kj: (b, 0, kj[p])),
            pl.BlockSpec((LANE, _WIDE), lambda b, p, qi, kj: (0, 0)),
            pl.BlockSpec((vw, LANE), lambda b, p, qi, kj: (0, 0)),
        ],
        out_specs=pl.BlockSpec((1, tq, vw), lambda b, p, qi, kj: (b, qi[p], 0)),
        scratch_shapes=[pltpu.VMEM((HEADS, SUBLANE, tq), F32),
                        pltpu.VMEM((HEADS, vw // HEADS + ONES_ROWS, tq), F32)],
    )
    return pl.pallas_call(
        functools.partial(_attn_kernel, chunk_mask=chunk_mask, n_meta=n_meta),
        out_shape=jax.ShapeDtypeStruct((B, F, vw), BF16),
        grid_spec=grid_spec,
        compiler_params=pltpu.CompilerParams(
            dimension_semantics=("parallel", "arbitrary"), vmem_limit_bytes=V7X_VMEM_LIMIT),
        name=name,
    )(qi, kj, q, k, v_t, k_meta, v_t_meta)


def _ffn_kernel(x_ref, oa_ref, ob_ref, woa_ref, wob_ref, g_ref, wg_ref, wu_ref, wd_ref, gf_ref, y_ref):
    x1 = x_ref[...] + _dot(oa_ref[...], woa_ref[...]) + _dot(ob_ref[...], wob_ref[...])
    h = _rms(x1, g_ref[...]).astype(BF16)
    gate = _dot(h, wg_ref[...])
    up = _dot(h, wu_ref[...])
    act = (gate * jax.nn.sigmoid(gate) * up).astype(BF16)
    x2 = x1 + _dot(act, wd_ref[...])
    y_ref[...] = _rms(x2, gf_ref[...])


def _ffn_call(x, oa, ob, w, tile):
    R, D = x.shape
    tile = min(tile, R)
    assert R % tile == 0
    dm = oa.shape[-1]
    dff = w["w_gate"].shape[-1]
    row = lambda width: pl.BlockSpec((tile, width), lambda r: (r, 0))
    return pl.pallas_call(
        _ffn_kernel,
        out_shape=jax.ShapeDtypeStruct((R, D), F32),
        grid=(R // tile,),
        in_specs=[row(D), row(dm), row(dm),
                  _const_spec((dm, D)), _const_spec((dm, D)), _const_spec((1, D)),
                  _const_spec((D, dff)), _const_spec((D, dff)), _const_spec((dff, D)),
                  _const_spec((1, D))],
        out_specs=row(D),
        compiler_params=pltpu.CompilerParams(
            dimension_semantics=("parallel",), vmem_limit_bytes=V7X_VMEM_LIMIT),
        name="ffn",
    )(x, oa, ob, w["w_out_a"], w["w_out_b"], w["g_ffn"], w["w_gate"], w["w_up"], w["w_down"],
      w["g_final"])


_CUM_CHUNK = 256


def _cumsum_kernel(x_ref, u_ref, o_ref):
    rows, n = x_ref.shape
    u = u_ref[...]
    carry = jnp.zeros((rows, 1), F32)
    for c in range(n // _CUM_CHUNK):
        hi, mid, lo = _split3(x_ref[:, c * _CUM_CHUNK:(c + 1) * _CUM_CHUNK])
        y = _dot(hi, u) + _dot(mid, u) + _dot(lo, u) + carry
        o_ref[:, c * _CUM_CHUNK:(c + 1) * _CUM_CHUNK] = y
        carry = y[:, _CUM_CHUNK - 1:_CUM_CHUNK]


def _cumsum_call(x):
    rows, n = x.shape
    assert n % _CUM_CHUNK == 0
    u = jnp.asarray(np.triu(np.ones((_CUM_CHUNK, _CUM_CHUNK), np.float32)), BF16)
    return pl.pallas_call(
        _cumsum_kernel,
        out_shape=jax.ShapeDtypeStruct((rows, n), F32),
        compiler_params=pltpu.CompilerParams(vmem_limit_bytes=V7X_VMEM_LIMIT),
        name="cache_cumsum",
    )(x, u)


def _sample_fox_kernel(fq_ref, k_ref, v_ref, ck_ref, fkn_ref, fvn_ref, o_ref, m_scr, l_scr, acc_scr):
    j = pl.program_id(1)
    nj = pl.num_programs(1)
    s_new = fq_ref.shape[1]
    dim = acc_scr.shape[-1]

    @pl.when(j == 0)
    def _():
        m_scr[...] = jnp.full(m_scr.shape, NEG_INF, F32)
        l_scr[...] = jnp.zeros(l_scr.shape, F32)
        acc_scr[...] = jnp.zeros(acc_scr.shape, F32)

    def update(h, s, v_h, v_transposed):
        m_prev = m_scr[h]
        m_new = jnp.maximum(m_prev, jnp.max(s, axis=1, keepdims=True))
        alpha = jnp.exp2(m_prev - m_new)
        p = jnp.exp2(s - m_new[:, :1])
        l_scr[h] = alpha * l_scr[h] + jnp.sum(p, axis=1, keepdims=True)
        p = p.astype(BF16)
        pv = _nt_dot(p, v_h) if v_transposed else _dot(p, v_h)
        acc_scr[h] = alpha[:, :dim] * acc_scr[h] + pv
        m_scr[h] = m_new

    ck2 = ck_ref[0] * LOG2E
    scores = [_dot(fq_ref[0, :, h * HEAD_GROUP:h * HEAD_GROUP + dim], k_ref[0, h].astype(BF16))
              for h in range(HEADS)]
    for h in range(HEADS):
        update(h, scores[h] - ck2[h:h + 1, :], v_ref[0, h].astype(BF16), True)

    @pl.when(j == nj - 1)
    def _():
        r = lax.broadcasted_iota(jnp.int32, (s_new, s_new), 0)
        c = lax.broadcasted_iota(jnp.int32, (s_new, s_new), 1)
        new_scores = [_nt_dot(fq_ref[0, :, h * HEAD_GROUP:(h + 1) * HEAD_GROUP],
                              fkn_ref[0, :, h * HEAD_GROUP:(h + 1) * HEAD_GROUP]) for h in range(HEADS)]
        for h in range(HEADS):
            update(h, jnp.where(c <= r, new_scores[h], NEG_INF), fvn_ref[0, :, h * dim:(h + 1) * dim], False)
        o_ref[0] = jnp.concatenate([acc_scr[h] / l_scr[h][:, :dim] for h in range(HEADS)],
                                   axis=1).astype(o_ref.dtype)


def _sample_fox_call(fq, cache_k, cache_v, cum_cache, fka_new, fv_new, tile):
    B, S, _ = fq.shape
    _, heads, dim, past = cache_k.shape
    assert past % tile == 0 and heads == HEADS
    w = heads * dim
    cache_spec = pl.BlockSpec((1, heads, dim, tile), lambda b, j: (b, 0, 0, j))
    return pl.pallas_call(
        _sample_fox_kernel,
        out_shape=jax.ShapeDtypeStruct((B, S, w), BF16),
        grid=(B, past // tile),
        in_specs=[pl.BlockSpec((1, S, _WIDE), lambda b, j: (b, 0, 0)),
                  cache_spec, cache_spec,
                  pl.BlockSpec((1, HEADS, tile), lambda b, j: (b, 0, j)),
                  pl.BlockSpec((1, S, _WIDE), lambda b, j: (b, 0, 0)),
                  pl.BlockSpec((1, S, w), lambda b, j: (b, 0, 0))],
        out_specs=pl.BlockSpec((1, S, w), lambda b, j: (b, 0, 0)),
        scratch_shapes=[pltpu.VMEM((HEADS, S, LANE), F32),
                        pltpu.VMEM((HEADS, S, LANE), F32),
                        pltpu.VMEM((HEADS, S, dim), F32)],
        compiler_params=pltpu.CompilerParams(
            dimension_semantics=("parallel", "arbitrary"), vmem_limit_bytes=V7X_VMEM_LIMIT),
        name="sample_fox",
    )(fq, cache_k, cache_v, cum_cache, fka_new, fv_new)


def _sample_mla_kernel(q_ref, lat_ref, kr_ref, latn_ref, krn_ref, wabs_ref, prope_ref, wv_ref, o_ref,
                       ql_scr, qr_scr, m_scr, l_scr, acc_scr):
    j = pl.program_id(1)
    nj = pl.num_programs(1)
    s_new = q_ref.shape[1]

    @pl.when(j == 0)
    def _():
        m_scr[...] = jnp.full(m_scr.shape, NEG_INF, F32)
        l_scr[...] = jnp.zeros(l_scr.shape, F32)
        acc_scr[...] = jnp.zeros(acc_scr.shape, F32)
        for h in range(HEADS):
            q_h = q_ref[0, :, h * HEAD_GROUP:(h + 1) * HEAD_GROUP]
            ql_scr[h * s_new:(h + 1) * s_new, :] = _dot(q_h, wabs_ref[h]).astype(BF16)
            qr_scr[h * s_new:(h + 1) * s_new, :] = _dot(q_h, prope_ref[...]).astype(BF16)

    def update(lat, rope_scores):
        s = _nt_dot(ql_scr[...], lat) + rope_scores
        m_prev = m_scr[...]
        m_new = jnp.maximum(m_prev, jnp.max(s, axis=1, keepdims=True))
        alpha = jnp.exp2(m_prev - m_new)
        p = jnp.exp2(s - m_new[:, :1])
        l_scr[...] = alpha * l_scr[...] + jnp.sum(p, axis=1, keepdims=True)
        acc_scr[...] = jnp.tile(alpha, (1, acc_scr.shape[1] // LANE)) * acc_scr[...] + _dot(p.astype(BF16), lat)
        m_scr[...] = m_new

    q_rope = qr_scr[:, :MLA_ROPE]
    update(lat_ref[0].astype(BF16), _dot(q_rope, kr_ref[0].astype(BF16)))

    @pl.when(j == nj - 1)
    def _():
        update(latn_ref[0].astype(BF16), _nt_dot(q_rope, krn_ref[0].astype(BF16)))
        o_lat = (acc_scr[...] / jnp.tile(l_scr[...], (1, acc_scr.shape[1] // LANE))).astype(BF16)
        out = _dot(o_lat[0:s_new], wv_ref[0])
        for h in range(1, HEADS):
            out = out + _dot(o_lat[h * s_new:(h + 1) * s_new], wv_ref[h])
        o_ref[0] = out.astype(o_ref.dtype)


def _sample_mla_call(q, cache_lat, cache_kr, lat_new, kr_new, w, tile):
    B, S, _ = q.shape
    past = cache_lat.shape[1]
    assert past % tile == 0
    c = cache_lat.shape[-1]
    ow = MLA_HEADS * MLA_V
    return pl.pallas_call(
        _sample_mla_kernel,
        out_shape=jax.ShapeDtypeStruct((B, S, ow), BF16),
        grid=(B, past // tile),
        in_specs=[pl.BlockSpec((1, S, _WIDE), lambda b, j: (b, 0, 0)),
                  pl.BlockSpec((1, tile, c), lambda b, j: (b, j, 0)),
                  pl.BlockSpec((1, MLA_ROPE, tile), lambda b, j: (b, 0, j)),
                  pl.BlockSpec((1, S, c), lambda b, j: (b, 0, 0)),
                  pl.BlockSpec((1, S, MLA_ROPE), lambda b, j: (b, 0, 0)),
                  _const_spec((HEADS, HEAD_GROUP, c)), _const_spec((HEAD_GROUP, LANE)),
                  _const_spec((HEADS, c, ow))],
        out_specs=pl.BlockSpec((1, S, ow), lambda b, j: (b, 0, 0)),
        scratch_shapes=[pltpu.VMEM((HEADS * S, c), BF16),
                        pltpu.VMEM((HEADS * S, LANE), BF16),
                        pltpu.VMEM((HEADS * S, LANE), F32),
                        pltpu.VMEM((HEADS * S, LANE), F32),
                        pltpu.VMEM((HEADS * S, c), F32)],
        compiler_params=pltpu.CompilerParams(
            dimension_semantics=("parallel", "arbitrary"), vmem_limit_bytes=V7X_VMEM_LIMIT),
        name="sample_mla",
    )(q, cache_lat, cache_kr, lat_new, kr_new, w["w_abs"], w["p_rope"], w["w_v_wide"])


def _gather_cols(w, src):
    src = np.asarray(src)
    pieces, start = [], 0
    for i in range(1, len(src) + 1):
        run_ends = i == len(src) or (src[i] != src[i - 1] + 1 if src[i - 1] >= 0 else src[i] >= 0)
        if run_ends:
            if src[start] >= 0:
                pieces.append(w[:, int(src[start]):int(src[start]) + i - start])
            else:
                pieces.append(jnp.zeros((w.shape[0], i - start), w.dtype))
            start = i
    return jnp.concatenate(pieces, axis=1)


def _prep_weights(norm_mix, w_in, b_forget, mla_q_norm, w_mla_uq, mla_kv_norm, w_mla_ukv, w_out,
                  norm_ffn, w_ffn_gate, w_ffn_up, w_ffn_down, norm_final):
    d_model = w_in.shape[0]
    o_kr = _MLA_Q_LORA + _MLA_KV_LORA
    o_fq = o_kr + MLA_ROPE
    o_fk = o_fq + _FOX_WIDTH
    o_fv = o_fk + _FOX_WIDTH
    o_fl = o_fv + _FOX_WIDTH
    half = MLA_ROPE // 2

    src = -np.ones((_NPROJ,), np.int64)
    src[_O_CQ:_O_CKV] = np.arange(0, _MLA_Q_LORA)
    src[_O_CKV:_O_FQ] = np.arange(_MLA_Q_LORA, o_kr)
    src[_O_FQ:_O_FK] = np.arange(o_fq, o_fk)
    src[_O_FK:_O_FV] = np.arange(o_fk, o_fv)
    src[_O_FV:_O_MISC] = np.arange(o_fv, o_fl)
    src[_O_MISC:_O_MISC + MLA_ROPE] = np.arange(o_kr, o_fq)
    src[_O_MISC + _LF_LANE:_O_MISC + _LF_LANE + FOX_HEADS] = np.arange(o_fl, o_fl + FOX_HEADS)
    w_in_r = _gather_cols(w_in, src).astype(BF16)

    src_a = -np.ones((_WIDE,), np.int64)
    for h in range(MLA_HEADS):
        src_a[h * HEAD_GROUP + np.arange(MLA_QK_DIM)] = h * MLA_QK_DIM + np.arange(MLA_QK_DIM)
    w_uq2 = _gather_cols(w_mla_uq, src_a).astype(BF16)

    src_k = -np.ones((_WIDE,), np.int64)
    src_v = np.zeros((MLA_HEADS * MLA_V,), np.int64)
    for h in range(MLA_HEADS):
        src_k[h * HEAD_GROUP + np.arange(MLA_NOPE)] = h * (MLA_NOPE + MLA_V) + np.arange(MLA_NOPE)
        src_v[h * MLA_V + np.arange(MLA_V)] = h * (MLA_NOPE + MLA_V) + MLA_NOPE + np.arange(MLA_V)
    place_r = np.zeros((LANE, _WIDE), np.float32)
    for h in range(MLA_HEADS):
        place_r[np.arange(MLA_ROPE), h * HEAD_GROUP + MLA_NOPE + np.arange(MLA_ROPE)] = 1.0
    w_uk_wide = _gather_cols(w_mla_ukv, src_k)
    w_kexp = jnp.concatenate([w_uk_wide, jnp.asarray(place_r)], axis=0).astype(BF16)
    w_uv = _gather_cols(w_mla_ukv, src_v).astype(BF16)

    w_uk = w_mla_ukv.reshape(_MLA_KV_LORA, MLA_HEADS, MLA_NOPE + MLA_V)[:, :, :MLA_NOPE]
    w_abs = jnp.zeros((MLA_HEADS, HEAD_GROUP, _MLA_KV_LORA), F32)
    w_abs = w_abs.at[:, :MLA_NOPE, :].set(jnp.transpose(w_uk, (1, 2, 0))).astype(BF16)
    p_rope = np.zeros((HEAD_GROUP, LANE), np.float32)
    p_rope[MLA_NOPE + np.arange(MLA_ROPE), np.arange(MLA_ROPE)] = 1.0
    w_uv_h = w_mla_ukv.reshape(_MLA_KV_LORA, MLA_HEADS, MLA_NOPE + MLA_V)[:, :, MLA_NOPE:]
    eye = jnp.asarray(np.eye(MLA_HEADS, dtype=np.float32))
    w_v_wide = (jnp.transpose(w_uv_h, (1, 0, 2))[:, :, None, :] * eye[:, None, :, None])
    w_v_wide = w_v_wide.reshape(MLA_HEADS, _MLA_KV_LORA, MLA_HEADS * MLA_V).astype(BF16)

    bias_f = jnp.zeros((1, LANE), F32).at[0, _LF_LANE:_LF_LANE + FOX_HEADS].set(b_forget.astype(F32))
    place_k = np.zeros((3 * LANE, _WIDE), np.float32)
    ones_q = np.zeros((1, _WIDE), np.float32)
    for h in range(FOX_HEADS):
        for part in range(3):
            place_k[part * LANE + _LF_LANE + h, h * HEAD_GROUP + FOX_HEAD_DIM + part] = -1.0
            ones_q[0, h * HEAD_GROUP + FOX_HEAD_DIM + part] = 1.0

    d_mla = MLA_HEADS * MLA_V
    return {
        "g_mix": norm_mix.reshape(1, d_model).astype(F32),
        "w_in": w_in_r,
        "bias_f": bias_f,
        "q_norm": mla_q_norm.reshape(1, -1).astype(F32),
        "w_uq2": w_uq2,
        "kv_norm": mla_kv_norm.reshape(1, -1).astype(F32),
        "w_kexp": w_kexp,
        "w_uv": w_uv,
        "place_k": jnp.asarray(place_k, BF16),
        "ones_q": jnp.asarray(ones_q),
        "w_abs": w_abs,
        "p_rope": jnp.asarray(p_rope, BF16),
        "w_v_wide": w_v_wide,
        "w_out_a": w_out[:d_mla].astype(BF16),
        "w_out_b": w_out[d_mla:].astype(BF16),
        "g_ffn": norm_ffn.reshape(1, d_model).astype(F32),
        "w_gate": w_ffn_gate.astype(BF16),
        "w_up": w_ffn_up.astype(BF16),
        "w_down": w_ffn_down.astype(BF16),
        "g_final": norm_final.reshape(1, d_model).astype(F32),
    }


def _rope_tables(pos):
    half = MLA_ROPE // 2
    inv_freq = ROPE_THETA ** (-jnp.arange(half, dtype=F32) / half)
    ang = pos.astype(F32)[:, None] * inv_freq[None, :]
    cos, sin = jnp.cos(ang), jnp.sin(ang)
    n = pos.shape[0]
    sc = MLA_SCALE * LOG2E
    zeros = lambda w: jnp.zeros((n, w), F32)
    tqc = jnp.concatenate([jnp.full((n, MLA_NOPE), sc, F32), sc * cos, sc * cos,
                           zeros(LANE - MLA_QK_DIM)], axis=1)
    tqs = jnp.concatenate([zeros(MLA_NOPE), -sc * sin, sc * sin, zeros(LANE - MLA_QK_DIM)], axis=1)
    tkc = jnp.concatenate([cos, cos, zeros(LANE - MLA_ROPE)], axis=1)
    tks = jnp.concatenate([-sin, sin, zeros(LANE - MLA_ROPE)], axis=1)
    return tqc, tqs, tkc, tks


def _pad_rows(a, rows):
    return jnp.concatenate([a, jnp.zeros((rows - a.shape[0],) + a.shape[1:], a.dtype)], axis=0)


PROJ_TILE = 256
ATTN_TQ = 2048
ATTN_TK = 512
FFN_TILE = 256
SAMPLE_TILE = 1024


def kernel(x_prompt, x_sample, cache_mla_latent, cache_mla_krope, cache_fox_k, cache_fox_v, cache_fox_logf, meta_tokens, norm_mix, w_in, b_forget, mla_q_norm, w_mla_uq, mla_kv_norm, w_mla_ukv, w_out, norm_ffn, w_ffn_gate, w_ffn_up, w_ffn_down, norm_final):
    depth = w_in.shape[0]
    assert depth == 1, "single-layer trunk: the meta rows' mixing output never reaches an output"
    B, seq, d_model = x_prompt.shape
    SB, s_new, _ = x_sample.shape
    n_meta = meta_tokens.shape[0]
    past = cache_mla_latent.shape[2]
    assert n_meta <= LANE

    w = _prep_weights(norm_mix[0], w_in[0], b_forget[0], mla_q_norm[0], w_mla_uq[0], mla_kv_norm[0],
                      w_mla_ukv[0], w_out[0], norm_ffn[0], w_ffn_gate[0], w_ffn_up[0], w_ffn_down[0],
                      norm_final)

    zero_carry = jnp.zeros((1, 1, LANE), F32)
    meta = _proj_call(_pad_rows(meta_tokens.astype(F32), LANE)[None], zero_carry,
                      _rope_tables(jnp.arange(LANE)), w, LANE, valid_rows=n_meta, transposed_v=True)
    (m_lat, m_kr, m_fk, m_fv, m_lf, _, m_km, m_vm, _, m_fka, m_fvb, m_carry) = meta
    frames = _proj_call(x_prompt, jnp.broadcast_to(m_carry, (B, 1, LANE)),
                        _rope_tables(n_meta + jnp.arange(seq)), w, PROJ_TILE, transposed_v=True,
                        row_offset=n_meta)
    f_qm, f_km, f_vm, f_fq, f_fka, f_fvb = frames[5:11]
    lat_p, kr_p, fk_p, fv_p, lf_p = _place_rows_call(
        [a[0, :n_meta] for a in (m_lat, m_kr, m_fk, m_fv, m_lf)],
        [a.reshape((B, n_meta + seq) + a.shape[1:]) for a in frames[:5]])

    o_mla = _attn_call(f_qm, f_km, f_vm, m_km[0], m_vm[0],
                       chunk_mask=True, n_meta=n_meta, tq=ATTN_TQ, tk=ATTN_TK, name="attn_mla")
    o_fox = _attn_call(f_fq, f_fka, f_fvb, m_fka[0], m_fvb[0],
                       chunk_mask=False, n_meta=n_meta, tq=ATTN_TQ, tk=ATTN_TK, name="attn_fox")
    y_prompt = _ffn_call(x_prompt.reshape(B * seq, d_model), o_mla.reshape(B * seq, -1),
                         o_fox.reshape(B * seq, -1), w, FFN_TILE).reshape(B, seq, d_model)

    lat_p, kr_p, lf_p = lat_p[None], kr_p[None], lf_p[None]
    fk_p = fk_p.reshape(1, B, n_meta + seq, FOX_HEADS, FOX_HEAD_DIM)
    fv_p = fv_p.reshape(1, B, n_meta + seq, FOX_HEADS, FOX_HEAD_DIM)

    logf_t = jnp.transpose(cache_fox_logf[0].astype(F32), (0, 2, 1))
    cum_cache = _cumsum_call(logf_t.reshape(SB * FOX_HEADS, past)).reshape(SB, FOX_HEADS, past)
    carry_s = jnp.zeros((SB, 1, LANE), F32).at[:, 0, _LF_LANE:_LF_LANE + FOX_HEADS].set(
        cum_cache[:, :, past - 1])
    samp = _proj_call(x_sample, carry_s, _rope_tables(past + jnp.arange(s_new)), w, s_new)
    (s_lat, s_kr, s_fk, s_fv, s_lf, s_qm, _, _, s_fq, s_fka, s_fvb, _) = samp

    positions_last = lambda a: jnp.moveaxis(a, 1, -1)
    so_mla = _sample_mla_call(s_qm, cache_mla_latent[0], positions_last(cache_mla_krope[0]), s_lat, s_kr, w,
                              SAMPLE_TILE)
    so_fox = _sample_fox_call(s_fq, positions_last(cache_fox_k[0]), positions_last(cache_fox_v[0]), cum_cache,
                              s_fka, s_fvb, SAMPLE_TILE)
    y_sample = _ffn_call(x_sample.reshape(SB * s_new, d_model), so_mla.reshape(SB * s_new, -1),
                         so_fox.reshape(SB * s_new, -1), w, FFN_TILE).reshape(SB, s_new, d_model)

    heads = lambda a: a.reshape(1, SB, s_new, FOX_HEADS, FOX_HEAD_DIM)
    return (y_prompt, y_sample, lat_p, kr_p, fk_p, fv_p, lf_p,
            s_lat[None], s_kr[None], heads(s_fk), heads(s_fv), s_lf[None])
```

```python
import functools

import numpy as np
import jax
import jax.numpy as jnp
from jax import lax
from jax.experimental import pallas as pl
from jax.experimental.pallas import tpu as pltpu

CHUNK = 64
EPS = 1e-6
NEG_INF = -1e30
MLA_HEADS = 8
MLA_NOPE = 64
MLA_ROPE = 32
MLA_V = 64
MLA_QK_DIM = MLA_NOPE + MLA_ROPE
MLA_SCALE = MLA_QK_DIM ** -0.5
ROPE_THETA = 10000.0
FOX_HEADS = 8
FOX_HEAD_DIM = 64
FOX_SCALE = FOX_HEAD_DIM ** -0.5
LOG2E = 1.4426950408889634

LANE = 128
SUBLANE = 8
BF16_SUBLANES = 16
HEAD_GROUP = LANE
HEADS = 8
V7X_VMEM_LIMIT = 56 * 1024 * 1024

BF16 = jnp.bfloat16
F32 = jnp.float32

_MLA_Q_LORA = 384
_MLA_KV_LORA = 256
_FOX_WIDTH = FOX_HEADS * FOX_HEAD_DIM
_WIDE = HEADS * HEAD_GROUP
_O_CQ = 0
_O_CKV = _O_CQ + _MLA_Q_LORA
_O_FQ = _O_CKV + _MLA_KV_LORA
_O_FK = _O_FQ + _FOX_WIDTH
_O_FV = _O_FK + _FOX_WIDTH
_O_MISC = _O_FV + _FOX_WIDTH
_NPROJ = _O_MISC + LANE
_LF_LANE = MLA_ROPE


def _nt_dot(a, b):
    return lax.dot_general(a, b, (((1,), (1,)), ((), ())), preferred_element_type=F32)


def _dot(a, b):
    return jnp.dot(a, b, preferred_element_type=F32)


def _rms(x, g):
    return x * lax.rsqrt(jnp.mean(x * x, axis=-1, keepdims=True) + EPS) * g


def _split3(x):
    hi = x.astype(BF16)
    r = x - hi.astype(F32)
    mid = r.astype(BF16)
    lo = (r - mid.astype(F32)).astype(BF16)
    return hi, mid, lo


def _proj_kernel(*refs, n_base, **static):
    _proj_body(*refs[n_base:], **static)


def _proj_body(x_ref, carry_in_ref, tqc_ref, tqs_ref, tkc_ref, tks_ref,
               g_ref, w_in_ref, bias_ref, qn_ref, w_uq_ref, kvn_ref, w_kexp_ref, w_uv_ref,
               tri_ref, place_ref, ones_q_ref,
               lat_ref, kr_ref, fk_ref, fv_ref, lf_ref,
               qm_ref, km_ref, vm_ref, fq_ref, fka_ref, fvb_ref, carry_out_ref,
               carry_scr, *, valid_rows, transposed_v):
    t = pl.program_id(1)

    @pl.when(t == 0)
    def _():
        carry_scr[...] = carry_in_ref[0]

    x = x_ref[0]
    h = _rms(x, g_ref[...]).astype(BF16)
    proj = _dot(h, w_in_ref[...])

    lane = lax.broadcasted_iota(jnp.int32, (x.shape[0], LANE), 1)
    half = MLA_ROPE // 2

    def rotary(a, first, cos_tab, sin_tab):
        partner = jnp.where(lane < first + half, pltpu.roll(a, LANE - half, 1), pltpu.roll(a, half, 1))
        return a * cos_tab + partner * sin_tab

    def put(ref, val):
        ref[...] = val.reshape(ref.shape)

    cqn = _rms(proj[:, _O_CQ:_O_CKV], qn_ref[...]).astype(BF16)
    q2 = _dot(cqn, w_uq_ref[...])
    tqc, tqs = tqc_ref[...], tqs_ref[...]
    qm_ref[0] = jnp.concatenate(
        [rotary(q2[:, h * HEAD_GROUP:(h + 1) * HEAD_GROUP], MLA_NOPE, tqc, tqs) for h in range(HEADS)],
        axis=1).astype(BF16)

    latent = _rms(proj[:, _O_CKV:_O_FQ], kvn_ref[...])
    put(lat_ref, latent)
    misc = proj[:, _O_MISC:_NPROJ]
    kr = rotary(misc, 0, tkc_ref[...], tks_ref[...])
    put(kr_ref, kr[:, :MLA_ROPE])
    lat_bf = latent.astype(BF16)
    kcat = jnp.concatenate([lat_bf, kr.astype(BF16)], axis=1)
    km_ref[0] = _dot(kcat, w_kexp_ref[...]).astype(BF16)
    v_mla = _dot(lat_bf, w_uv_ref[...])
    vm_ref[0] = (v_mla.T if transposed_v else v_mla).astype(BF16)

    fq = proj[:, _O_FQ:_O_FK]
    fk = proj[:, _O_FK:_O_FV]
    fv = proj[:, _O_FV:_O_MISC]
    fvb_ref[0] = (fv.T if transposed_v else fv).astype(BF16)

    def head_groups(a):
        low = lane < FOX_HEAD_DIM
        groups = []
        for p in range(FOX_HEADS // 2):
            pair = a[:, p * LANE:(p + 1) * LANE]
            groups.append(jnp.where(low, pair, 0.0))
            groups.append(jnp.where(low, pltpu.roll(pair, LANE - FOX_HEAD_DIM, 1), 0.0))
        return groups

    put(fk_ref, fk)
    put(fv_ref, fv)

    z = misc + bias_ref[...]
    keep = (lane >= _LF_LANE) & (lane < _LF_LANE + FOX_HEADS)
    if valid_rows is not None:
        keep = keep & (lax.broadcasted_iota(jnp.int32, z.shape, 0) < valid_rows)
    lf = jnp.where(keep, jnp.minimum(z, 0.0) - jnp.log1p(jnp.exp(-jnp.abs(z))), 0.0)
    put(lf_ref, lf[:, _LF_LANE:_LF_LANE + FOX_HEADS])
    hi, mid, lo = _split3(lf)
    tri = tri_ref[...]
    cum = _dot(tri, hi) + _dot(tri, mid) + _dot(tri, lo) + carry_scr[...]
    n_rows = cum.shape[0]
    carry_scr[...] = cum[n_rows - 1:n_rows, :]
    carry_out_ref[0] = cum[n_rows - 1:n_rows, :]
    c_hi, c_mid, c_lo = _split3(cum * LOG2E)
    kb = _dot(jnp.concatenate([c_hi, c_mid, c_lo], axis=1), place_ref[...])
    fq_wide = jnp.concatenate(head_groups(fq), axis=1)
    fq_ref[0] = (fq_wide * (FOX_SCALE * LOG2E) + ones_q_ref[...]).astype(BF16)
    fka_ref[0] = (jnp.concatenate(head_groups(fk), axis=1) + kb).astype(BF16)


def _const_spec(shape):
    zeros = (0,) * len(shape)
    return pl.BlockSpec(shape, lambda *_: zeros, pipeline_mode=pl.Buffered(1))


def _proj_call(x, carry_in, tabs, w, tile, valid_rows=None, transposed_v=False, row_offset=0):
    B, L, D = x.shape
    assert L % tile == 0
    nt = L // tile
    tri = jnp.asarray(np.tril(np.ones((tile, tile), np.float32)), BF16)

    row = lambda width: pl.BlockSpec((1, tile, width), lambda b, t: (b, t, 0))
    tab = pl.BlockSpec((tile, LANE), lambda b, t: (t, 0))
    in_specs = [row(D), pl.BlockSpec((1, 1, LANE), lambda b, t: (b, 0, 0)), tab, tab, tab, tab,
                _const_spec((1, D)), _const_spec((D, _NPROJ)), _const_spec((1, LANE)),
                _const_spec((1, _MLA_Q_LORA)), _const_spec((_MLA_Q_LORA, _WIDE)),
                _const_spec((1, _MLA_KV_LORA)), _const_spec((_MLA_KV_LORA + LANE, _WIDE)),
                _const_spec((_MLA_KV_LORA, MLA_HEADS * MLA_V)),
                _const_spec((tile, tile)), _const_spec((3 * LANE, _WIDE)), _const_spec((1, _WIDE))]
    out_shapes = [
        jax.ShapeDtypeStruct((B, L, _MLA_KV_LORA), F32),
        jax.ShapeDtypeStruct((B, L, MLA_ROPE), F32),
        jax.ShapeDtypeStruct((B, L, _FOX_WIDTH), F32),
        jax.ShapeDtypeStruct((B, L, _FOX_WIDTH), F32),
        jax.ShapeDtypeStruct((B, L, FOX_HEADS), F32),
        jax.ShapeDtypeStruct((B, L, _WIDE), BF16),
        jax.ShapeDtypeStruct((B, L, _WIDE), BF16),
        jax.ShapeDtypeStruct((B, L, MLA_HEADS * MLA_V), BF16),
        jax.ShapeDtypeStruct((B, L, _WIDE), BF16),
        jax.ShapeDtypeStruct((B, L, _WIDE), BF16),
        jax.ShapeDtypeStruct((B, L, _FOX_WIDTH), BF16),
        jax.ShapeDtypeStruct((B, 1, LANE), F32),
    ]
    out_specs = [row(s.shape[-1]) for s in out_shapes[:-1]]
    out_specs.append(pl.BlockSpec((1, 1, LANE), lambda b, t: (b, 0, 0)))
    if transposed_v:
        for idx in (7, 10):
            vw = out_shapes[idx].shape[-1]
            out_shapes[idx] = jax.ShapeDtypeStruct((B, vw, L), BF16)
            out_specs[idx] = pl.BlockSpec((1, vw, tile), lambda b, t: (b, 0, t))
    if row_offset:
        assert row_offset % SUBLANE == 0 and L % SUBLANE == 0
        for idx in range(5):
            width = out_shapes[idx].shape[-1]
            out_shapes[idx] = jax.ShapeDtypeStruct((B * (row_offset + L), width), F32)
            out_specs[idx] = pl.BlockSpec(
                (pl.Element(tile), pl.Element(width)),
                lambda b, t: (pl.multiple_of(b * (row_offset + L) + row_offset + t * tile, SUBLANE), 0))
    bases = [jnp.zeros(s.shape, s.dtype) for s in out_shapes[:5]] if row_offset else []
    n_base = len(bases)
    return pl.pallas_call(
        functools.partial(_proj_kernel, n_base=n_base, valid_rows=valid_rows, transposed_v=transposed_v),
        out_shape=out_shapes,
        grid=(B, nt),
        in_specs=[pl.BlockSpec(memory_space=pl.ANY)] * n_base + in_specs,
        out_specs=out_specs,
        input_output_aliases={i: i for i in range(n_base)},
        scratch_shapes=[pltpu.VMEM((1, LANE), F32)],
        compiler_params=pltpu.CompilerParams(
            dimension_semantics=("parallel", "arbitrary"), vmem_limit_bytes=V7X_VMEM_LIMIT),
        name="proj",
    )(*bases, x, carry_in, *tabs, w["g_mix"], w["w_in"], w["bias_f"], w["q_norm"], w["w_uq2"],
      w["kv_norm"], w["w_kexp"], w["w_uv"], tri, w["place_k"], w["ones_q"])


def _place_rows_kernel(*refs):
    n = len(refs) // 3
    for src, out in zip(refs[:n], refs[2 * n:]):
        out[0] = src[...]


def _place_rows_call(rows, dests):
    n = len(rows)
    B = dests[0].shape[0]
    in_specs = [pl.BlockSpec(r.shape, lambda b, nd=r.ndim: (0,) * nd) for r in rows]
    in_specs += [pl.BlockSpec(memory_space=pl.ANY) for _ in dests]
    out_specs = [pl.BlockSpec((1,) + r.shape, lambda b, nd=r.ndim: (b,) + (0,) * nd) for r in rows]
    return pl.pallas_call(
        _place_rows_kernel,
        out_shape=[jax.ShapeDtypeStruct(d.shape, d.dtype) for d in dests],
        grid=(B,),
        in_specs=in_specs,
        out_specs=out_specs,
        input_output_aliases={n + i: i for i in range(n)},
        compiler_params=pltpu.CompilerParams(dimension_semantics=("arbitrary",)),
        name="place_meta_rows",
    )(*rows, *dests)


QK_LOOKAHEAD = 8
UNIT_Q = 256
UNIT_K = 256
ONES_ROWS = 16


def _with_ones_rows(v_t):
    row = lax.broadcasted_iota(jnp.int32, (ONES_ROWS, v_t.shape[1]), 0)
    return jnp.concatenate([v_t, jnp.where(row == 0, 1.0, 0.0).astype(v_t.dtype)], axis=0)


def _softmax_update_t(h, qs, s_t, v_t, m_scr, acc_scr):
    uk, uq = s_t.shape
    s3 = s_t.reshape(uk // SUBLANE, SUBLANE, uq)
    m_cur = jnp.max(s3, axis=0)
    for shift in (4, 2, 1):
        m_cur = jnp.maximum(m_cur, pltpu.roll(m_cur, shift, 0))
    m_prev = m_scr[h, :, qs]
    m_new = jnp.maximum(m_prev, m_cur)
    alpha = jnp.exp2(m_prev - m_new)
    p_t = jnp.exp2(s3 - m_new[None]).reshape(uk, uq)
    pv = _dot(v_t, p_t.astype(BF16))
    rows = pv.shape[0]
    acc = acc_scr[h, :, qs].reshape(rows // SUBLANE, SUBLANE, uq)
    acc_scr[h, :, qs] = (alpha[None] * acc).reshape(rows, uq) + pv
    m_scr[h, :, qs] = m_new


def _attn_kernel(qi_ref, kj_ref, q_ref, k_ref, vt_ref, kmeta_ref, vtmeta_ref, o_ref,
                 m_scr, acc_scr, *, chunk_mask, n_meta):
    p_id = pl.program_id(1)
    i = qi_ref[p_id]
    j = kj_ref[p_id]
    tq = q_ref.shape[1]
    tk = k_ref.shape[1]
    dv = acc_scr.shape[1] - ONES_ROWS

    def head(ref2d, h):
        return ref2d[:, h * HEAD_GROUP:(h + 1) * HEAD_GROUP]

    @pl.when(j == 0)
    def _():
        n_pad = kmeta_ref.shape[0]
        n_rows = -(-n_meta // BF16_SUBLANES) * BF16_SUBLANES
        row = lax.broadcasted_iota(jnp.int32, (n_rows, tq), 0)
        scores = [_nt_dot(kmeta_ref[:n_rows, h * HEAD_GROUP:(h + 1) * HEAD_GROUP], head(q_ref.at[0], h))
                  for h in range(HEADS)]
        for h in range(HEADS):
            s3 = jnp.where(row < n_meta, scores[h], NEG_INF).reshape(n_rows // SUBLANE, SUBLANE, tq)
            m_new = jnp.max(s3, axis=0)
            for shift in (4, 2, 1):
                m_new = jnp.maximum(m_new, pltpu.roll(m_new, shift, 0))
            p_t = jnp.exp2(s3 - m_new[None]).reshape(n_rows, tq).astype(BF16)
            p_t = jnp.concatenate([p_t, jnp.zeros((n_pad - n_rows, tq), BF16)], axis=0)
            acc_scr[h] = _dot(_with_ones_rows(vtmeta_ref[h * dv:(h + 1) * dv, :]), p_t)
            m_scr[h] = m_new

    uq = min(UNIT_Q, tq)
    uk = min(UNIT_K, tk)

    def block(key_shift):
        masked = key_shift is not None
        units = [(h, q0, k0) for h in range(HEADS) for k0 in range(0, tk, uk) for q0 in range(0, tq, uq)]
        if masked:
            units = [u for u in units if key_shift + u[2] < u[1] + uq]
            kk = lax.broadcasted_iota(jnp.int32, (uk, uq), 0) + key_shift
            qq = lax.broadcasted_iota(jnp.int32, (uk, uq), 1)

        def qk(u):
            h, q0, k0 = u
            return _nt_dot(k_ref[0, k0:k0 + uk, h * HEAD_GROUP:(h + 1) * HEAD_GROUP],
                           q_ref[0, q0:q0 + uq, h * HEAD_GROUP:(h + 1) * HEAD_GROUP])

        scores = {}
        for n in range(-QK_LOOKAHEAD, len(units)):
            if n + QK_LOOKAHEAD < len(units):
                scores[n + QK_LOOKAHEAD] = qk(units[n + QK_LOOKAHEAD])
            if n >= 0:
                h, q0, k0 = units[n]
                s_t = scores.pop(n)
                if masked and key_shift + k0 + uk > q0 + 1:
                    if chunk_mask:
                        valid = (kk + k0) // CHUNK <= (qq + q0) // CHUNK
                    else:
                        valid = kk + k0 <= qq + q0
                    s_t = jnp.where(valid, s_t, NEG_INF)
                v_t = _with_ones_rows(vt_ref[0, h * dv:(h + 1) * dv, k0:k0 + uk])
                _softmax_update_t(h, slice(q0, q0 + uq), s_t, v_t, m_scr, acc_scr)

    ratio = tq // tk
    d = j - ratio * i

    @pl.when(d < 0)
    def _():
        block(None)

    for dd in range(ratio):
        @pl.when(d == dd)
        def _(dd=dd):
            block(dd * tk)

    @pl.when(d == ratio - 1)
    def _():
        def normalised(h):
            return acc_scr[h, :dv, :] / acc_scr[h, dv:dv + 1, :]

        for hp in range(HEADS // 2):
            o_pair_t = jnp.concatenate([normalised(2 * hp), normalised(2 * hp + 1)], axis=0)
            o_ref[0, :, hp * 2 * dv:(hp + 1) * 2 * dv] = o_pair_t.T.astype(o_ref.dtype)


def _attn_call(q, k, v_t, k_meta, v_t_meta, *, chunk_mask, n_meta, tq, tk, name):
    B, F, _ = q.shape
    tq, tk = min(tq, F), min(tk, F)
    assert F % tq == 0 and tq % tk == 0 and tk % CHUNK == 0
    ratio = tq // tk
    pairs = [(i, j) for i in range(F // tq) for j in range(ratio * (i + 1))]
    qi = jnp.asarray([p[0] for p in pairs], jnp.int32)
    kj = jnp.asarray([p[1] for p in pairs], jnp.int32)
    vw = v_t.shape[1]
    grid_spec = pltpu.PrefetchScalarGridSpec(
        num_scalar_prefetch=2,
        grid=(B, len(pairs)),
        in_specs=[
            pl.BlockSpec((1, tq, _WIDE), lambda b, p, qi, kj: (b, qi[p], 0)),
            pl.BlockSpec((1, tk, _WIDE), lambda b, p, qi, kj: (b, kj[p], 0)),
            pl.BlockSpec((1, vw, tk), lambda b, p, qi, kj: (b, 0, kj[p])),
            pl.BlockSpec((LANE, _WIDE), lambda b, p, qi, kj: (0, 0)),
            pl.BlockSpec((vw, LANE), lambda b, p, qi, kj: (0, 0)),
        ],
        out_specs=pl.BlockSpec((1, tq, vw), lambda b, p, qi, kj: (b, qi[p], 0)),
        scratch_shapes=[pltpu.VMEM((HEADS, SUBLANE, tq), F32),
                        pltpu.VMEM((HEADS, vw // HEADS + ONES_ROWS, tq), F32)],
    )
    return pl.pallas_call(
        functools.partial(_attn_kernel, chunk_mask=chunk_mask, n_meta=n_meta),
        out_shape=jax.ShapeDtypeStruct((B, F, vw), BF16),
        grid_spec=grid_spec,
        compiler_params=pltpu.CompilerParams(
            dimension_semantics=("parallel", "arbitrary"), vmem_limit_bytes=V7X_VMEM_LIMIT),
        name=name,
    )(qi, kj, q, k, v_t, k_meta, v_t_meta)


def _ffn_kernel(x_ref, oa_ref, ob_ref, woa_ref, wob_ref, g_ref, wg_ref, wu_ref, wd_ref, gf_ref, y_ref):
    x1 = x_ref[...] + _dot(oa_ref[...], woa_ref[...]) + _dot(ob_ref[...], wob_ref[...])
    h = _rms(x1, g_ref[...]).astype(BF16)
    gate = _dot(h, wg_ref[...])
    up = _dot(h, wu_ref[...])
    act = (gate * jax.nn.sigmoid(gate) * up).astype(BF16)
    x2 = x1 + _dot(act, wd_ref[...])
    y_ref[...] = _rms(x2, gf_ref[...])


def _ffn_call(x, oa, ob, w, tile):
    R, D = x.shape
    tile = min(tile, R)
    assert R % tile == 0
    dm = oa.shape[-1]
    dff = w["w_gate"].shape[-1]
    row = lambda width: pl.BlockSpec((tile, width), lambda r: (r, 0))
    return pl.pallas_call(
        _ffn_kernel,
        out_shape=jax.ShapeDtypeStruct((R, D), F32),
        grid=(R // tile,),
        in_specs=[row(D), row(dm), row(dm),
                  _const_spec((dm, D)), _const_spec((dm, D)), _const_spec((1, D)),
                  _const_spec((D, dff)), _const_spec((D, dff)), _const_spec((dff, D)),
                  _const_spec((1, D))],
        out_specs=row(D),
        compiler_params=pltpu.CompilerParams(
            dimension_semantics=("parallel",), vmem_limit_bytes=V7X_VMEM_LIMIT),
        name="ffn",
    )(x, oa, ob, w["w_out_a"], w["w_out_b"], w["g_ffn"], w["w_gate"], w["w_up"], w["w_down"],
      w["g_final"])


_CUM_CHUNK = 256


def _cumsum_kernel(x_ref, u_ref, o_ref):
    rows, n = x_ref.shape
    u = u_ref[...]
    carry = jnp.zeros((rows, 1), F32)
    for c in range(n // _CUM_CHUNK):
        hi, mid, lo = _split3(x_ref[:, c * _CUM_CHUNK:(c + 1) * _CUM_CHUNK])
        y = _dot(hi, u) + _dot(mid, u) + _dot(lo, u) + carry
        o_ref[:, c * _CUM_CHUNK:(c + 1) * _CUM_CHUNK] = y
        carry = y[:, _CUM_CHUNK - 1:_CUM_CHUNK]


def _cumsum_call(x):
    rows, n = x.shape
    assert n % _CUM_CHUNK == 0
    u = jnp.asarray(np.triu(np.ones((_CUM_CHUNK, _CUM_CHUNK), np.float32)), BF16)
    return pl.pallas_call(
        _cumsum_kernel,
        out_shape=jax.ShapeDtypeStruct((rows, n), F32),
        compiler_params=pltpu.CompilerParams(vmem_limit_bytes=V7X_VMEM_LIMIT),
        name="cache_cumsum",
    )(x, u)


def _sample_fox_kernel(fq_ref, k_ref, v_ref, ck_ref, fkn_ref, fvn_ref, o_ref, m_scr, l_scr, acc_scr):
    j = pl.program_id(1)
    nj = pl.num_programs(1)
    s_new = fq_ref.shape[1]
    dim = acc_scr.shape[-1]

    @pl.when(j == 0)
    def _():
        m_scr[...] = jnp.full(m_scr.shape, NEG_INF, F32)
        l_scr[...] = jnp.zeros(l_scr.shape, F32)
        acc_scr[...] = jnp.zeros(acc_scr.shape, F32)

    def update(h, s, v_h, v_transposed):
        m_prev = m_scr[h]
        m_new = jnp.maximum(m_prev, jnp.max(s, axis=1, keepdims=True))
        alpha = jnp.exp2(m_prev - m_new)
        p = jnp.exp2(s - m_new[:, :1])
        l_scr[h] = alpha * l_scr[h] + jnp.sum(p, axis=1, keepdims=True)
        p = p.astype(BF16)
        pv = _nt_dot(p, v_h) if v_transposed else _dot(p, v_h)
        acc_scr[h] = alpha[:, :dim] * acc_scr[h] + pv
        m_scr[h] = m_new

    ck2 = ck_ref[0] * LOG2E
    scores = [_dot(fq_ref[0, :, h * HEAD_GROUP:h * HEAD_GROUP + dim], k_ref[0, h].astype(BF16))
              for h in range(HEADS)]
    for h in range(HEADS):
        update(h, scores[h] - ck2[h:h + 1, :], v_ref[0, h].astype(BF16), True)

    @pl.when(j == nj - 1)
    def _():
        r = lax.broadcasted_iota(jnp.int32, (s_new, s_new), 0)
        c = lax.broadcasted_iota(jnp.int32, (s_new, s_new), 1)
        new_scores = [_nt_dot(fq_ref[0, :, h * HEAD_GROUP:(h + 1) * HEAD_GROUP],
                              fkn_ref[0, :, h * HEAD_GROUP:(h + 1) * HEAD_GROUP]) for h in range(HEADS)]
        for h in range(HEADS):
            update(h, jnp.where(c <= r, new_scores[h], NEG_INF), fvn_ref[0, :, h * dim:(h + 1) * dim], False)
        o_ref[0] = jnp.concatenate([acc_scr[h] / l_scr[h][:, :dim] for h in range(HEADS)],
                                   axis=1).astype(o_ref.dtype)


def _sample_fox_call(fq, cache_k, cache_v, cum_cache, fka_new, fv_new, tile):
    B, S, _ = fq.shape
    _, heads, dim, past = cache_k.shape
    assert past % tile == 0 and heads == HEADS
    w = heads * dim
    cache_spec = pl.BlockSpec((1, heads, dim, tile), lambda b, j: (b, 0, 0, j))
    return pl.pallas_call(
        _sample_fox_kernel,
        out_shape=jax.ShapeDtypeStruct((B, S, w), BF16),
        grid=(B, past // tile),
        in_specs=[pl.BlockSpec((1, S, _WIDE), lambda b, j: (b, 0, 0)),
                  cache_spec, cache_spec,
                  pl.BlockSpec((1, HEADS, tile), lambda b, j: (b, 0, j)),
                  pl.BlockSpec((1, S, _WIDE), lambda b, j: (b, 0, 0)),
                  pl.BlockSpec((1, S, w), lambda b, j: (b, 0, 0))],
        out_specs=pl.BlockSpec((1, S, w), lambda b, j: (b, 0, 0)),
        scratch_shapes=[pltpu.VMEM((HEADS, S, LANE), F32),
                        pltpu.VMEM((HEADS, S, LANE), F32),
                        pltpu.VMEM((HEADS, S, dim), F32)],
        compiler_params=pltpu.CompilerParams(
            dimension_semantics=("parallel", "arbitrary"), vmem_limit_bytes=V7X_VMEM_LIMIT),
        name="sample_fox",
    )(fq, cache_k, cache_v, cum_cache, fka_new, fv_new)


def _sample_mla_kernel(q_ref, lat_ref, kr_ref, latn_ref, krn_ref, wabs_ref, prope_ref, wv_ref, o_ref,
                       ql_scr, qr_scr, m_scr, l_scr, acc_scr):
    j = pl.program_id(1)
    nj = pl.num_programs(1)
    s_new = q_ref.shape[1]

    @pl.when(j == 0)
    def _():
        m_scr[...] = jnp.full(m_scr.shape, NEG_INF, F32)
        l_scr[...] = jnp.zeros(l_scr.shape, F32)
        acc_scr[...] = jnp.zeros(acc_scr.shape, F32)
        for h in range(HEADS):
            q_h = q_ref[0, :, h * HEAD_GROUP:(h + 1) * HEAD_GROUP]
            ql_scr[h * s_new:(h + 1) * s_new, :] = _dot(q_h, wabs_ref[h]).astype(BF16)
            qr_scr[h * s_new:(h + 1) * s_new, :] = _dot(q_h, prope_ref[...]).astype(BF16)

    def update(lat, rope_scores):
        s = _nt_dot(ql_scr[...], lat) + rope_scores
        m_prev = m_scr[...]
        m_new = jnp.maximum(m_prev, jnp.max(s, axis=1, keepdims=True))
        alpha = jnp.exp2(m_prev - m_new)
        p = jnp.exp2(s - m_new[:, :1])
        l_scr[...] = alpha * l_scr[...] + jnp.sum(p, axis=1, keepdims=True)
        acc_scr[...] = jnp.tile(alpha, (1, acc_scr.shape[1] // LANE)) * acc_scr[...] + _dot(p.astype(BF16), lat)
        m_scr[...] = m_new

    q_rope = qr_scr[:, :MLA_ROPE]
    update(lat_ref[0].astype(BF16), _dot(q_rope, kr_ref[0].astype(BF16)))

    @pl.when(j == nj - 1)
    def _():
        update(latn_ref[0].astype(BF16), _nt_dot(q_rope, krn_ref[0].astype(BF16)))
        o_lat = (acc_scr[...] / jnp.tile(l_scr[...], (1, acc_scr.shape[1] // LANE))).astype(BF16)
        out = _dot(o_lat[0:s_new], wv_ref[0])
        for h in range(1, HEADS):
            out = out + _dot(o_lat[h * s_new:(h + 1) * s_new], wv_ref[h])
        o_ref[0] = out.astype(o_ref.dtype)


def _sample_mla_call(q, cache_lat, cache_kr, lat_new, kr_new, w, tile):
    B, S, _ = q.shape
    past = cache_lat.shape[1]
    assert past % tile == 0
    c = cache_lat.shape[-1]
    ow = MLA_HEADS * MLA_V
    return pl.pallas_call(
        _sample_mla_kernel,
        out_shape=jax.ShapeDtypeStruct((B, S, ow), BF16),
        grid=(B, past // tile),
        in_specs=[pl.BlockSpec((1, S, _WIDE), lambda b, j: (b, 0, 0)),
                  pl.BlockSpec((1, tile, c), lambda b, j: (b, j, 0)),
                  pl.BlockSpec((1, MLA_ROPE, tile), lambda b, j: (b, 0, j)),
                  pl.BlockSpec((1, S, c), lambda b, j: (b, 0, 0)),
                  pl.BlockSpec((1, S, MLA_ROPE), lambda b, j: (b, 0, 0)),
                  _const_spec((HEADS, HEAD_GROUP, c)), _const_spec((HEAD_GROUP, LANE)),
                  _const_spec((HEADS, c, ow))],
        out_specs=pl.BlockSpec((1, S, ow), lambda b, j: (b, 0, 0)),
        scratch_shapes=[pltpu.VMEM((HEADS * S, c), BF16),
                        pltpu.VMEM((HEADS * S, LANE), BF16),
                        pltpu.VMEM((HEADS * S, LANE), F32),
                        pltpu.VMEM((HEADS * S, LANE), F32),
                        pltpu.VMEM((HEADS * S, c), F32)],
        compiler_params=pltpu.CompilerParams(
            dimension_semantics=("parallel", "arbitrary"), vmem_limit_bytes=V7X_VMEM_LIMIT),
        name="sample_mla",
    )(q, cache_lat, cache_kr, lat_new, kr_new, w["w_abs"], w["p_rope"], w["w_v_wide"])


def _gather_cols(w, src):
    src = np.asarray(src)
    pieces, start = [], 0
    for i in range(1, len(src) + 1):
        run_ends = i == len(src) or (src[i] != src[i - 1] + 1 if src[i - 1] >= 0 else src[i] >= 0)
        if run_ends:
            if src[start] >= 0:
                pieces.append(w[:, int(src[start]):int(src[start]) + i - start])
            else:
                pieces.append(jnp.zeros((w.shape[0], i - start), w.dtype))
            start = i
    return jnp.concatenate(pieces, axis=1)


def _prep_weights(norm_mix, w_in, b_forget, mla_q_norm, w_mla_uq, mla_kv_norm, w_mla_ukv, w_out,
                  norm_ffn, w_ffn_gate, w_ffn_up, w_ffn_down, norm_final):
    d_model = w_in.shape[0]
    o_kr = _MLA_Q_LORA + _MLA_KV_LORA
    o_fq = o_kr + MLA_ROPE
    o_fk = o_fq + _FOX_WIDTH
    o_fv = o_fk + _FOX_WIDTH
    o_fl = o_fv + _FOX_WIDTH
    half = MLA_ROPE // 2

    src = -np.ones((_NPROJ,), np.int64)
    src[_O_CQ:_O_CKV] = np.arange(0, _MLA_Q_LORA)
    src[_O_CKV:_O_FQ] = np.arange(_MLA_Q_LORA, o_kr)
    src[_O_FQ:_O_FK] = np.arange(o_fq, o_fk)
    src[_O_FK:_O_FV] = np.arange(o_fk, o_fv)
    src[_O_FV:_O_MISC] = np.arange(o_fv, o_fl)
    src[_O_MISC:_O_MISC + MLA_ROPE] = np.arange(o_kr, o_fq)
    src[_O_MISC + _LF_LANE:_O_MISC + _LF_LANE + FOX_HEADS] = np.arange(o_fl, o_fl + FOX_HEADS)
    w_in_r = _gather_cols(w_in, src).astype(BF16)

    src_a = -np.ones((_WIDE,), np.int64)
    for h in range(MLA_HEADS):
        src_a[h * HEAD_GROUP + np.arange(MLA_QK_DIM)] = h * MLA_QK_DIM + np.arange(MLA_QK_DIM)
    w_uq2 = _gather_cols(w_mla_uq, src_a).astype(BF16)

    src_k = -np.ones((_WIDE,), np.int64)
    src_v = np.zeros((MLA_HEADS * MLA_V,), np.int64)
    for h in range(MLA_HEADS):
        src_k[h * HEAD_GROUP + np.arange(MLA_NOPE)] = h * (MLA_NOPE + MLA_V) + np.arange(MLA_NOPE)
        src_v[h * MLA_V + np.arange(MLA_V)] = h * (MLA_NOPE + MLA_V) + MLA_NOPE + np.arange(MLA_V)
    place_r = np.zeros((LANE, _WIDE), np.float32)
    for h in range(MLA_HEADS):
        place_r[np.arange(MLA_ROPE), h * HEAD_GROUP + MLA_NOPE + np.arange(MLA_ROPE)] = 1.0
    w_uk_wide = _gather_cols(w_mla_ukv, src_k)
    w_kexp = jnp.concatenate([w_uk_wide, jnp.asarray(place_r)], axis=0).astype(BF16)
    w_uv = _gather_cols(w_mla_ukv, src_v).astype(BF16)

    w_uk = w_mla_ukv.reshape(_MLA_KV_LORA, MLA_HEADS, MLA_NOPE + MLA_V)[:, :, :MLA_NOPE]
    w_abs = jnp.zeros((MLA_HEADS, HEAD_GROUP, _MLA_KV_LORA), F32)
    w_abs = w_abs.at[:, :MLA_NOPE, :].set(jnp.transpose(w_uk, (1, 2, 0))).astype(BF16)
    p_rope = np.zeros((HEAD_GROUP, LANE), np.float32)
    p_rope[MLA_NOPE + np.arange(MLA_ROPE), np.arange(MLA_ROPE)] = 1.0
    w_uv_h = w_mla_ukv.reshape(_MLA_KV_LORA, MLA_HEADS, MLA_NOPE + MLA_V)[:, :, MLA_NOPE:]
    eye = jnp.asarray(np.eye(MLA_HEADS, dtype=np.float32))
    w_v_wide = (jnp.transpose(w_uv_h, (1, 0, 2))[:, :, None, :] * eye[:, None, :, None])
    w_v_wide = w_v_wide.reshape(MLA_HEADS, _MLA_KV_LORA, MLA_HEADS * MLA_V).astype(BF16)

    bias_f = jnp.zeros((1, LANE), F32).at[0, _LF_LANE:_LF_LANE + FOX_HEADS].set(b_forget.astype(F32))
    place_k = np.zeros((3 * LANE, _WIDE), np.float32)
    ones_q = np.zeros((1, _WIDE), np.float32)
    for h in range(FOX_HEADS):
        for part in range(3):
            place_k[part * LANE + _LF_LANE + h, h * HEAD_GROUP + FOX_HEAD_DIM + part] = -1.0
            ones_q[0, h * HEAD_GROUP + FOX_HEAD_DIM + part] = 1.0

    d_mla = MLA_HEADS * MLA_V
    return {
        "g_mix": norm_mix.reshape(1, d_model).astype(F32),
        "w_in": w_in_r,
        "bias_f": bias_f,
        "q_norm": mla_q_norm.reshape(1, -1).astype(F32),
        "w_uq2": w_uq2,
        "kv_norm": mla_kv_norm.reshape(1, -1).astype(F32),
        "w_kexp": w_kexp,
        "w_uv": w_uv,
        "place_k": jnp.asarray(place_k, BF16),
        "ones_q": jnp.asarray(ones_q),
        "w_abs": w_abs,
        "p_rope": jnp.asarray(p_rope, BF16),
        "w_v_wide": w_v_wide,
        "w_out_a": w_out[:d_mla].astype(BF16),
        "w_out_b": w_out[d_mla:].astype(BF16),
        "g_ffn": norm_ffn.reshape(1, d_model).astype(F32),
        "w_gate": w_ffn_gate.astype(BF16),
        "w_up": w_ffn_up.astype(BF16),
        "w_down": w_ffn_down.astype(BF16),
        "g_final": norm_final.reshape(1, d_model).astype(F32),
    }


def _rope_tables(pos):
    half = MLA_ROPE // 2
    inv_freq = ROPE_THETA ** (-jnp.arange(half, dtype=F32) / half)
    ang = pos.astype(F32)[:, None] * inv_freq[None, :]
    cos, sin = jnp.cos(ang), jnp.sin(ang)
    n = pos.shape[0]
    sc = MLA_SCALE * LOG2E
    zeros = lambda w: jnp.zeros((n, w), F32)
    tqc = jnp.concatenate([jnp.full((n, MLA_NOPE), sc, F32), sc * cos, sc * cos,
                           zeros(LANE - MLA_QK_DIM)], axis=1)
    tqs = jnp.concatenate([zeros(MLA_NOPE), -sc * sin, sc * sin, zeros(LANE - MLA_QK_DIM)], axis=1)
    tkc = jnp.concatenate([cos, cos, zeros(LANE - MLA_ROPE)], axis=1)
    tks = jnp.concatenate([-sin, sin, zeros(LANE - MLA_ROPE)], axis=1)
    return tqc, tqs, tkc, tks


def _pad_rows(a, rows):
    return jnp.concatenate([a, jnp.zeros((rows - a.shape[0],) + a.shape[1:], a.dtype)], axis=0)


PROJ_TILE = 256
ATTN_TQ = 2048
ATTN_TK = 512
FFN_TILE = 256
SAMPLE_TILE = 2048


def kernel(x_prompt, x_sample, cache_mla_latent, cache_mla_krope, cache_fox_k, cache_fox_v, cache_fox_logf, meta_tokens, norm_mix, w_in, b_forget, mla_q_norm, w_mla_uq, mla_kv_norm, w_mla_ukv, w_out, norm_ffn, w_ffn_gate, w_ffn_up, w_ffn_down, norm_final):
    depth = w_in.shape[0]
    assert depth == 1, "single-layer trunk: the meta rows' mixing output never reaches an output"
    B, seq, d_model = x_prompt.shape
    SB, s_new, _ = x_sample.shape
    n_meta = meta_tokens.shape[0]
    past = cache_mla_latent.shape[2]
    assert n_meta <= LANE

    w = _prep_weights(norm_mix[0], w_in[0], b_forget[0], mla_q_norm[0], w_mla_uq[0], mla_kv_norm[0],
                      w_mla_ukv[0], w_out[0], norm_ffn[0], w_ffn_gate[0], w_ffn_up[0], w_ffn_down[0],
                      norm_final)

    zero_carry = jnp.zeros((1, 1, LANE), F32)
    meta = _proj_call(_pad_rows(meta_tokens.astype(F32), LANE)[None], zero_carry,
                      _rope_tables(jnp.arange(LANE)), w, LANE, valid_rows=n_meta, transposed_v=True)
    (m_lat, m_kr, m_fk, m_fv, m_lf, _, m_km, m_vm, _, m_fka, m_fvb, m_carry) = meta
    frames = _proj_call(x_prompt, jnp.broadcast_to(m_carry, (B, 1, LANE)),
                        _rope_tables(n_meta + jnp.arange(seq)), w, PROJ_TILE, transposed_v=True,
                        row_offset=n_meta)
    f_qm, f_km, f_vm, f_fq, f_fka, f_fvb = frames[5:11]
    lat_p, kr_p, fk_p, fv_p, lf_p = _place_rows_call(
        [a[0, :n_meta] for a in (m_lat, m_kr, m_fk, m_fv, m_lf)],
        [a.reshape((B, n_meta + seq) + a.shape[1:]) for a in frames[:5]])

    o_mla = _attn_call(f_qm, f_km, f_vm, m_km[0], m_vm[0],
                       chunk_mask=True, n_meta=n_meta, tq=ATTN_TQ, tk=ATTN_TK, name="attn_mla")
    o_fox = _attn_call(f_fq, f_fka, f_fvb, m_fka[0], m_fvb[0],
                       chunk_mask=False, n_meta=n_meta, tq=ATTN_TQ, tk=ATTN_TK, name="attn_fox")
    y_prompt = _ffn_call(x_prompt.reshape(B * seq, d_model), o_mla.reshape(B * seq, -1),
                         o_fox.reshape(B * seq, -1), w, FFN_TILE).reshape(B, seq, d_model)

    lat_p, kr_p, lf_p = lat_p[None], kr_p[None], lf_p[None]
    fk_p = fk_p.reshape(1, B, n_meta + seq, FOX_HEADS, FOX_HEAD_DIM)
    fv_p = fv_p.reshape(1, B, n_meta + seq, FOX_HEADS, FOX_HEAD_DIM)

    logf_t = jnp.transpose(cache_fox_logf[0].astype(F32), (0, 2, 1))
    cum_cache = _cumsum_call(logf_t.reshape(SB * FOX_HEADS, past)).reshape(SB, FOX_HEADS, past)
    carry_s = jnp.zeros((SB, 1, LANE), F32).at[:, 0, _LF_LANE:_LF_LANE + FOX_HEADS].set(
        cum_cache[:, :, past - 1])
    samp = _proj_call(x_sample, carry_s, _rope_tables(past + jnp.arange(s_new)), w, s_new)
    (s_lat, s_kr, s_fk, s_fv, s_lf, s_qm, _, _, s_fq, s_fka, s_fvb, _) = samp

    positions_last = lambda a: jnp.moveaxis(a, 1, -1)
    so_mla = _sample_mla_call(s_qm, cache_mla_latent[0], positions_last(cache_mla_krope[0]), s_lat, s_kr, w,
                              SAMPLE_TILE)
    so_fox = _sample_fox_call(s_fq, positions_last(cache_fox_k[0]), positions_last(cache_fox_v[0]), cum_cache,
                              s_fka, s_fvb, SAMPLE_TILE)
    y_sample = _ffn_call(x_sample.reshape(SB * s_new, d_model), so_mla.reshape(SB * s_new, -1),
                         so_fox.reshape(SB * s_new, -1), w, FFN_TILE).reshape(SB, s_new, d_model)

    heads = lambda a: a.reshape(1, SB, s_new, FOX_HEADS, FOX_HEAD_DIM)
    return (y_prompt, y_sample, lat_p, kr_p, fk_p, fv_p, lf_p,
            s_lat[None], s_kr[None], heads(s_fk), heads(s_fv), s_lf[None])
```
